```python
import math
import jax
import jax.numpy as jnp
from jax import lax
import numpy as np

D_MODEL = 2048
BATCH = 16
SEQ = 2048
DEPTH = 2

CHUNK = 64
Q_BLOCK = 128
TOKEN_BLOCK = 128
EPS = 1e-6
HEAD_DIM = D_MODEL // 16

N_EVEN = (DEPTH + 1) // 2
N_ODD = DEPTH // 2

RET_HEADS = 8
RET_DK = HEAD_DIM
RET_DV = HEAD_DIM
DIFF_HEADS = 8
DIFF_DK = HEAD_DIM // 2
DIFF_DV = HEAD_DIM
EVEN_SIZES = (RET_HEADS * RET_DK, RET_HEADS * RET_DK, RET_HEADS * RET_DV, RET_HEADS * RET_DV,
              DIFF_HEADS * 2 * DIFF_DK, DIFF_HEADS * 2 * DIFF_DK, DIFF_HEADS * DIFF_DV)
EVEN_IN = sum(EVEN_SIZES)
EVEN_OUT = RET_HEADS * RET_DV + DIFF_HEADS * DIFF_DV

GDN_HEADS = 8
GDN_DK = HEAD_DIM
GDN_DV = HEAD_DIM
CONV_WIDTH = 4
CONV_CH = 2 * GDN_HEADS * GDN_DK + GDN_HEADS * GDN_DV
MLA_HEADS = 8
MLA_Q_RANK = 3 * D_MODEL // 16
MLA_KV_RANK = D_MODEL // 8
MLA_NOPE = HEAD_DIM
MLA_ROPE = HEAD_DIM // 2
MLA_DV = HEAD_DIM
ROPE_THETA = 10000.0
ODD_SIZES = (GDN_HEADS * GDN_DK, GDN_HEADS * GDN_DK, GDN_HEADS * GDN_DV, GDN_HEADS * GDN_DV,
             GDN_HEADS, GDN_HEADS, MLA_Q_RANK, MLA_KV_RANK, MLA_ROPE)
ODD_IN = sum(ODD_SIZES)
ODD_OUT = GDN_HEADS * GDN_DV + MLA_HEADS * MLA_DV

PEER_HEADS = 8
N_KEYS = 128
N_EXPERTS = N_KEYS * N_KEYS
PEER_DQ = D_MODEL // 8
PEER_TOPK = 16

kernel_name = 'hybrid_retention_diffattn_gdn_mla_peer'


def rmsnorm(x, gain):
    xf = x.astype(jnp.float32)
    xf = xf * lax.rsqrt(jnp.mean(xf * xf, axis=-1, keepdims=True) + EPS)
    return xf.astype(x.dtype) * gain


def l2norm(x):
    xf = x.astype(jnp.float32)
    return xf * lax.rsqrt(jnp.sum(xf * xf, axis=-1, keepdims=True) + EPS)


def split_cols(t, sizes):
    return jnp.split(t, [int(s) for s in np.cumsum(sizes)[:-1]], axis=-1)


def to_chunks(t):
    b, s, h = t.shape[:3]
    t = t.astype(jnp.float32).reshape((b, s // CHUNK, CHUNK, h) + t.shape[3:])
    return jnp.moveaxis(t, (1, 3), (0, 2))


def from_chunks(t):
    nc, b, h, cl = t.shape[:4]
    return jnp.moveaxis(t, (0, 2), (1, 3)).reshape((b, nc * cl, h) + t.shape[4:])


def chunk_causal_softmax(s, q_start):
    q_chunk = (q_start + jnp.arange(s.shape[-2])) // CHUNK
    k_chunk = jnp.arange(s.shape[-1]) // CHUNK
    mask = k_chunk[None, :] <= q_chunk[:, None]
    return jax.nn.softmax(jnp.where(mask, s.astype(jnp.float32), -jnp.inf), axis=-1)


def rope(x, positions):
    half = x.shape[-1] // 2
    inv_freq = ROPE_THETA ** (-jnp.arange(half, dtype=jnp.float32) / half)
    ang = positions.astype(jnp.float32)[:, None] * inv_freq[None, :]
    ang = ang.reshape(ang.shape[:1] + (1,) * (x.ndim - 3) + ang.shape[1:])
    cos = jnp.cos(ang).astype(x.dtype)
    sin = jnp.sin(ang).astype(x.dtype)
    x1, x2 = x[..., :half], x[..., half:]
    return jnp.concatenate([x1 * cos - x2 * sin, x2 * cos + x1 * sin], axis=-1)


def causal_conv(x, w):
    width, ch = w.shape
    return lax.conv_general_dilated(x, w[:, None, :], window_strides=(1,), padding=[(width - 1, 0)],
                                    dimension_numbers=('NWC', 'WIO', 'NWC'), feature_group_count=ch)


def retention(q, k, v, log_gamma):
    b, s, h, dk = q.shape
    dv = v.shape[-1]
    pos = jnp.arange(CHUNK, dtype=jnp.float32)
    lg = log_gamma.astype(jnp.float32)[:, None]
    intra = jnp.exp(lg[..., None] * jnp.abs(pos[:, None] - pos[None, :]))
    q_dec = jnp.exp(lg * (pos + 1.0))[..., None]
    k_dec = jnp.exp(lg * (CHUNK - 1.0 - pos))[..., None]
    chunk_dec = jnp.exp(lg * CHUNK)[..., None]

    def step(state, inp):
        qc, kc, vc = inp
        o = jnp.einsum('bhij,bhjd->bhid', jnp.einsum('bhid,bhjd->bhij', qc, kc) * intra, vc) \
            + (qc * q_dec) @ state
        state = state * chunk_dec + jnp.einsum('bhjd,bhje->bhde', kc * k_dec, vc)
        return state, o

    state0 = jnp.zeros((b, h, dk, dv), jnp.float32)
    _, o = lax.scan(step, state0, (to_chunks(q), to_chunks(k), to_chunks(v)))
    return from_chunks(o)


def diff_attention(q, k, v, lam):
    b, s, h, _, dk = q.shape
    scale = dk ** -0.5
    outs = []
    for qs in range(0, s, Q_BLOCK):
        ke = qs + Q_BLOCK
        sc = jnp.einsum('bqhmd,bkhmd->bhmqk', q[:, qs:ke], k[:, :ke]) * scale
        p = chunk_causal_softmax(sc, qs)
        w = (p[:, :, 0] - lam * p[:, :, 1]).astype(v.dtype)
        outs.append(jnp.einsum('bhqk,bkhd->bqhd', w, v[:, :ke]))
    return jnp.concatenate(outs, axis=1)


def gated_delta_rule(q, k, v, beta, g):
    b, s, h, dk = q.shape
    dv = v.shape[-1]
    q = to_chunks(q) * dk ** -0.5
    k, v, beta, g = to_chunks(k), to_chunks(v), to_chunks(beta), to_chunks(g)
    g = jnp.cumsum(g, axis=-1)
    tri = jnp.tril(jnp.ones((CHUNK, CHUNK), bool))
    strict = jnp.tril(jnp.ones((CHUNK, CHUNK), bool), -1)
    gdiff = g[..., :, None] - g[..., None, :]
    decay = jnp.where(tri, jnp.exp(jnp.where(tri, gdiff, 0.0)), 0.0)
    k_beta = k * beta[..., None]
    l_mat = jnp.where(strict, jnp.einsum('nbhid,nbhjd->nbhij', k_beta, k) * decay, 0.0)
    eye = jnp.eye(CHUNK, dtype=jnp.float32)
    t_inv = lax.linalg.triangular_solve(l_mat + eye, jnp.broadcast_to(eye, l_mat.shape),
                                        left_side=True, lower=True, unit_diagonal=True)
    u = t_inv @ (v * beta[..., None])
    w = t_inv @ (k_beta * jnp.exp(g)[..., None])
    intra = jnp.einsum('nbhid,nbhjd->nbhij', q, k) * decay

    def step(state, inp):
        qi, ki, ui, wi, gi, ai = inp
        v_new = ui - wi @ state
        o = (qi * jnp.exp(gi)[..., None]) @ state + ai @ v_new
        g_last = gi[..., -1:]
        state = state * jnp.exp(g_last)[..., None] \
            + jnp.einsum('bhcd,bhce->bhde', ki * jnp.exp(g_last - gi)[..., None], v_new)
        return state, o

    state0 = jnp.zeros((b, h, dk, dv), jnp.float32)
    _, o = lax.scan(step, state0, (q, k, u, w, g, intra))
    return from_chunks(o)


def mla_attention(q_nope, q_rope, k_nope, k_rope, v):
    b, s, h, _ = q_nope.shape
    scale = (MLA_NOPE + MLA_ROPE) ** -0.5
    outs = []
    for qs in range(0, s, Q_BLOCK):
        ke = qs + Q_BLOCK
        sc = (jnp.einsum('bqhd,bkhd->bhqk', q_nope[:, qs:ke], k_nope[:, :ke])
              + jnp.einsum('bqhr,bkr->bhqk', q_rope[:, qs:ke], k_rope[:, :ke])) * scale
        p = chunk_causal_softmax(sc, qs).astype(v.dtype)
        outs.append(jnp.einsum('bhqk,bkhd->bqhd', p, v[:, :ke]))
    return jnp.concatenate(outs, axis=1)


def mixer_ab(h, w_in, w_out, ret_norm, diff_lambda, diff_norm, lambda_init):
    b, s, _ = h.shape
    rq, rk, rv, rg, dq, dk, dv = split_cols(h @ w_in, EVEN_SIZES)
    log_gamma = jnp.log1p(-jnp.exp2(-5.0 - jnp.arange(RET_HEADS, dtype=jnp.float32)))
    ret = retention(rq.reshape(b, s, RET_HEADS, RET_DK), rk.reshape(b, s, RET_HEADS, RET_DK) * RET_DK ** -0.5,
                    rv.reshape(b, s, RET_HEADS, RET_DV), log_gamma).astype(h.dtype)
    ret = rmsnorm(ret, ret_norm) * jax.nn.silu(rg.reshape(b, s, RET_HEADS, RET_DV))
    lam = (jnp.exp(jnp.sum(diff_lambda[0] * diff_lambda[1]).astype(jnp.float32))
           - jnp.exp(jnp.sum(diff_lambda[2] * diff_lambda[3]).astype(jnp.float32)) + lambda_init)
    dif = diff_attention(dq.reshape(b, s, DIFF_HEADS, 2, DIFF_DK), dk.reshape(b, s, DIFF_HEADS, 2, DIFF_DK),
                         dv.reshape(b, s, DIFF_HEADS, DIFF_DV), lam)
    dif = rmsnorm(dif, diff_norm) * (1.0 - lambda_init)
    o = jnp.concatenate([ret.reshape(b, s, -1), dif.reshape(b, s, -1)], axis=-1)
    return o @ w_out


def mixer_cd(h, w_in, w_out, gdn_conv, gdn_a_log, gdn_dt_bias, gdn_norm,
             mla_q_norm, mla_w_uq, mla_kv_norm, mla_w_ukv):
    b, s, _ = h.shape
    gq, gk, gv, gg, gb, ga, cq, ckv, kr = split_cols(h @ w_in, ODD_SIZES)
    qkv = jax.nn.silu(causal_conv(jnp.concatenate([gq, gk, gv], axis=-1), gdn_conv))
    gq, gk, gv = split_cols(qkv, (GDN_HEADS * GDN_DK, GDN_HEADS * GDN_DK, GDN_HEADS * GDN_DV))
    beta = jax.nn.sigmoid(gb.astype(jnp.float32))
    g = -jnp.exp(gdn_a_log.astype(jnp.float32)) * jax.nn.softplus(ga.astype(jnp.float32)
                                                                  + gdn_dt_bias.astype(jnp.float32))
    gdn = gated_delta_rule(l2norm(gq.reshape(b, s, GDN_HEADS, GDN_DK)), l2norm(gk.reshape(b, s, GDN_HEADS, GDN_DK)),
                           gv.reshape(b, s, GDN_HEADS, GDN_DV), beta, g).astype(h.dtype)
    gdn = rmsnorm(gdn, gdn_norm) * jax.nn.silu(gg.reshape(b, s, GDN_HEADS, GDN_DV))
    pos = jnp.arange(s, dtype=jnp.int32)
    qf = (rmsnorm(cq, mla_q_norm) @ mla_w_uq).reshape(b, s, MLA_HEADS, MLA_NOPE + MLA_ROPE)
    q_nope, q_rope = qf[..., :MLA_NOPE], rope(qf[..., MLA_NOPE:], pos)
    kv = (rmsnorm(ckv, mla_kv_norm) @ mla_w_ukv).reshape(b, s, MLA_HEADS, MLA_NOPE + MLA_DV)
    k_nope, v = kv[..., :MLA_NOPE], kv[..., MLA_NOPE:]
    mla = mla_attention(q_nope, q_rope, k_nope, rope(kr, pos), v)
    o = jnp.concatenate([gdn.reshape(b, s, -1), mla.reshape(b, s, -1)], axis=-1)
    return o @ w_out


def peer(h, w_query, sub_keys, expert_down, expert_up):
    b, s, d = h.shape
    q = (h @ w_query).reshape(b, s, PEER_HEADS, 2, PEER_DQ // 2)
    sc = jnp.einsum('bshpd,pnd->bshpn', q, sub_keys).astype(jnp.float32)
    top_s, top_i = lax.top_k(sc, PEER_TOPK)
    cand_s = top_s[..., 0, :, None] + top_s[..., 1, None, :]
    cand_i = top_i[..., 0, :, None] * N_KEYS + top_i[..., 1, None, :]
    best_s, best_pos = lax.top_k(cand_s.reshape(b, s, PEER_HEADS, PEER_TOPK * PEER_TOPK), PEER_TOPK)
    idx = jnp.take_along_axis(cand_i.reshape(b, s, PEER_HEADS, PEER_TOPK * PEER_TOPK), best_pos, axis=-1)
    gates = jax.nn.softmax(best_s, axis=-1).astype(h.dtype)
    n_blk = (b * s) // TOKEN_BLOCK
    hb = h.reshape(n_blk, TOKEN_BLOCK, d)
    ib = idx.reshape(n_blk, TOKEN_BLOCK, PEER_HEADS * PEER_TOPK)
    gb = gates.reshape(n_blk, TOKEN_BLOCK, PEER_HEADS * PEER_TOPK)

    def apply(args):
        ht, it, gt = args
        act = jax.nn.gelu(jnp.einsum('td,ted->te', ht, expert_down[it]), approximate=False)
        return jnp.einsum('te,ted->td', gt * act, expert_up[it])

    return lax.map(apply, (hb, ib, gb)).reshape(b, s, d)


def setup_inputs(seed: int = 0) -> dict:
    key = jax.random.key(seed)
    ks = jax.random.split(key, 32)

    def nrm(i, shape, scale):
        return jax.random.normal(ks[i], shape, jnp.float32) * scale

    def gain(i, shape):
        return 1.0 + nrm(i, shape, 0.02)

    dt = jnp.exp(jax.random.uniform(ks[26], (N_ODD, GDN_HEADS), jnp.float32, math.log(1e-3), math.log(1e-1)))
    a_init = jax.random.uniform(ks[27], (N_ODD, GDN_HEADS), jnp.float32, 1.0, 16.0)
    return {
        'x': nrm(0, (BATCH, SEQ, D_MODEL), 1.0),
        'c': nrm(1, (BATCH, D_MODEL), 1.0),
        'norm_mix': gain(2, (DEPTH, D_MODEL)),
        'norm_ffn': gain(3, (DEPTH, D_MODEL)),
        'ada_w': nrm(4, (DEPTH, D_MODEL, 6 * D_MODEL), 0.5 * D_MODEL ** -0.5),
        'ada_b': nrm(5, (DEPTH, 6 * D_MODEL), 0.02),
        'even_w_in': nrm(6, (N_EVEN, D_MODEL, EVEN_IN), D_MODEL ** -0.5),
        'even_w_out': nrm(7, (N_EVEN, EVEN_OUT, D_MODEL), EVEN_OUT ** -0.5),
        'ret_norm': gain(8, (N_EVEN, RET_DV)),
        'diff_lambda': nrm(9, (N_EVEN, 4, DIFF_DK), 0.1),
        'diff_norm': gain(10, (N_EVEN, DIFF_DV)),
        'odd_w_in': nrm(11, (N_ODD, D_MODEL, ODD_IN), D_MODEL ** -0.5),
        'odd_w_out': nrm(12, (N_ODD, ODD_OUT, D_MODEL), ODD_OUT ** -0.5),
        'gdn_conv': nrm(13, (N_ODD, CONV_WIDTH, CONV_CH), CONV_WIDTH ** -0.5),
        'gdn_a_log': jnp.log(a_init),
        'gdn_dt_bias': dt + jnp.log(-jnp.expm1(-dt)),
        'gdn_norm': gain(14, (N_ODD, GDN_DV)),
        'mla_q_norm': gain(15, (N_ODD, MLA_Q_RANK)),
        'mla_w_uq': nrm(16, (N_ODD, MLA_Q_RANK, MLA_HEADS * (MLA_NOPE + MLA_ROPE)), MLA_Q_RANK ** -0.5),
        'mla_kv_norm': gain(17, (N_ODD, MLA_KV_RANK)),
        'mla_w_ukv': nrm(18, (N_ODD, MLA_KV_RANK, MLA_HEADS * (MLA_NOPE + MLA_DV)), MLA_KV_RANK ** -0.5),
        'peer_w_query': nrm(19, (DEPTH, D_MODEL, PEER_HEADS * PEER_DQ), D_MODEL ** -0.5),
        'peer_sub_keys': nrm(20, (DEPTH, 2, N_KEYS, PEER_DQ // 2), (PEER_DQ // 2) ** -0.5),
        'peer_down': nrm(21, (DEPTH, N_EXPERTS, D_MODEL), D_MODEL ** -0.5),
        'peer_up': nrm(22, (DEPTH, N_EXPERTS, D_MODEL), 1.0),
        'final_norm': gain(23, (D_MODEL,)),
    }


def reference(x, c, norm_mix, norm_ffn, ada_w, ada_b, even_w_in, even_w_out, ret_norm, diff_lambda, diff_norm,
              odd_w_in, odd_w_out, gdn_conv, gdn_a_log, gdn_dt_bias, gdn_norm, mla_q_norm, mla_w_uq,
              mla_kv_norm, mla_w_ukv, peer_w_query, peer_sub_keys, peer_down, peer_up, final_norm):
    cond = jax.nn.silu(c)
    for i in range(DEPTH):
        mod = (cond @ ada_w[i] + ada_b[i])[:, None, :]
        sh_m, sc_m, gt_m, sh_f, sc_f, gt_f = jnp.split(mod, 6, axis=-1)
        h = rmsnorm(x, norm_mix[i]) * (1.0 + sc_m) + sh_m
        j = i // 2
        if i % 2 == 0:
            y = mixer_ab(h, even_w_in[j], even_w_out[j], ret_norm[j], diff_lambda[j], diff_norm[j],
                         0.8 - 0.6 * math.exp(-0.3 * i))
        else:
            y = mixer_cd(h, odd_w_in[j], odd_w_out[j], gdn_conv[j], gdn_a_log[j], gdn_dt_bias[j], gdn_norm[j],
                         mla_q_norm[j], mla_w_uq[j], mla_kv_norm[j], mla_w_ukv[j])
        x = x + gt_m * y
        h = rmsnorm(x, norm_ffn[i]) * (1.0 + sc_f) + sh_f
        x = x + gt_f * peer(h, peer_w_query[i], peer_sub_keys[i], peer_down[i], peer_up[i])
    return rmsnorm(x, final_norm)
```

```python
import functools
import math

import jax
import jax.numpy as jnp
import numpy as np
from jax import lax
from jax.experimental import pallas as pl
from jax.experimental.pallas import tpu as pltpu

F32 = jnp.float32
BF16 = jnp.bfloat16

CHUNK = 64
EPS = 1e-6
HEAD_DIM = 128
N_HEADS = 8
ROPE_THETA = 10000.0
MLA_Q_RANK = 384
MLA_KV_RANK = 256
MLA_ROPE = 64
N_KEYS = 128
PEER_HEADS = 8
PEER_TOPK = 16
PEER_SLOTS = PEER_HEADS * PEER_TOPK

LANES = 128
SUBLANES = 8
VMEM_LIMIT_BYTES = 56 * 1024 * 1024

EXPERT_TILE = 4096
N_EXPERT_TILES = (N_KEYS * N_KEYS) // EXPERT_TILE
EXPERT_TILE_SHIFT = int(math.log2(EXPERT_TILE))
SLOT_GROUP = 8

NEG_INF = float("-inf")


def _cparams(*sem):
    return pltpu.CompilerParams(dimension_semantics=sem, vmem_limit_bytes=VMEM_LIMIT_BYTES)


def _mod_kernel(c_ref, w_ref, b_ref, o_ref):
    c = c_ref[...]
    cond = c * jax.nn.sigmoid(c)
    o_ref[0] = jnp.dot(cond, w_ref[0], preferred_element_type=F32) + b_ref[0]


def modulation(c, ada_w, ada_b):
    depth, d, n = ada_w.shape
    b = c.shape[0]
    tn = 1536
    return pl.pallas_call(
        _mod_kernel,
        grid=(depth, n // tn),
        in_specs=[pl.BlockSpec((b, d), lambda l, j: (0, 0)),
                  pl.BlockSpec((1, d, tn), lambda l, j: (l, 0, j)),
                  pl.BlockSpec((1, 1, tn), lambda l, j: (l, 0, j))],
        out_specs=pl.BlockSpec((1, b, tn), lambda l, j: (l, 0, j)),
        out_shape=jax.ShapeDtypeStruct((depth, b, n), F32),
        compiler_params=_cparams("parallel", "parallel"),
        name="modulation",
    )(c, ada_w, ada_b.reshape(depth, 1, n))


def _norm_matmul_kernel(x_ref, g_ref, sc_ref, sh_ref, w_ref, *rest, emit_h):
    if emit_h:
        y_ref, h_ref, h_scr = rest
    else:
        y_ref, h_scr = rest

    @pl.when(pl.program_id(1) == 0)
    def _():
        x = x_ref[...]
        xn = x * lax.rsqrt(jnp.mean(x * x, axis=-1, keepdims=True) + EPS)
        h = xn * g_ref[...] * (1.0 + sc_ref[0]) + sh_ref[0]
        h_scr[...] = h.astype(BF16)
        if emit_h:
            h_ref[...] = h

    y_ref[...] = jnp.dot(h_scr[...], w_ref[...], preferred_element_type=F32)


def norm_matmul(x2, gain, scale, shift, w_bf16, seq, *, emit_h=False, tm=1024, tn=512):
    t, d = x2.shape
    n = w_bf16.shape[1]
    tm = min(tm, seq)
    assert seq % tm == 0 and n % tn == 0
    per_seq = seq // tm
    bvec = lambda i, j: (i // per_seq, 0, 0)
    out_shape = [jax.ShapeDtypeStruct((t, n), F32)]
    out_specs = [pl.BlockSpec((tm, tn), lambda i, j: (i, j))]
    if emit_h:
        out_shape.append(jax.ShapeDtypeStruct((t, d), F32))
        out_specs.append(pl.BlockSpec((tm, d), lambda i, j: (i, 0)))
    res = pl.pallas_call(
        functools.partial(_norm_matmul_kernel, emit_h=emit_h),
        grid=(t // tm, n // tn),
        in_specs=[pl.BlockSpec((tm, d), lambda i, j: (i, 0)),
                  pl.BlockSpec((1, d), lambda i, j: (0, 0)),
                  pl.BlockSpec((1, 1, d), bvec),
                  pl.BlockSpec((1, 1, d), bvec),
                  pl.BlockSpec((d, tn), lambda i, j: (0, j))],
        out_specs=out_specs,
        out_shape=out_shape,
        scratch_shapes=[pltpu.VMEM((tm, d), BF16)],
        compiler_params=_cparams("parallel", "arbitrary"),
        name="norm_matmul",
    )(x2, gain.reshape(1, d), scale[:, None, :], shift[:, None, :], w_bf16)
    return res if emit_h else res[0]


def _out_proj_kernel(a1_ref, a2_ref, w1_ref, w2_ref, x_ref, gt_ref, o_ref):
    y = jnp.dot(a1_ref[...], w1_ref[...], preferred_element_type=F32)
    y = y + jnp.dot(a2_ref[...], w2_ref[...], preferred_element_type=F32)
    o_ref[...] = x_ref[...] + gt_ref[0] * y


def out_proj_residual(a1, a2, w_bf16, x2, gate, seq, *, tm=1024, tn=512):
    t, k1 = a1.shape
    k2 = a2.shape[1]
    d = x2.shape[1]
    tm = min(tm, seq)
    per_seq = seq // tm
    return pl.pallas_call(
        _out_proj_kernel,
        grid=(t // tm, d // tn),
        in_specs=[pl.BlockSpec((tm, k1), lambda i, j: (i, 0)),
                  pl.BlockSpec((tm, k2), lambda i, j: (i, 0)),
                  pl.BlockSpec((k1, tn), lambda i, j: (0, j)),
                  pl.BlockSpec((k2, tn), lambda i, j: (0, j)),
                  pl.BlockSpec((tm, tn), lambda i, j: (i, j)),
                  pl.BlockSpec((1, 1, tn), lambda i, j: (i // per_seq, 0, j))],
        out_specs=pl.BlockSpec((tm, tn), lambda i, j: (i, j)),
        out_shape=jax.ShapeDtypeStruct((t, d), F32),
        compiler_params=_cparams("parallel", "parallel"),
        name="out_proj_residual",
    )(a1, a2, w_bf16[:k1], w_bf16[k1:], x2, gate[:, None, :])


def _dot_nt(a, b):
    return lax.dot_general(a, b, (((1,), (1,)), ((), ())), preferred_element_type=F32)


def _dot_tn(a, b):
    return lax.dot_general(a, b, (((0,), (0,)), ((), ())), preferred_element_type=F32)


def _retention_kernel(q_ref, k_ref, v_ref, g_ref, intra_ref, qdec_ref, kdec_ref, cdec_ref, norm_ref,
                      o_ref, state):
    @pl.when(pl.program_id(2) == 0)
    def _():
        state[...] = jnp.zeros_like(state)

    n_chunks = q_ref.shape[0] // CHUNK
    intra = intra_ref[0]
    qdec = qdec_ref[0]
    kdec = kdec_ref[0]
    cdec = cdec_ref[0]
    for c in range(n_chunks):
        rows = pl.ds(c * CHUNK, CHUNK)
        qc = q_ref[rows, :]
        kc = k_ref[rows, :] * (HEAD_DIM ** -0.5)
        vc = v_ref[rows, :]
        st = state[...]
        s = _dot_nt(qc, kc) * intra
        o = jnp.dot(s, vc, preferred_element_type=F32) + jnp.dot(qc * qdec, st, preferred_element_type=F32)
        state[...] = st * cdec + _dot_tn(kc * kdec, vc)
        on = o * lax.rsqrt(jnp.mean(o * o, axis=-1, keepdims=True) + EPS) * norm_ref[...]
        g = g_ref[rows, :]
        o_ref[rows, :] = (on * (g * jax.nn.sigmoid(g))).astype(o_ref.dtype)


def retention_mixer(proj, ret_norm, batch, seq, *, ts=512):
    t = proj.shape[0]
    ts = min(ts, seq)
    per_seq = seq // ts
    h8 = N_HEADS
    pos = jnp.arange(CHUNK, dtype=F32)
    lg = jnp.log1p(-jnp.exp2(-5.0 - jnp.arange(h8, dtype=F32)))[:, None]
    intra = jnp.exp(lg[..., None] * jnp.abs(pos[:, None] - pos[None, :]))
    qdec = jnp.broadcast_to(jnp.exp(lg * (pos + 1.0))[..., None], (h8, CHUNK, HEAD_DIM))
    kdec = jnp.broadcast_to(jnp.exp(lg * (CHUNK - 1.0 - pos))[..., None], (h8, CHUNK, HEAD_DIM))
    cdec = jnp.broadcast_to(jnp.exp(lg * CHUNK)[..., None], (h8, HEAD_DIM, HEAD_DIM))

    def col(off):
        return pl.BlockSpec((ts, HEAD_DIM), lambda b, h, s, off=off: (b * per_seq + s, off + h))

    hspec = lambda shape: pl.BlockSpec((1,) + shape, lambda b, h, s: (h, 0, 0))
    return pl.pallas_call(
        _retention_kernel,
        grid=(batch, h8, per_seq),
        in_specs=[col(0), col(h8), col(2 * h8), col(3 * h8),
                  hspec((CHUNK, CHUNK)), hspec((CHUNK, HEAD_DIM)), hspec((CHUNK, HEAD_DIM)),
                  hspec((HEAD_DIM, HEAD_DIM)),
                  pl.BlockSpec((1, HEAD_DIM), lambda b, h, s: (0, 0))],
        out_specs=pl.BlockSpec((ts, HEAD_DIM), lambda b, h, s: (b * per_seq + s, h)),
        out_shape=jax.ShapeDtypeStruct((t, h8 * HEAD_DIM), BF16),
        scratch_shapes=[pltpu.VMEM((HEAD_DIM, HEAD_DIM), F32)],
        compiler_params=_cparams("parallel", "parallel", "arbitrary"),
        name="retention",
    )(proj, proj, proj, proj, intra, qdec, kdec, cdec, ret_norm.reshape(1, HEAD_DIM))


def _chunk_mask(q0, k0, tq, tk):
    qc = (q0 + lax.broadcasted_iota(jnp.int32, (tq, tk), 0)) // CHUNK
    kc = (k0 + lax.broadcasted_iota(jnp.int32, (tq, tk), 1)) // CHUNK
    return kc <= qc


def _online_step(s, v, m, l, acc):
    m_new = jnp.maximum(m, jnp.max(s, axis=-1, keepdims=True))
    alpha = jnp.exp(m - m_new)
    p = jnp.exp(s - m_new)
    l = alpha * l + jnp.sum(p, axis=-1, keepdims=True)
    acc = alpha * acc + jnp.dot(p, v, preferred_element_type=F32)
    return m_new, l, acc


def _diff_attn_kernel(lam_ref, q_ref, k_ref, v_ref, norm_ref, o_ref, *, tq, tk, lambda_init):
    qi = pl.program_id(2)
    q = q_ref[...]
    half = lax.broadcasted_iota(jnp.int32, q.shape, 1) < (HEAD_DIM // 2)
    q0 = jnp.where(half, q, 0.0)
    q1 = jnp.where(half, 0.0, q)
    scale = (HEAD_DIM // 2) ** -0.5
    n_kb = (qi + 1) * (tq // tk)

    def step(kb, carry, masked):
        m0, l0, a0, m1, l1, a1 = carry
        rows = pl.ds(pl.multiple_of(kb * tk, tk), tk)
        k = k_ref[rows, :]
        v = v_ref[rows, :]
        s0 = _dot_nt(q0, k) * scale
        s1 = _dot_nt(q1, k) * scale
        if masked:
            mask = _chunk_mask(qi * tq, kb * tk, tq, tk)
            s0 = jnp.where(mask, s0, NEG_INF)
            s1 = jnp.where(mask, s1, NEG_INF)
        m0, l0, a0 = _online_step(s0, v, m0, l0, a0)
        m1, l1, a1 = _online_step(s1, v, m1, l1, a1)
        return m0, l0, a0, m1, l1, a1

    init = (jnp.full((tq, 1), NEG_INF, F32), jnp.zeros((tq, 1), F32), jnp.zeros((tq, HEAD_DIM), F32)) * 2
    n_diag = tq // tk
    carry = lax.fori_loop(0, n_kb - n_diag, functools.partial(step, masked=False), init)
    for d in range(n_diag):
        carry = step(n_kb - n_diag + d, carry, True)
    m0, l0, a0, m1, l1, a1 = carry
    dl = lam_ref[...]
    lam = (jnp.exp(jnp.sum(dl[0:1] * dl[1:2], axis=-1, keepdims=True))
           - jnp.exp(jnp.sum(dl[2:3] * dl[3:4], axis=-1, keepdims=True)) + lambda_init)
    o = a0 / l0 - lam * (a1 / l1)
    on = o * lax.rsqrt(jnp.mean(o * o, axis=-1, keepdims=True) + EPS) * norm_ref[...]
    o_ref[...] = (on * (1.0 - lambda_init)).astype(o_ref.dtype)


def diff_attention_mixer(proj, diff_lambda, diff_norm, lambda_init, batch, seq, *, tq=256, tk=128):
    t = proj.shape[0]
    tq = min(tq, seq)
    per_seq = seq // tq
    kv = lambda off: pl.BlockSpec((seq, HEAD_DIM), lambda b, h, s, off=off: (b, off + h))
    return pl.pallas_call(
        functools.partial(_diff_attn_kernel, tq=tq, tk=tk, lambda_init=lambda_init),
        grid=(batch, N_HEADS, per_seq),
        in_specs=[pl.BlockSpec(diff_lambda.shape, lambda b, h, s: (0, 0)),
                  pl.BlockSpec((tq, HEAD_DIM), lambda b, h, s: (b * per_seq + s, 4 * N_HEADS + h)),
                  kv(5 * N_HEADS), kv(6 * N_HEADS),
                  pl.BlockSpec((1, HEAD_DIM), lambda b, h, s: (0, 0))],
        out_specs=pl.BlockSpec((tq, HEAD_DIM), lambda b, h, s: (b * per_seq + s, h)),
        out_shape=jax.ShapeDtypeStruct((t, N_HEADS * HEAD_DIM), BF16),
        compiler_params=_cparams("parallel", "parallel", "arbitrary"),
        name="diff_attention",
    )(diff_lambda, proj, proj, proj, diff_norm.reshape(1, HEAD_DIM))


def _split_mod(mod):
    return jnp.split(mod, 6, axis=-1)


def even_layer_mixer(x2, mod, p, layer, j, batch, seq):
    sh_m, sc_m, gt_m, _, _, _ = _split_mod(mod)
    proj = norm_matmul(x2, p["norm_mix"][layer], sc_m, sh_m, p["even_w_in"][j].astype(BF16), seq)
    ret = retention_mixer(proj, p["ret_norm"][j], batch, seq)
    lambda_init = 0.8 - 0.6 * math.exp(-0.3 * layer)
    dif = diff_attention_mixer(proj, p["diff_lambda"][j], p["diff_norm"][j], lambda_init, batch, seq)
    return out_proj_residual(ret, dif, p["even_w_out"][j].astype(BF16), x2, gt_m, seq)


ODD_COLS = 40 * LANES
GATE_BLOCK = 4 * N_HEADS


def _odd_w_in_layout(w):
    d = w.shape[0]
    hd = N_HEADS * HEAD_DIM
    zeros = lambda n: jnp.zeros((d, n), w.dtype)
    o = 4 * hd
    gates = w[:, o:o + 2 * N_HEADS]
    o += 2 * N_HEADS
    cq = w[:, o:o + MLA_Q_RANK]
    o += MLA_Q_RANK
    ckv = w[:, o:o + MLA_KV_RANK]
    o += MLA_KV_RANK
    kr = w[:, o:o + MLA_ROPE]
    return jnp.concatenate([w[:, :4 * hd], gates, zeros(LANES - 2 * N_HEADS), cq, ckv, kr,
                            zeros(LANES - MLA_ROPE), zeros(LANES)], axis=1)


def _silu(x):
    return x * jax.nn.sigmoid(x)


def _gdn_conv_kernel(x_ref, w_ref, o_ref):
    c = pl.program_id(1)
    x = x_ref[...]
    w = w_ref[...]
    width = w.shape[0]
    row = lax.broadcasted_iota(jnp.int32, x.shape, 0)
    y = x * w[width - 1:width]
    for sft in range(1, width):
        xs = jnp.where(row >= sft, pltpu.roll(x, sft, axis=0), 0.0)
        y = y + xs * w[width - 1 - sft:width - sft]
    y = _silu(y)
    yn = y * lax.rsqrt(jnp.sum(y * y, axis=-1, keepdims=True) + EPS)
    yn = yn * jnp.where(c < N_HEADS, HEAD_DIM ** -0.5, 1.0)
    o_ref[...] = jnp.where(c < 2 * N_HEADS, yn, y)


def gdn_conv(proj, conv_w, batch, seq):
    t = proj.shape[0]
    n_blk = 3 * N_HEADS
    return pl.pallas_call(
        _gdn_conv_kernel,
        grid=(batch, n_blk),
        in_specs=[pl.BlockSpec((seq, LANES), lambda b, c: (b, c)),
                  pl.BlockSpec((conv_w.shape[0], LANES), lambda b, c: (0, c))],
        out_specs=pl.BlockSpec((seq, LANES), lambda b, c: (b, c)),
        out_shape=jax.ShapeDtypeStruct((t, n_blk * LANES), F32),
        compiler_params=_cparams("parallel", "parallel"),
        name="gdn_conv",
    )(proj, conv_w)


def _dot_hi(a, b):
    return jnp.dot(a, b, preferred_element_type=F32, precision=lax.Precision.HIGHEST)


def _unit_lower_inverse(l_mat):
    n = l_mat.shape[0]
    eye = (lax.broadcasted_iota(jnp.int32, (n, n), 0) == lax.broadcasted_iota(jnp.int32, (n, n), 1)).astype(F32)
    inv = eye - l_mat
    power = _dot_hi(l_mat, l_mat)
    span = 2
    while span < n:
        inv = inv + _dot_hi(inv, power)
        span *= 2
        if span < n:
            power = _dot_hi(power, power)
    return inv


def _pick_lane(x, lane):
    sel = lax.broadcasted_iota(jnp.int32, x.shape, 1) == lane
    return jnp.sum(jnp.where(sel, x, 0.0), axis=-1, keepdims=True)


def _gdn_kernel(q_ref, k_ref, v_ref, gg_ref, gate_ref, alog_ref, dtb_ref, norm_ref, o_ref, state):
    h = pl.program_id(1)

    @pl.when(pl.program_id(2) == 0)
    def _():
        state[...] = jnp.zeros_like(state)

    n_chunks = q_ref.shape[0] // CHUNK
    ri = lax.broadcasted_iota(jnp.int32, (CHUNK, CHUNK), 0)
    ci = lax.broadcasted_iota(jnp.int32, (CHUNK, CHUNK), 1)
    tri = ci <= ri
    strict = ci < ri
    tri_f = tri.astype(F32)
    a_coef = -jnp.exp(_pick_lane(alog_ref[...], h))
    dt_bias = _pick_lane(dtb_ref[...], h)
    for c in range(n_chunks):
        rows = pl.ds(c * CHUNK, CHUNK)
        q = q_ref[rows, :]
        k = k_ref[rows, :]
        v = v_ref[rows, :]
        raw = gate_ref[rows, :]
        beta = jax.nn.sigmoid(_pick_lane(raw, h))
        ga = _pick_lane(raw, N_HEADS + h) + dt_bias
        softplus = jnp.maximum(ga, 0.0) + jnp.log1p(jnp.exp(-jnp.abs(ga)))
        g = jnp.broadcast_to(a_coef * softplus, (CHUNK, HEAD_DIM))
        gc = _dot_hi(tri_f, g)
        g_rows = gc.T[:CHUNK, :]
        gdiff = gc[:, :CHUNK] - g_rows
        decay = jnp.where(tri, jnp.exp(jnp.where(tri, gdiff, 0.0)), 0.0)
        k_beta = k * beta
        l_mat = jnp.where(strict, _dot_nt(k_beta, k) * decay, 0.0)
        t_inv = _unit_lower_inverse(l_mat)
        eg = jnp.exp(gc)
        u = jnp.dot(t_inv, v * beta, preferred_element_type=F32)
        w = jnp.dot(t_inv, k_beta * eg, preferred_element_type=F32)
        intra = _dot_nt(q, k) * decay
        st = state[...]
        v_new = u - jnp.dot(w, st, preferred_element_type=F32)
        o = jnp.dot(q * eg, st, preferred_element_type=F32) + jnp.dot(intra, v_new, preferred_element_type=F32)
        g_last = gc[CHUNK - 1:CHUNK, :]
        state[...] = st * jnp.exp(g_last) + _dot_tn(k * jnp.exp(g_last - gc), v_new)
        on = o * lax.rsqrt(jnp.mean(o * o, axis=-1, keepdims=True) + EPS) * norm_ref[...]
        o_ref[rows, :] = (on * _silu(gg_ref[rows, :])).astype(o_ref.dtype)


def gdn_mixer(qkv, proj, a_log, dt_bias, gdn_norm, batch, seq, *, ts=512):
    t = qkv.shape[0]
    ts = min(ts, seq)
    per_seq = seq // ts
    pad = lambda v: jnp.pad(v.reshape(1, -1), ((0, 0), (0, LANES - v.shape[0])))

    def col(off):
        return pl.BlockSpec((ts, HEAD_DIM), lambda b, h, s, off=off: (b * per_seq + s, off + h))

    row1 = pl.BlockSpec((1, LANES), lambda b, h, s: (0, 0))
    return pl.pallas_call(
        _gdn_kernel,
        grid=(batch, N_HEADS, per_seq),
        in_specs=[col(0), col(N_HEADS), col(2 * N_HEADS), col(3 * N_HEADS),
                  pl.BlockSpec((ts, LANES), lambda b, h, s: (b * per_seq + s, GATE_BLOCK)),
                  row1, row1, row1],
        out_specs=pl.BlockSpec((ts, HEAD_DIM), lambda b, h, s: (b * per_seq + s, h)),
        out_shape=jax.ShapeDtypeStruct((t, N_HEADS * HEAD_DIM), BF16),
        scratch_shapes=[pltpu.VMEM((HEAD_DIM, HEAD_DIM), F32)],
        compiler_params=_cparams("parallel", "parallel", "arbitrary"),
        name="gdn",
    )(qkv, qkv, qkv, proj, proj, pad(a_log), pad(dt_bias), gdn_norm.reshape(1, HEAD_DIM))


def _rope_tables(seq):
    half = MLA_ROPE // 2
    inv_freq = ROPE_THETA ** (-jnp.arange(half, dtype=F32) / half)
    ang = jnp.arange(seq, dtype=jnp.int32).astype(F32)[:, None] * inv_freq[None, :]
    cos, sin = jnp.cos(ang), jnp.sin(ang)
    z = jnp.zeros((seq, LANES - MLA_ROPE), F32)
    zh = jnp.zeros((seq, half), F32)
    return (jnp.concatenate([cos, cos, z], axis=1),
            jnp.concatenate([zh, sin, z], axis=1),
            jnp.concatenate([-sin, zh, z], axis=1))


def _rope(y, cos, sin_up, sin_dn):
    half = MLA_ROPE // 2
    return y * cos + pltpu.roll(y, half, axis=1) * sin_up + pltpu.roll(y, LANES - half, axis=1) * sin_dn


def _mla_proj_kernel(cq_ref, ckv_ref, kr_ref, qg_ref, kvg_ref, wqn_ref, wqr_ref, wkv_ref, cos_ref, sup_ref, sdn_ref,
                     qn_ref, qr_ref, kv_ref, kro_ref):
    def rms(x, g):
        return x * lax.rsqrt(jnp.mean(x * x, axis=-1, keepdims=True) + EPS) * g

    cos, sup, sdn = cos_ref[...], sup_ref[...], sdn_ref[...]
    cq = rms(cq_ref[...], qg_ref[...]).astype(BF16)
    qn_ref[...] = jnp.dot(cq, wqn_ref[...], preferred_element_type=F32).astype(qn_ref.dtype)
    qr = jnp.dot(cq, wqr_ref[...], preferred_element_type=F32)
    for h in range(N_HEADS):
        cols = slice(h * LANES, (h + 1) * LANES)
        qr_ref[:, cols] = _rope(qr[:, cols], cos, sup, sdn).astype(qr_ref.dtype)
    ckv = rms(ckv_ref[...], kvg_ref[...]).astype(BF16)
    kv_ref[...] = jnp.dot(ckv, wkv_ref[...], preferred_element_type=F32).astype(kv_ref.dtype)
    kro_ref[...] = _rope(kr_ref[...], cos, sup, sdn).astype(kro_ref.dtype)


def mla_projection(proj, q_norm, w_uq, kv_norm, w_ukv, seq, *, tm=512):
    t = proj.shape[0]
    tm = min(tm, seq)
    per_seq = seq // tm
    hd = N_HEADS * HEAD_DIM
    w3 = w_uq.reshape(MLA_Q_RANK, N_HEADS, HEAD_DIM + MLA_ROPE)
    wqn = w3[:, :, :HEAD_DIM].reshape(MLA_Q_RANK, hd).astype(BF16)
    wqr = jnp.pad(w3[:, :, HEAD_DIM:], ((0, 0), (0, 0), (0, LANES - MLA_ROPE))).reshape(MLA_Q_RANK, hd).astype(BF16)
    cos, sup, sdn = _rope_tables(seq)
    full = lambda a: pl.BlockSpec(a.shape, lambda i: (0,) * a.ndim)
    tab = pl.BlockSpec((tm, LANES), lambda i: (i % per_seq, 0))
    wkv = w_ukv.astype(BF16)
    qg, kvg = q_norm.reshape(1, -1), kv_norm.reshape(1, -1)
    return pl.pallas_call(
        _mla_proj_kernel,
        grid=(t // tm,),
        in_specs=[pl.BlockSpec((tm, MLA_Q_RANK), lambda i: (i, (GATE_BLOCK + 1) * LANES // MLA_Q_RANK)),
                  pl.BlockSpec((tm, MLA_KV_RANK), lambda i: (i, (GATE_BLOCK + 4) * LANES // MLA_KV_RANK)),
                  pl.BlockSpec((tm, LANES), lambda i: (i, GATE_BLOCK + 6)),
                  full(qg), full(kvg), full(wqn), full(wqr), full(wkv), tab, tab, tab],
        out_specs=[pl.BlockSpec((tm, hd), lambda i: (i, 0)), pl.BlockSpec((tm, hd), lambda i: (i, 0)),
                   pl.BlockSpec((tm, 2 * hd), lambda i: (i, 0)), pl.BlockSpec((tm, LANES), lambda i: (i, 0))],
        out_shape=[jax.ShapeDtypeStruct((t, hd), BF16), jax.ShapeDtypeStruct((t, hd), BF16),
                   jax.ShapeDtypeStruct((t, 2 * hd), BF16), jax.ShapeDtypeStruct((t, LANES), BF16)],
        compiler_params=_cparams("parallel"),
        name="mla_projection",
    )(proj, proj, proj, qg, kvg, wqn, wqr, wkv, cos, sup, sdn)


def _mla_attn_kernel(qn_ref, qr_ref, kn_ref, kr_ref, v_ref, o_ref, *, tq, tk):
    qi = pl.program_id(2)
    qn = qn_ref[...]
    qr = qr_ref[...]
    scale = (HEAD_DIM + MLA_ROPE) ** -0.5
    n_kb = (qi + 1) * (tq // tk)

    def step(kb, carry, masked):
        rows = pl.ds(pl.multiple_of(kb * tk, tk), tk)
        s = (_dot_nt(qn, kn_ref[rows, :]) + _dot_nt(qr, kr_ref[rows, :])) * scale
        if masked:
            s = jnp.where(_chunk_mask(qi * tq, kb * tk, tq, tk), s, NEG_INF)
        return _online_step(s, v_ref[rows, :], *carry)

    init = (jnp.full((tq, 1), NEG_INF, F32), jnp.zeros((tq, 1), F32), jnp.zeros((tq, HEAD_DIM), F32))
    n_diag = tq // tk
    carry = lax.fori_loop(0, n_kb - n_diag, functools.partial(step, masked=False), init)
    for d in range(n_diag):
        carry = step(n_kb - n_diag + d, carry, True)
    _, l, acc = carry
    o_ref[...] = (acc / l).astype(o_ref.dtype)


def mla_attention_mixer(qn, qr, kv, kr, batch, seq, *, tq=256, tk=128):
    t = qn.shape[0]
    tq = min(tq, seq)
    per_seq = seq // tq
    qspec = pl.BlockSpec((tq, HEAD_DIM), lambda b, h, s: (b * per_seq + s, h))
    return pl.pallas_call(
        functools.partial(_mla_attn_kernel, tq=tq, tk=tk),
        grid=(batch, N_HEADS, per_seq),
        in_specs=[qspec, qspec,
                  pl.BlockSpec((seq, HEAD_DIM), lambda b, h, s: (b, 2 * h)),
                  pl.BlockSpec((seq, LANES), lambda b, h, s: (b, 0)),
                  pl.BlockSpec((seq, HEAD_DIM), lambda b, h, s: (b, 2 * h + 1))],
        out_specs=qspec,
        out_shape=jax.ShapeDtypeStruct((t, N_HEADS * HEAD_DIM), BF16),
        compiler_params=_cparams("parallel", "parallel", "arbitrary"),
        name="mla_attention",
    )(qn, qr, kv, kr, kv)


def odd_layer_mixer(x2, mod, p, layer, j, batch, seq):
    sh_m, sc_m, gt_m, _, _, _ = _split_mod(mod)
    w_in = _odd_w_in_layout(p["odd_w_in"][j]).astype(BF16)
    proj = norm_matmul(x2, p["norm_mix"][layer], sc_m, sh_m, w_in, seq)
    qkv = gdn_conv(proj, p["gdn_conv"][j], batch, seq)
    gdn = gdn_mixer(qkv, proj, p["gdn_a_log"][j], p["gdn_dt_bias"][j], p["gdn_norm"][j], batch, seq)
    qn, qr, kv, kr = mla_projection(proj, p["mla_q_norm"][j], p["mla_w_uq"][j], p["mla_kv_norm"][j],
                                    p["mla_w_ukv"][j], seq)
    mla = mla_attention_mixer(qn, qr, kv, kr, batch, seq)
    return out_proj_residual(gdn, mla, p["odd_w_out"][j].astype(BF16), x2, gt_m, seq)


ROUTE_TOKENS = 128


def _top16(s):
    n, tt = s.shape
    row = lax.broadcasted_iota(jnp.int32, (n, tt), 0).astype(F32)
    r16 = lax.broadcasted_iota(jnp.int32, (PEER_TOPK, tt), 0)

    def body(k, carry):
        s, vals, ids = carry
        m = jnp.max(s, axis=0, keepdims=True)
        i = jnp.min(jnp.where(s == m, row, float(n)), axis=0, keepdims=True)
        s = jnp.where(row == i, NEG_INF, s)
        hit = r16 == k
        return s, jnp.where(hit, m, vals), jnp.where(hit, i, ids)

    zeros = jnp.zeros((PEER_TOPK, tt), F32)
    _, vals, ids = lax.fori_loop(0, PEER_TOPK, body, (s, zeros, zeros))
    return vals, ids


def _select_rows(table, sel):
    out = jnp.zeros_like(sel)
    for a in range(table.shape[0]):
        out = out + jnp.where(sel == float(a), table[a:a + 1, :], 0.0)
    return out


def _route_kernel(q_ref, keys_ref, idx_ref, gate_ref, cnt_ref, idx_all, gate_all):
    tt = q_ref.shape[0]

    def head_body(hd, _):
        tops = []
        for half in range(2):
            col = pl.multiple_of((2 * hd + half) * LANES, LANES)
            s = _dot_nt(keys_ref[half], q_ref[:, pl.ds(col, LANES)])
            tops.append(_top16(s))
        (v1, i1), (v2, i2) = tops
        cand = jnp.concatenate([v1[a:a + 1, :] + v2 for a in range(PEER_TOPK)], axis=0)
        best, pos = _top16(cand)
        a_sel = jnp.floor(pos * (1.0 / PEER_TOPK))
        b_sel = pos - a_sel * PEER_TOPK
        expert = _select_rows(i1, a_sel) * N_KEYS + _select_rows(i2, b_sel)
        e = jnp.exp(best - jnp.max(best, axis=0, keepdims=True))
        rows = pl.ds(pl.multiple_of(hd * PEER_TOPK, PEER_TOPK), PEER_TOPK)
        idx_all[rows, :] = expert.astype(jnp.int32)
        gate_all[rows, :] = e / jnp.sum(e, axis=0, keepdims=True)
        return 0

    lax.fori_loop(0, PEER_HEADS, head_body, 0)

    idx = idx_all[...]
    gate = gate_all[...]
    tid = idx >> EXPERT_TILE_SHIFT
    local = idx & (EXPERT_TILE - 1)
    n = PEER_SLOTS
    before = (lax.broadcasted_iota(jnp.int32, (n, n), 1) < lax.broadcasted_iota(jnp.int32, (n, n), 0)).astype(F32)
    for k in range(N_EXPERT_TILES):
        mem = tid == k
        shift = jnp.dot(before, jnp.where(mem, 0.0, 1.0), preferred_element_type=F32).astype(jnp.int32)
        xi = jnp.where(mem, local, 0)
        xg = jnp.where(mem, gate, 0.0)
        xd = jnp.where(mem, shift, 0)
        for b in range(int(math.log2(n))):
            step = 1 << b
            mv = ((xd >> b) & 1) == 1
            pull = lambda a: pltpu.roll(a, n - step, axis=0)
            inc = pull(mv.astype(jnp.int32)) == 1
            xi = jnp.where(inc, pull(xi), jnp.where(mv, 0, xi))
            xg = jnp.where(inc, pull(xg), jnp.where(mv, 0.0, xg))
            xd = jnp.where(inc, pull(xd), jnp.where(mv, 0, xd))
        idx_ref[k * n:(k + 1) * n, :] = xi
        gate_ref[k * n:(k + 1) * n, :] = xg
        cnt_ref[k:k + 1, :] = jnp.sum(mem.astype(jnp.int32), axis=0, keepdims=True)


def peer_route(q, sub_keys):
    t, d = q.shape
    tt = ROUTE_TOKENS
    n = N_EXPERT_TILES * PEER_SLOTS
    return pl.pallas_call(
        _route_kernel,
        grid=(t // tt,),
        in_specs=[pl.BlockSpec((tt, d), lambda i: (i, 0)),
                  pl.BlockSpec(sub_keys.shape, lambda i: (0, 0, 0))],
        out_specs=[pl.BlockSpec((n, tt), lambda i: (0, i)), pl.BlockSpec((n, tt), lambda i: (0, i)),
                   pl.BlockSpec((N_EXPERT_TILES, tt), lambda i: (0, i))],
        out_shape=[jax.ShapeDtypeStruct((n, t), jnp.int32), jax.ShapeDtypeStruct((n, t), F32),
                   jax.ShapeDtypeStruct((N_EXPERT_TILES, t), jnp.int32)],
        scratch_shapes=[pltpu.VMEM((PEER_SLOTS, tt), jnp.int32), pltpu.VMEM((PEER_SLOTS, tt), F32)],
        compiler_params=_cparams("parallel"),
        name="peer_route",
    )(q, sub_keys)


EXPERT_TOKENS = 128


def _n_groups(cnt):
    return (cnt + (SLOT_GROUP - 1)) // SLOT_GROUP


def _down_kernel(idx_ref, cnt_ref, h_ref, tbl_ref, act_ref):
    lane = lax.broadcasted_iota(jnp.int32, (1, PEER_SLOTS), 1)

    def token_body(t, _):
        hv = h_ref[t]

        def group_body(g, row):
            for u in range(SLOT_GROUP):
                j = g * SLOT_GROUP + u
                r = tbl_ref[idx_ref[j, t]]
                row = jnp.where(lane == j, jnp.sum(r * hv, axis=-1, keepdims=True), row)
            return row

        act_ref[t] = lax.fori_loop(0, _n_groups(cnt_ref[0, 0, t]), group_body, jnp.zeros((1, PEER_SLOTS), F32))
        return 0

    lax.fori_loop(0, h_ref.shape[0], token_body, 0)


def _smem_spec(block, index_map):
    return pl.BlockSpec(block, index_map, memory_space=pltpu.SMEM)


def _table_spec(d, index_map):
    return pl.BlockSpec((EXPERT_TILE, 1, d), index_map, pipeline_mode=pl.Buffered(1))


def peer_down_acts(idx, cnt, h3, table3):
    t, _, d = h3.shape
    tb = EXPERT_TOKENS
    return pl.pallas_call(
        _down_kernel,
        grid=(N_EXPERT_TILES, t // tb),
        in_specs=[_smem_spec((PEER_SLOTS, tb), lambda k, i: (k, i)),
                  _smem_spec((1, 1, tb), lambda k, i: (k, 0, i)),
                  pl.BlockSpec((tb, 1, d), lambda k, i: (i, 0, 0)),
                  _table_spec(d, lambda k, i: (k, 0, 0))],
        out_specs=pl.BlockSpec((tb, 1, PEER_SLOTS), lambda k, i: (k * (t // tb) + i, 0, 0)),
        out_shape=jax.ShapeDtypeStruct((N_EXPERT_TILES * t, 1, PEER_SLOTS), F32),
        compiler_params=_cparams("arbitrary", "arbitrary"),
        name="peer_down",
    )(idx, cnt, h3, table3)


def _coef_kernel(act_ref, gate_ref, o_ref):
    a = act_ref[...].T
    o_ref[...] = gate_ref[...] * (0.5 * a * (1.0 + lax.erf(a * (2.0 ** -0.5))))


def peer_coef(act2, gate):
    n, t = gate.shape
    tb = 512 if t % 512 == 0 else EXPERT_TOKENS
    per_tile = t // tb
    return pl.pallas_call(
        _coef_kernel,
        grid=(N_EXPERT_TILES, per_tile),
        in_specs=[pl.BlockSpec((tb, PEER_SLOTS), lambda k, i: (k * per_tile + i, 0)),
                  pl.BlockSpec((PEER_SLOTS, tb), lambda k, i: (k, i))],
        out_specs=pl.BlockSpec((PEER_SLOTS, tb), lambda k, i: (k, i)),
        out_shape=jax.ShapeDtypeStruct((n, t), F32),
        compiler_params=_cparams("parallel", "parallel"),
        name="peer_coef",
    )(act2, gate)


def _up_kernel(idx_ref, coef_ref, cnt_ref, x_ref, gt_ref, tbl_ref, o_ref):
    d = x_ref.shape[-1]

    def token_body(t, _):
        def group_body(g, acc):
            for u in range(SLOT_GROUP):
                j = g * SLOT_GROUP + u
                acc = acc + coef_ref[j, t] * tbl_ref[idx_ref[j, t]]
            return acc

        acc = lax.fori_loop(0, _n_groups(cnt_ref[0, 0, t]), group_body, jnp.zeros((1, d), F32))
        o_ref[t] = x_ref[t] + gt_ref[0] * acc
        return 0

    lax.fori_loop(0, x_ref.shape[0], token_body, 0)


def peer_up_tile(k, idx, coef, cnt, x3, gate3, table3, seq):
    t, _, d = x3.shape
    tb = min(EXPERT_TOKENS, seq)
    per_seq = seq // tb
    return pl.pallas_call(
        _up_kernel,
        grid=(t // tb,),
        in_specs=[_smem_spec((PEER_SLOTS, tb), lambda i: (k, i)),
                  _smem_spec((PEER_SLOTS, tb), lambda i: (k, i)),
                  _smem_spec((1, 1, tb), lambda i: (k, 0, i)),
                  pl.BlockSpec((tb, 1, d), lambda i: (i, 0, 0)),
                  pl.BlockSpec((1, 1, d), lambda i: (i // per_seq, 0, 0)),
                  _table_spec(d, lambda i: (k, 0, 0))],
        out_specs=pl.BlockSpec((tb, 1, d), lambda i: (i, 0, 0)),
        out_shape=jax.ShapeDtypeStruct(x3.shape, F32),
        compiler_params=_cparams("arbitrary"),
        name="peer_up",
    )(idx, coef, cnt, x3, gate3, table3)


def peer_layer(x2, mod, p, layer, batch, seq):
    _, _, _, sh_f, sc_f, gt_f = _split_mod(mod)
    t, d = x2.shape
    q, h = norm_matmul(x2, p["norm_ffn"][layer], sc_f, sh_f, p["peer_w_query"][layer].astype(BF16), seq, emit_h=True)
    idx, gate, cnt = peer_route(q, p["peer_sub_keys"][layer])
    cnt = cnt.reshape(N_EXPERT_TILES, 1, t)
    n_exp = p["peer_down"].shape[1]
    act = peer_down_acts(idx, cnt, h.reshape(t, 1, d), p["peer_down"][layer].reshape(n_exp, 1, d))
    coef = peer_coef(act.reshape(N_EXPERT_TILES * t, PEER_SLOTS), gate)
    x3 = x2.reshape(t, 1, d)
    up3 = p["peer_up"][layer].reshape(n_exp, 1, d)
    for k in range(N_EXPERT_TILES):
        x3 = peer_up_tile(k, idx, coef, cnt, x3, gt_f[:, None, :], up3, seq)
    return x3.reshape(t, d)


def _final_norm_kernel(x_ref, g_ref, o_ref):
    x = x_ref[...]
    o_ref[...] = x * lax.rsqrt(jnp.mean(x * x, axis=-1, keepdims=True) + EPS) * g_ref[...]


def final_rmsnorm(x2, gain, *, tm=512):
    t, d = x2.shape
    return pl.pallas_call(
        _final_norm_kernel,
        grid=(t // tm,),
        in_specs=[pl.BlockSpec((tm, d), lambda i: (i, 0)), pl.BlockSpec((1, d), lambda i: (0, 0))],
        out_specs=pl.BlockSpec((tm, d), lambda i: (i, 0)),
        out_shape=jax.ShapeDtypeStruct((t, d), F32),
        compiler_params=_cparams("parallel"),
        name="final_norm",
    )(x2, gain.reshape(1, d))


def kernel(x, c, norm_mix, norm_ffn, ada_w, ada_b, even_w_in, even_w_out, ret_norm, diff_lambda, diff_norm, odd_w_in, odd_w_out, gdn_conv, gdn_a_log, gdn_dt_bias, gdn_norm, mla_q_norm, mla_w_uq, mla_kv_norm, mla_w_ukv, peer_w_query, peer_sub_keys, peer_down, peer_up, final_norm):
    p = dict(norm_mix=norm_mix, norm_ffn=norm_ffn, even_w_in=even_w_in, even_w_out=even_w_out, ret_norm=ret_norm,
             diff_lambda=diff_lambda, diff_norm=diff_norm, odd_w_in=odd_w_in, odd_w_out=odd_w_out, gdn_conv=gdn_conv,
             gdn_a_log=gdn_a_log, gdn_dt_bias=gdn_dt_bias, gdn_norm=gdn_norm, mla_q_norm=mla_q_norm,
             mla_w_uq=mla_w_uq, mla_kv_norm=mla_kv_norm, mla_w_ukv=mla_w_ukv, peer_w_query=peer_w_query,
             peer_sub_keys=peer_sub_keys, peer_down=peer_down, peer_up=peer_up)
    batch, seq, d = x.shape
    depth = ada_w.shape[0]
    x2 = x.reshape(batch * seq, d)
    mod = modulation(c, ada_w, ada_b)
    for layer in range(depth):
        if layer % 2 == 0:
            x2 = even_layer_mixer(x2, mod[layer], p, layer, layer // 2, batch, seq)
        else:
            x2 = odd_layer_mixer(x2, mod[layer], p, layer, layer // 2, batch, seq)
        x2 = peer_layer(x2, mod[layer], p, layer, batch, seq)
    return final_rmsnorm(x2, final_norm).reshape(batch, seq, d)
```

```python
import functools
import math

import jax
import jax.numpy as jnp
import numpy as np
from jax import lax
from jax.experimental import pallas as pl
from jax.experimental.pallas import tpu as pltpu

F32 = jnp.float32
BF16 = jnp.bfloat16

CHUNK = 64
EPS = 1e-6
HEAD_DIM = 128
N_HEADS = 8
ROPE_THETA = 10000.0
MLA_Q_RANK = 384
MLA_KV_RANK = 256
MLA_ROPE = 64
N_KEYS = 128
PEER_HEADS = 8
PEER_TOPK = 16
PEER_SLOTS = PEER_HEADS * PEER_TOPK

LANES = 128
SUBLANES = 8
VMEM_LIMIT_BYTES = 56 * 1024 * 1024

N_EXPERT_TILES = 4
EXPERT_TILE = (N_KEYS * N_KEYS) // N_EXPERT_TILES
KEY_BITS = int(math.log2(N_KEYS))
TILE_BITS = int(math.log2(N_EXPERT_TILES))
SLOT_GROUP = 8

NEG_INF = float("-inf")


def _cparams(*sem):
    return pltpu.CompilerParams(dimension_semantics=sem, vmem_limit_bytes=VMEM_LIMIT_BYTES)


def _mod_kernel(c_ref, w_ref, b_ref, o_ref):
    c = c_ref[...]
    cond = c * jax.nn.sigmoid(c)
    o_ref[0] = jnp.dot(cond, w_ref[0], preferred_element_type=F32) + b_ref[0]


def modulation(c, ada_w, ada_b):
    depth, d, n = ada_w.shape
    b = c.shape[0]
    tn = 1536
    return pl.pallas_call(
        _mod_kernel,
        grid=(depth, n // tn),
        in_specs=[pl.BlockSpec((b, d), lambda l, j: (0, 0)),
                  pl.BlockSpec((1, d, tn), lambda l, j: (l, 0, j)),
                  pl.BlockSpec((1, 1, tn), lambda l, j: (l, 0, j))],
        out_specs=pl.BlockSpec((1, b, tn), lambda l, j: (l, 0, j)),
        out_shape=jax.ShapeDtypeStruct((depth, b, n), F32),
        compiler_params=_cparams("parallel", "parallel"),
        name="modulation",
    )(c, ada_w, ada_b.reshape(depth, 1, n))


def _norm_matmul_kernel(x_ref, g_ref, sc_ref, sh_ref, w_ref, *rest, emit_h):
    if emit_h:
        y_ref, h_ref, h_scr = rest
    else:
        y_ref, h_scr = rest

    @pl.when(pl.program_id(1) == 0)
    def _():
        x = x_ref[...]
        xn = x * lax.rsqrt(jnp.mean(x * x, axis=-1, keepdims=True) + EPS)
        h = xn * g_ref[...] * (1.0 + sc_ref[0]) + sh_ref[0]
        h_scr[...] = h.astype(BF16)
        if emit_h:
            h_ref[...] = h

    y_ref[...] = jnp.dot(h_scr[...], w_ref[...], preferred_element_type=F32)


def norm_matmul(x2, gain, scale, shift, w_bf16, seq, *, emit_h=False, tm=1024, tn=512):
    t, d = x2.shape
    n = w_bf16.shape[1]
    tm = min(tm, seq)
    assert seq % tm == 0 and n % tn == 0
    per_seq = seq // tm
    bvec = lambda i, j: (i // per_seq, 0, 0)
    out_shape = [jax.ShapeDtypeStruct((t, n), F32)]
    out_specs = [pl.BlockSpec((tm, tn), lambda i, j: (i, j))]
    if emit_h:
        out_shape.append(jax.ShapeDtypeStruct((t, d), F32))
        out_specs.append(pl.BlockSpec((tm, d), lambda i, j: (i, 0)))
    res = pl.pallas_call(
        functools.partial(_norm_matmul_kernel, emit_h=emit_h),
        grid=(t // tm, n // tn),
        in_specs=[pl.BlockSpec((tm, d), lambda i, j: (i, 0)),
                  pl.BlockSpec((1, d), lambda i, j: (0, 0)),
                  pl.BlockSpec((1, 1, d), bvec),
                  pl.BlockSpec((1, 1, d), bvec),
                  pl.BlockSpec((d, tn), lambda i, j: (0, j))],
        out_specs=out_specs,
        out_shape=out_shape,
        scratch_shapes=[pltpu.VMEM((tm, d), BF16)],
        compiler_params=_cparams("parallel", "arbitrary"),
        name="norm_matmul",
    )(x2, gain.reshape(1, d), scale[:, None, :], shift[:, None, :], w_bf16)
    return res if emit_h else res[0]


def _out_proj_kernel(a1_ref, a2_ref, w1_ref, w2_ref, x_ref, gt_ref, o_ref):
    y = jnp.dot(a1_ref[...], w1_ref[...], preferred_element_type=F32)
    y = y + jnp.dot(a2_ref[...], w2_ref[...], preferred_element_type=F32)
    o_ref[...] = x_ref[...] + gt_ref[0] * y


def out_proj_residual(a1, a2, w_bf16, x2, gate, seq, *, tm=1024, tn=512):
    t, k1 = a1.shape
    k2 = a2.shape[1]
    d = x2.shape[1]
    tm = min(tm, seq)
    per_seq = seq // tm
    return pl.pallas_call(
        _out_proj_kernel,
        grid=(t // tm, d // tn),
        in_specs=[pl.BlockSpec((tm, k1), lambda i, j: (i, 0)),
                  pl.BlockSpec((tm, k2), lambda i, j: (i, 0)),
                  pl.BlockSpec((k1, tn), lambda i, j: (0, j)),
                  pl.BlockSpec((k2, tn), lambda i, j: (0, j)),
                  pl.BlockSpec((tm, tn), lambda i, j: (i, j)),
                  pl.BlockSpec((1, 1, tn), lambda i, j: (i // per_seq, 0, j))],
        out_specs=pl.BlockSpec((tm, tn), lambda i, j: (i, j)),
        out_shape=jax.ShapeDtypeStruct((t, d), F32),
        compiler_params=_cparams("parallel", "parallel"),
        name="out_proj_residual",
    )(a1, a2, w_bf16[:k1], w_bf16[k1:], x2, gate[:, None, :])


def _dot_nt(a, b):
    return lax.dot_general(a, b, (((1,), (1,)), ((), ())), preferred_element_type=F32)


def _dot_tn(a, b):
    return lax.dot_general(a, b, (((0,), (0,)), ((), ())), preferred_element_type=F32)


def _retention_kernel(q_ref, k_ref, v_ref, g_ref, intra_ref, qdec_ref, kdec_ref, cdec_ref, norm_ref,
                      o_ref, state):
    @pl.when(pl.program_id(2) == 0)
    def _():
        state[...] = jnp.zeros_like(state)

    n_chunks = q_ref.shape[0] // CHUNK
    intra = intra_ref[0]
    qdec = qdec_ref[0]
    kdec = kdec_ref[0]
    cdec = cdec_ref[0]
    for c in range(n_chunks):
        rows = pl.ds(c * CHUNK, CHUNK)
        qc = q_ref[rows, :]
        kc = k_ref[rows, :] * (HEAD_DIM ** -0.5)
        vc = v_ref[rows, :]
        st = state[...]
        s = _dot_nt(qc, kc) * intra
        o = jnp.dot(s, vc, preferred_element_type=F32) + jnp.dot(qc * qdec, st, preferred_element_type=F32)
        state[...] = st * cdec + _dot_tn(kc * kdec, vc)
        on = o * lax.rsqrt(jnp.mean(o * o, axis=-1, keepdims=True) + EPS) * norm_ref[...]
        g = g_ref[rows, :]
        o_ref[rows, :] = (on * (g * jax.nn.sigmoid(g))).astype(o_ref.dtype)


def retention_mixer(proj, ret_norm, batch, seq, *, ts=512):
    t = proj.shape[0]
    ts = min(ts, seq)
    per_seq = seq // ts
    h8 = N_HEADS
    pos = jnp.arange(CHUNK, dtype=F32)
    lg = jnp.log1p(-jnp.exp2(-5.0 - jnp.arange(h8, dtype=F32)))[:, None]
    intra = jnp.exp(lg[..., None] * jnp.abs(pos[:, None] - pos[None, :]))
    qdec = jnp.broadcast_to(jnp.exp(lg * (pos + 1.0))[..., None], (h8, CHUNK, HEAD_DIM))
    kdec = jnp.broadcast_to(jnp.exp(lg * (CHUNK - 1.0 - pos))[..., None], (h8, CHUNK, HEAD_DIM))
    cdec = jnp.broadcast_to(jnp.exp(lg * CHUNK)[..., None], (h8, HEAD_DIM, HEAD_DIM))

    def col(off):
        return pl.BlockSpec((ts, HEAD_DIM), lambda b, h, s, off=off: (b * per_seq + s, off + h))

    hspec = lambda shape: pl.BlockSpec((1,) + shape, lambda b, h, s: (h, 0, 0))
    return pl.pallas_call(
        _retention_kernel,
        grid=(batch, h8, per_seq),
        in_specs=[col(0), col(h8), col(2 * h8), col(3 * h8),
                  hspec((CHUNK, CHUNK)), hspec((CHUNK, HEAD_DIM)), hspec((CHUNK, HEAD_DIM)),
                  hspec((HEAD_DIM, HEAD_DIM)),
                  pl.BlockSpec((1, HEAD_DIM), lambda b, h, s: (0, 0))],
        out_specs=pl.BlockSpec((ts, HEAD_DIM), lambda b, h, s: (b * per_seq + s, h)),
        out_shape=jax.ShapeDtypeStruct((t, h8 * HEAD_DIM), BF16),
        scratch_shapes=[pltpu.VMEM((HEAD_DIM, HEAD_DIM), F32)],
        compiler_params=_cparams("parallel", "parallel", "arbitrary"),
        name="retention",
    )(proj, proj, proj, proj, intra, qdec, kdec, cdec, ret_norm.reshape(1, HEAD_DIM))


def _chunk_mask(q0, k0, tq, tk):
    qc = (q0 + lax.broadcasted_iota(jnp.int32, (tq, tk), 0)) // CHUNK
    kc = (k0 + lax.broadcasted_iota(jnp.int32, (tq, tk), 1)) // CHUNK
    return kc <= qc


def _online_step(s, v, m, l, acc):
    m_new = jnp.maximum(m, jnp.max(s, axis=-1, keepdims=True))
    alpha = jnp.exp(m - m_new)
    p = jnp.exp(s - m_new)
    l = alpha * l + jnp.sum(p, axis=-1, keepdims=True)
    acc = alpha * acc + jnp.dot(p, v, preferred_element_type=F32)
    return m_new, l, acc


def _diff_attn_kernel(lam_ref, q_ref, k_ref, v_ref, norm_ref, o_ref, *, tq, tk, lambda_init):
    qi = pl.program_id(2)
    q = q_ref[...]
    half = lax.broadcasted_iota(jnp.int32, q.shape, 1) < (HEAD_DIM // 2)
    q0 = jnp.where(half, q, 0.0)
    q1 = jnp.where(half, 0.0, q)
    scale = (HEAD_DIM // 2) ** -0.5
    n_kb = (qi + 1) * (tq // tk)

    def step(kb, carry, masked):
        m0, l0, a0, m1, l1, a1 = carry
        rows = pl.ds(pl.multiple_of(kb * tk, tk), tk)
        k = k_ref[rows, :]
        v = v_ref[rows, :]
        s0 = _dot_nt(q0, k) * scale
        s1 = _dot_nt(q1, k) * scale
        if masked:
            mask = _chunk_mask(qi * tq, kb * tk, tq, tk)
            s0 = jnp.where(mask, s0, NEG_INF)
            s1 = jnp.where(mask, s1, NEG_INF)
        m0, l0, a0 = _online_step(s0, v, m0, l0, a0)
        m1, l1, a1 = _online_step(s1, v, m1, l1, a1)
        return m0, l0, a0, m1, l1, a1

    init = (jnp.full((tq, 1), NEG_INF, F32), jnp.zeros((tq, 1), F32), jnp.zeros((tq, HEAD_DIM), F32)) * 2
    n_diag = tq // tk
    carry = lax.fori_loop(0, n_kb - n_diag, functools.partial(step, masked=False), init)
    for d in range(n_diag):
        carry = step(n_kb - n_diag + d, carry, True)
    m0, l0, a0, m1, l1, a1 = carry
    dl = lam_ref[...]
    lam = (jnp.exp(jnp.sum(dl[0:1] * dl[1:2], axis=-1, keepdims=True))
           - jnp.exp(jnp.sum(dl[2:3] * dl[3:4], axis=-1, keepdims=True)) + lambda_init)
    o = a0 / l0 - lam * (a1 / l1)
    on = o * lax.rsqrt(jnp.mean(o * o, axis=-1, keepdims=True) + EPS) * norm_ref[...]
    o_ref[...] = (on * (1.0 - lambda_init)).astype(o_ref.dtype)


def diff_attention_mixer(proj, diff_lambda, diff_norm, lambda_init, batch, seq, *, tq=256, tk=128):
    t = proj.shape[0]
    tq = min(tq, seq)
    per_seq = seq // tq
    kv = lambda off: pl.BlockSpec((seq, HEAD_DIM), lambda b, h, s, off=off: (b, off + h))
    return pl.pallas_call(
        functools.partial(_diff_attn_kernel, tq=tq, tk=tk, lambda_init=lambda_init),
        grid=(batch, N_HEADS, per_seq),
        in_specs=[pl.BlockSpec(diff_lambda.shape, lambda b, h, s: (0, 0)),
                  pl.BlockSpec((tq, HEAD_DIM), lambda b, h, s: (b * per_seq + s, 4 * N_HEADS + h)),
                  kv(5 * N_HEADS), kv(6 * N_HEADS),
                  pl.BlockSpec((1, HEAD_DIM), lambda b, h, s: (0, 0))],
        out_specs=pl.BlockSpec((tq, HEAD_DIM), lambda b, h, s: (b * per_seq + s, h)),
        out_shape=jax.ShapeDtypeStruct((t, N_HEADS * HEAD_DIM), BF16),
        compiler_params=_cparams("parallel", "parallel", "arbitrary"),
        name="diff_attention",
    )(diff_lambda, proj, proj, proj, diff_norm.reshape(1, HEAD_DIM))


def _split_mod(mod):
    return jnp.split(mod, 6, axis=-1)


def even_layer_mixer(x2, mod, p, layer, j, batch, seq):
    sh_m, sc_m, gt_m, _, _, _ = _split_mod(mod)
    proj = norm_matmul(x2, p["norm_mix"][layer], sc_m, sh_m, p["even_w_in"][j].astype(BF16), seq)
    ret = retention_mixer(proj, p["ret_norm"][j], batch, seq)
    lambda_init = 0.8 - 0.6 * math.exp(-0.3 * layer)
    dif = diff_attention_mixer(proj, p["diff_lambda"][j], p["diff_norm"][j], lambda_init, batch, seq)
    return out_proj_residual(ret, dif, p["even_w_out"][j].astype(BF16), x2, gt_m, seq)


ODD_COLS = 40 * LANES
GATE_BLOCK = 4 * N_HEADS


def _odd_w_in_layout(w):
    d = w.shape[0]
    hd = N_HEADS * HEAD_DIM
    zeros = lambda n: jnp.zeros((d, n), w.dtype)
    o = 4 * hd
    gates = w[:, o:o + 2 * N_HEADS]
    o += 2 * N_HEADS
    cq = w[:, o:o + MLA_Q_RANK]
    o += MLA_Q_RANK
    ckv = w[:, o:o + MLA_KV_RANK]
    o += MLA_KV_RANK
    kr = w[:, o:o + MLA_ROPE]
    return jnp.concatenate([w[:, :4 * hd], gates, zeros(LANES - 2 * N_HEADS), cq, ckv, kr,
                            zeros(LANES - MLA_ROPE), zeros(LANES)], axis=1)


def _silu(x):
    return x * jax.nn.sigmoid(x)


def _gdn_conv_kernel(x_ref, w_ref, o_ref):
    c = pl.program_id(1)
    x = x_ref[...]
    w = w_ref[...]
    width = w.shape[0]
    row = lax.broadcasted_iota(jnp.int32, x.shape, 0)
    y = x * w[width - 1:width]
    for sft in range(1, width):
        xs = jnp.where(row >= sft, pltpu.roll(x, sft, axis=0), 0.0)
        y = y + xs * w[width - 1 - sft:width - sft]
    y = _silu(y)
    yn = y * lax.rsqrt(jnp.sum(y * y, axis=-1, keepdims=True) + EPS)
    yn = yn * jnp.where(c < N_HEADS, HEAD_DIM ** -0.5, 1.0)
    o_ref[...] = jnp.where(c < 2 * N_HEADS, yn, y)


def gdn_conv(proj, conv_w, batch, seq):
    t = proj.shape[0]
    n_blk = 3 * N_HEADS
    return pl.pallas_call(
        _gdn_conv_kernel,
        grid=(batch, n_blk),
        in_specs=[pl.BlockSpec((seq, LANES), lambda b, c: (b, c)),
                  pl.BlockSpec((conv_w.shape[0], LANES), lambda b, c: (0, c))],
        out_specs=pl.BlockSpec((seq, LANES), lambda b, c: (b, c)),
        out_shape=jax.ShapeDtypeStruct((t, n_blk * LANES), F32),
        compiler_params=_cparams("parallel", "parallel"),
        name="gdn_conv",
    )(proj, conv_w)


def _dot_hi(a, b):
    return jnp.dot(a, b, preferred_element_type=F32, precision=lax.Precision.HIGHEST)


def _unit_lower_inverse(l_mat):
    n = l_mat.shape[0]
    eye = (lax.broadcasted_iota(jnp.int32, (n, n), 0) == lax.broadcasted_iota(jnp.int32, (n, n), 1)).astype(F32)
    inv = eye - l_mat
    power = _dot_hi(l_mat, l_mat)
    span = 2
    while span < n:
        inv = inv + _dot_hi(inv, power)
        span *= 2
        if span < n:
            power = _dot_hi(power, power)
    return inv


def _pick_lane(x, lane):
    sel = lax.broadcasted_iota(jnp.int32, x.shape, 1) == lane
    return jnp.sum(jnp.where(sel, x, 0.0), axis=-1, keepdims=True)


def _gdn_kernel(q_ref, k_ref, v_ref, gg_ref, gate_ref, alog_ref, dtb_ref, norm_ref, o_ref, state):
    h = pl.program_id(1)

    @pl.when(pl.program_id(2) == 0)
    def _():
        state[...] = jnp.zeros_like(state)

    n_chunks = q_ref.shape[0] // CHUNK
    ri = lax.broadcasted_iota(jnp.int32, (CHUNK, CHUNK), 0)
    ci = lax.broadcasted_iota(jnp.int32, (CHUNK, CHUNK), 1)
    tri = ci <= ri
    strict = ci < ri
    tri_f = tri.astype(F32)
    a_coef = -jnp.exp(_pick_lane(alog_ref[...], h))
    dt_bias = _pick_lane(dtb_ref[...], h)
    for c in range(n_chunks):
        rows = pl.ds(c * CHUNK, CHUNK)
        q = q_ref[rows, :]
        k = k_ref[rows, :]
        v = v_ref[rows, :]
        raw = gate_ref[rows, :]
        beta = jax.nn.sigmoid(_pick_lane(raw, h))
        ga = _pick_lane(raw, N_HEADS + h) + dt_bias
        softplus = jnp.maximum(ga, 0.0) + jnp.log1p(jnp.exp(-jnp.abs(ga)))
        g = jnp.broadcast_to(a_coef * softplus, (CHUNK, HEAD_DIM))
        gc = _dot_hi(tri_f, g)
        g_rows = gc.T[:CHUNK, :]
        gdiff = gc[:, :CHUNK] - g_rows
        decay = jnp.where(tri, jnp.exp(jnp.where(tri, gdiff, 0.0)), 0.0)
        k_beta = k * beta
        l_mat = jnp.where(strict, _dot_nt(k_beta, k) * decay, 0.0)
        t_inv = _unit_lower_inverse(l_mat)
        eg = jnp.exp(gc)
        u = jnp.dot(t_inv, v * beta, preferred_element_type=F32)
        w = jnp.dot(t_inv, k_beta * eg, preferred_element_type=F32)
        intra = _dot_nt(q, k) * decay
        st = state[...]
        v_new = u - jnp.dot(w, st, preferred_element_type=F32)
        o = jnp.dot(q * eg, st, preferred_element_type=F32) + jnp.dot(intra, v_new, preferred_element_type=F32)
        g_last = gc[CHUNK - 1:CHUNK, :]
        state[...] = st * jnp.exp(g_last) + _dot_tn(k * jnp.exp(g_last - gc), v_new)
        on = o * lax.rsqrt(jnp.mean(o * o, axis=-1, keepdims=True) + EPS) * norm_ref[...]
        o_ref[rows, :] = (on * _silu(gg_ref[rows, :])).astype(o_ref.dtype)


def gdn_mixer(qkv, proj, a_log, dt_bias, gdn_norm, batch, seq, *, ts=512):
    t = qkv.shape[0]
    ts = min(ts, seq)
    per_seq = seq // ts
    pad = lambda v: jnp.pad(v.reshape(1, -1), ((0, 0), (0, LANES - v.shape[0])))

    def col(off):
        return pl.BlockSpec((ts, HEAD_DIM), lambda b, h, s, off=off: (b * per_seq + s, off + h))

    row1 = pl.BlockSpec((1, LANES), lambda b, h, s: (0, 0))
    return pl.pallas_call(
        _gdn_kernel,
        grid=(batch, N_HEADS, per_seq),
        in_specs=[col(0), col(N_HEADS), col(2 * N_HEADS), col(3 * N_HEADS),
                  pl.BlockSpec((ts, LANES), lambda b, h, s: (b * per_seq + s, GATE_BLOCK)),
                  row1, row1, row1],
        out_specs=pl.BlockSpec((ts, HEAD_DIM), lambda b, h, s: (b * per_seq + s, h)),
        out_shape=jax.ShapeDtypeStruct((t, N_HEADS * HEAD_DIM), BF16),
        scratch_shapes=[pltpu.VMEM((HEAD_DIM, HEAD_DIM), F32)],
        compiler_params=_cparams("parallel", "parallel", "arbitrary"),
        name="gdn",
    )(qkv, qkv, qkv, proj, proj, pad(a_log), pad(dt_bias), gdn_norm.reshape(1, HEAD_DIM))


def _rope_tables(seq):
    half = MLA_ROPE // 2
    inv_freq = ROPE_THETA ** (-jnp.arange(half, dtype=F32) / half)
    ang = jnp.arange(seq, dtype=jnp.int32).astype(F32)[:, None] * inv_freq[None, :]
    cos, sin = jnp.cos(ang), jnp.sin(ang)
    z = jnp.zeros((seq, LANES - MLA_ROPE), F32)
    zh = jnp.zeros((seq, half), F32)
    return (jnp.concatenate([cos, cos, z], axis=1),
            jnp.concatenate([zh, sin, z], axis=1),
            jnp.concatenate([-sin, zh, z], axis=1))


def _rope(y, cos, sin_up, sin_dn):
    half = MLA_ROPE // 2
    return y * cos + pltpu.roll(y, half, axis=1) * sin_up + pltpu.roll(y, LANES - half, axis=1) * sin_dn


def _mla_proj_kernel(cq_ref, ckv_ref, kr_ref, qg_ref, kvg_ref, wqn_ref, wqr_ref, wkv_ref, cos_ref, sup_ref, sdn_ref,
                     qn_ref, qr_ref, kv_ref, kro_ref):
    def rms(x, g):
        return x * lax.rsqrt(jnp.mean(x * x, axis=-1, keepdims=True) + EPS) * g

    cos, sup, sdn = cos_ref[...], sup_ref[...], sdn_ref[...]
    cq = rms(cq_ref[...], qg_ref[...]).astype(BF16)
    qn_ref[...] = jnp.dot(cq, wqn_ref[...], preferred_element_type=F32).astype(qn_ref.dtype)
    qr = jnp.dot(cq, wqr_ref[...], preferred_element_type=F32)
    for h in range(N_HEADS):
        cols = slice(h * LANES, (h + 1) * LANES)
        qr_ref[:, cols] = _rope(qr[:, cols], cos, sup, sdn).astype(qr_ref.dtype)
    ckv = rms(ckv_ref[...], kvg_ref[...]).astype(BF16)
    kv_ref[...] = jnp.dot(ckv, wkv_ref[...], preferred_element_type=F32).astype(kv_ref.dtype)
    kro_ref[...] = _rope(kr_ref[...], cos, sup, sdn).astype(kro_ref.dtype)


def mla_projection(proj, q_norm, w_uq, kv_norm, w_ukv, seq, *, tm=512):
    t = proj.shape[0]
    tm = min(tm, seq)
    per_seq = seq // tm
    hd = N_HEADS * HEAD_DIM
    w3 = w_uq.reshape(MLA_Q_RANK, N_HEADS, HEAD_DIM + MLA_ROPE)
    wqn = w3[:, :, :HEAD_DIM].reshape(MLA_Q_RANK, hd).astype(BF16)
    wqr = jnp.pad(w3[:, :, HEAD_DIM:], ((0, 0), (0, 0), (0, LANES - MLA_ROPE))).reshape(MLA_Q_RANK, hd).astype(BF16)
    cos, sup, sdn = _rope_tables(seq)
    full = lambda a: pl.BlockSpec(a.shape, lambda i: (0,) * a.ndim)
    tab = pl.BlockSpec((tm, LANES), lambda i: (i % per_seq, 0))
    wkv = w_ukv.astype(BF16)
    qg, kvg = q_norm.reshape(1, -1), kv_norm.reshape(1, -1)
    return pl.pallas_call(
        _mla_proj_kernel,
        grid=(t // tm,),
        in_specs=[pl.BlockSpec((tm, MLA_Q_RANK), lambda i: (i, (GATE_BLOCK + 1) * LANES // MLA_Q_RANK)),
                  pl.BlockSpec((tm, MLA_KV_RANK), lambda i: (i, (GATE_BLOCK + 4) * LANES // MLA_KV_RANK)),
                  pl.BlockSpec((tm, LANES), lambda i: (i, GATE_BLOCK + 6)),
                  full(qg), full(kvg), full(wqn), full(wqr), full(wkv), tab, tab, tab],
        out_specs=[pl.BlockSpec((tm, hd), lambda i: (i, 0)), pl.BlockSpec((tm, hd), lambda i: (i, 0)),
                   pl.BlockSpec((tm, 2 * hd), lambda i: (i, 0)), pl.BlockSpec((tm, LANES), lambda i: (i, 0))],
        out_shape=[jax.ShapeDtypeStruct((t, hd), BF16), jax.ShapeDtypeStruct((t, hd), BF16),
                   jax.ShapeDtypeStruct((t, 2 * hd), BF16), jax.ShapeDtypeStruct((t, LANES), BF16)],
        compiler_params=_cparams("parallel"),
        name="mla_projection",
    )(proj, proj, proj, qg, kvg, wqn, wqr, wkv, cos, sup, sdn)


def _mla_attn_kernel(qn_ref, qr_ref, kn_ref, kr_ref, v_ref, o_ref, *, tq, tk):
    qi = pl.program_id(2)
    qn = qn_ref[...]
    qr = qr_ref[...]
    scale = (HEAD_DIM + MLA_ROPE) ** -0.5
    n_kb = (qi + 1) * (tq // tk)

    def step(kb, carry, masked):
        rows = pl.ds(pl.multiple_of(kb * tk, tk), tk)
        s = (_dot_nt(qn, kn_ref[rows, :]) + _dot_nt(qr, kr_ref[rows, :])) * scale
        if masked:
            s = jnp.where(_chunk_mask(qi * tq, kb * tk, tq, tk), s, NEG_INF)
        return _online_step(s, v_ref[rows, :], *carry)

    init = (jnp.full((tq, 1), NEG_INF, F32), jnp.zeros((tq, 1), F32), jnp.zeros((tq, HEAD_DIM), F32))
    n_diag = tq // tk
    carry = lax.fori_loop(0, n_kb - n_diag, functools.partial(step, masked=False), init)
    for d in range(n_diag):
        carry = step(n_kb - n_diag + d, carry, True)
    _, l, acc = carry
    o_ref[...] = (acc / l).astype(o_ref.dtype)


def mla_attention_mixer(qn, qr, kv, kr, batch, seq, *, tq=256, tk=128):
    t = qn.shape[0]
    tq = min(tq, seq)
    per_seq = seq // tq
    qspec = pl.BlockSpec((tq, HEAD_DIM), lambda b, h, s: (b * per_seq + s, h))
    return pl.pallas_call(
        functools.partial(_mla_attn_kernel, tq=tq, tk=tk),
        grid=(batch, N_HEADS, per_seq),
        in_specs=[qspec, qspec,
                  pl.BlockSpec((seq, HEAD_DIM), lambda b, h, s: (b, 2 * h)),
                  pl.BlockSpec((seq, LANES), lambda b, h, s: (b, 0)),
                  pl.BlockSpec((seq, HEAD_DIM), lambda b, h, s: (b, 2 * h + 1))],
        out_specs=qspec,
        out_shape=jax.ShapeDtypeStruct((t, N_HEADS * HEAD_DIM), BF16),
        compiler_params=_cparams("parallel", "parallel", "arbitrary"),
        name="mla_attention",
    )(qn, qr, kv, kr, kv)


def odd_layer_mixer(x2, mod, p, layer, j, batch, seq):
    sh_m, sc_m, gt_m, _, _, _ = _split_mod(mod)
    w_in = _odd_w_in_layout(p["odd_w_in"][j]).astype(BF16)
    proj = norm_matmul(x2, p["norm_mix"][layer], sc_m, sh_m, w_in, seq)
    qkv = gdn_conv(proj, p["gdn_conv"][j], batch, seq)
    gdn = gdn_mixer(qkv, proj, p["gdn_a_log"][j], p["gdn_dt_bias"][j], p["gdn_norm"][j], batch, seq)
    qn, qr, kv, kr = mla_projection(proj, p["mla_q_norm"][j], p["mla_w_uq"][j], p["mla_kv_norm"][j],
                                    p["mla_w_ukv"][j], seq)
    mla = mla_attention_mixer(qn, qr, kv, kr, batch, seq)
    return out_proj_residual(gdn, mla, p["odd_w_out"][j].astype(BF16), x2, gt_m, seq)


ROUTE_TOKENS = 128


def _top16(*problems):
    n, tt = problems[0].shape
    row = lax.broadcasted_iota(jnp.int32, (n, tt), 0).astype(F32)
    r16 = lax.broadcasted_iota(jnp.int32, (PEER_TOPK, tt), 0)

    def body(k, carry):
        hit = r16 == k
        out = []
        for s, vals, ids in carry:
            m = jnp.max(s, axis=0, keepdims=True)
            i = jnp.min(jnp.where(s == m, row, float(n)), axis=0, keepdims=True)
            out.append((jnp.where(row == i, NEG_INF, s), jnp.where(hit, m, vals), jnp.where(hit, i, ids)))
        return tuple(out)

    zeros = jnp.zeros((PEER_TOPK, tt), F32)
    res = lax.fori_loop(0, PEER_TOPK, body, tuple((s, zeros, zeros) for s in problems))
    res = [(vals, ids) for _, vals, ids in res]
    return res[0] if len(problems) == 1 else res


def _select_rows(table, sel):
    out = jnp.zeros_like(sel)
    for a in range(table.shape[0]):
        out = out + jnp.where(sel == float(a), table[a:a + 1, :], 0.0)
    return out


def _route_kernel(q_ref, keys_ref, idx_ref, gate_ref, cnt_ref, idx_all, gate_all, *, rows_per_expert):
    def head_body(hd, _):
        scores = []
        for half in range(2):
            col = pl.multiple_of((2 * hd + half) * LANES, LANES)
            scores.append(_dot_nt(keys_ref[half], q_ref[:, pl.ds(col, LANES)]))
        (v1, i1), (v2, i2) = _top16(*scores)
        cand = jnp.concatenate([v1[a:a + 1, :] + v2 for a in range(PEER_TOPK)], axis=0)
        best, pos = _top16(cand)
        a_sel = jnp.floor(pos * (1.0 / PEER_TOPK))
        b_sel = pos - a_sel * PEER_TOPK
        expert = _select_rows(i1, a_sel) * N_KEYS + _select_rows(i2, b_sel)
        e = jnp.exp(best - jnp.max(best, axis=0, keepdims=True))
        rows = pl.ds(pl.multiple_of(hd * PEER_TOPK, PEER_TOPK), PEER_TOPK)
        idx_all[rows, :] = expert.astype(jnp.int32)
        gate_all[rows, :] = e / jnp.sum(e, axis=0, keepdims=True)
        return 0

    lax.fori_loop(0, PEER_HEADS, head_body, 0)

    idx = idx_all[...]
    gate = gate_all[...]
    key1 = idx >> KEY_BITS
    key2 = idx & (N_KEYS - 1)
    tid = (key1 + key2) & (N_EXPERT_TILES - 1)
    local = ((key1 << (KEY_BITS - TILE_BITS)) + (key2 >> TILE_BITS)) * rows_per_expert
    n = PEER_SLOTS
    before = (lax.broadcasted_iota(jnp.int32, (n, n), 1) < lax.broadcasted_iota(jnp.int32, (n, n), 0)).astype(F32)
    for k in range(N_EXPERT_TILES):
        mem = tid == k
        shift = jnp.dot(before, jnp.where(mem, 0.0, 1.0), preferred_element_type=F32).astype(jnp.int32)
        xi = jnp.where(mem, local, 0)
        xg = jnp.where(mem, gate, 0.0)
        xd = jnp.where(mem, shift, 0)
        for b in range(int(math.log2(n))):
            step = 1 << b
            mv = ((xd >> b) & 1) == 1
            pull = lambda a: pltpu.roll(a, n - step, axis=0)
            inc = pull(mv.astype(jnp.int32)) == 1
            xi = jnp.where(inc, pull(xi), jnp.where(mv, 0, xi))
            xg = jnp.where(inc, pull(xg), jnp.where(mv, 0.0, xg))
            xd = jnp.where(inc, pull(xd), jnp.where(mv, 0, xd))
        idx_ref[k] = xi.T
        gate_ref[k * n:(k + 1) * n, :] = xg
        cnt_ref[k:k + 1, :] = jnp.sum(mem.astype(jnp.int32), axis=0, keepdims=True)


def peer_route(q, sub_keys, rows_per_expert):
    t, d = q.shape
    tt = ROUTE_TOKENS
    n = N_EXPERT_TILES * PEER_SLOTS
    return pl.pallas_call(
        functools.partial(_route_kernel, rows_per_expert=rows_per_expert),
        grid=(t // tt,),
        in_specs=[pl.BlockSpec((tt, d), lambda i: (i, 0)),
                  pl.BlockSpec(sub_keys.shape, lambda i: (0, 0, 0))],
        out_specs=[pl.BlockSpec((N_EXPERT_TILES, tt, PEER_SLOTS), lambda i: (0, i, 0)),
                   pl.BlockSpec((n, tt), lambda i: (0, i)),
                   pl.BlockSpec((N_EXPERT_TILES, tt), lambda i: (0, i))],
        out_shape=[jax.ShapeDtypeStruct((N_EXPERT_TILES, t, PEER_SLOTS), jnp.int32),
                   jax.ShapeDtypeStruct((n, t), F32),
                   jax.ShapeDtypeStruct((N_EXPERT_TILES, t), jnp.int32)],
        scratch_shapes=[pltpu.VMEM((PEER_SLOTS, tt), jnp.int32), pltpu.VMEM((PEER_SLOTS, tt), F32)],
        compiler_params=_cparams("parallel"),
        name="peer_route",
    )(q, sub_keys)


EXPERT_TOKENS = 128


def _n_groups(cnt):
    return (cnt + (SLOT_GROUP - 1)) // SLOT_GROUP


STATIC_SLOTS = 40
STATIC_GROUPS = STATIC_SLOTS // SLOT_GROUP
FOLD_ORDER = (0, 4, 2, 6, 1, 5, 3, 7)


def _rows_of(i, n):
    return pl.ds(pl.multiple_of(i * n, n), n)


def _fold_group(parts):
    sub = lax.broadcasted_iota(jnp.int32, parts[0].shape, 0)
    xs = [parts[i] for i in FOLD_ORDER]
    half = SUBLANES // 2
    while half >= 1:
        keep = (sub & half) == 0
        xs = [jnp.where(keep, x, pltpu.roll(y, half, axis=0)) + jnp.where(keep, pltpu.roll(x, SUBLANES - half, axis=0), y)
              for x, y in zip(xs[0::2], xs[1::2])]
        half //= 2
    return xs[0]


def _down_kernel(idx_ref, cnt_ref, h_ref, tbl_ref, act_ref, part_scr):
    tb = act_ref.shape[1]
    rpe = h_ref.shape[0] // tb
    lane = lax.broadcasted_iota(jnp.int32, (SUBLANES, tb), 1)

    def partial(off, hv):
        p = tbl_ref[pl.ds(pl.multiple_of(off, rpe), rpe), :] * hv
        out = p[:SUBLANES]
        for c in range(1, rpe // SUBLANES):
            out = out + p[c * SUBLANES:(c + 1) * SUBLANES]
        return out

    def group(t, g, hv):
        return _fold_group([partial(idx_ref[t, g * SLOT_GROUP + u], hv) for u in range(SLOT_GROUP)])

    act_ref[STATIC_SLOTS:, :] = jnp.zeros((PEER_SLOTS - STATIC_SLOTS, tb), F32)

    def tok_body(t, _):
        hv = h_ref[_rows_of(t, rpe), :]
        for g in range(STATIC_GROUPS):
            part_scr[_rows_of(t * STATIC_GROUPS + g, SUBLANES), :] = group(t, g, hv)
        n = cnt_ref[0, 0, t]

        @pl.when(n > STATIC_SLOTS)
        def _():
            def group_body(g, _):
                rows = _rows_of(g, SLOT_GROUP)
                col = jnp.sum(group(t, g, hv), axis=-1, keepdims=True)
                act_ref[rows, :] = jnp.where(lane == t, col, act_ref[rows, :])
                return 0

            lax.fori_loop(STATIC_GROUPS, _n_groups(n), group_body, 0)

        return 0

    lax.fori_loop(0, tb, tok_body, 0)

    for g in range(STATIC_GROUPS):
        acc = jnp.zeros((SUBLANES, tb), F32)
        for t in range(tb):
            c = part_scr[(t * STATIC_GROUPS + g) * SUBLANES:(t * STATIC_GROUPS + g + 1) * SUBLANES, :]
            acc = jnp.where(lane == t, jnp.sum(c, axis=-1, keepdims=True), acc)
        act_ref[g * SLOT_GROUP:(g + 1) * SLOT_GROUP, :] = acc


def _smem_spec(block, index_map):
    return pl.BlockSpec(block, index_map, memory_space=pltpu.SMEM)


def _tile_spec(rpe, index_map):
    return pl.BlockSpec((EXPERT_TILE * rpe, LANES), index_map, pipeline_mode=pl.Buffered(1))


def peer_down_acts(idx, cnt, h2, table2):
    rpe = table2.shape[0] // (N_KEYS * N_KEYS)
    t = h2.shape[0] // rpe
    tb = EXPERT_TOKENS
    return pl.pallas_call(
        _down_kernel,
        grid=(N_EXPERT_TILES, t // tb),
        in_specs=[_smem_spec((tb, PEER_SLOTS), lambda k, i: (k * (t // tb) + i, 0)),
                  _smem_spec((1, 1, tb), lambda k, i: (k, 0, i)),
                  pl.BlockSpec((tb * rpe, LANES), lambda k, i: (i, 0)),
                  _tile_spec(rpe, lambda k, i: (k, 0))],
        out_specs=pl.BlockSpec((PEER_SLOTS, tb), lambda k, i: (k, i)),
        out_shape=jax.ShapeDtypeStruct((N_EXPERT_TILES * PEER_SLOTS, t), F32),
        scratch_shapes=[pltpu.VMEM((tb * STATIC_GROUPS * SUBLANES, LANES), F32)],
        compiler_params=_cparams("arbitrary", "arbitrary"),
        name="peer_down",
    )(idx, cnt, h2, table2)


def _coef_kernel(act_ref, gate_ref, o_ref):
    a = act_ref[...]
    coef = gate_ref[...] * (0.5 * a * (1.0 + lax.erf(a * (2.0 ** -0.5))))
    o_ref[...] = coef.T


def peer_coef(act, gate):
    n, t = gate.shape
    tb = 512 if t % 512 == 0 else EXPERT_TOKENS
    per_tile = t // tb
    spec = pl.BlockSpec((PEER_SLOTS, tb), lambda k, i: (k, i))
    return pl.pallas_call(
        _coef_kernel,
        grid=(N_EXPERT_TILES, per_tile),
        in_specs=[spec, spec],
        out_specs=pl.BlockSpec((tb, PEER_SLOTS), lambda k, i: (k * per_tile + i, 0)),
        out_shape=jax.ShapeDtypeStruct((N_EXPERT_TILES * t, PEER_SLOTS), F32),
        compiler_params=_cparams("parallel", "parallel"),
        name="peer_coef",
    )(act, gate)


def _up_kernel(idx_ref, coef_ref, cnt_ref, x_ref, gt_ref, tbl_ref, o_ref):
    tb = cnt_ref.shape[-1]
    rpe = x_ref.shape[0] // tb
    gt = gt_ref[...]

    def term(t, j):
        return coef_ref[t, j] * tbl_ref[pl.ds(pl.multiple_of(idx_ref[t, j], rpe), rpe), :]

    def token_body(t, _):
        accs = [jnp.zeros((rpe, LANES), F32), jnp.zeros((rpe, LANES), F32)]
        for j in range(STATIC_SLOTS):
            accs[j % 2] = accs[j % 2] + term(t, j)

        def group_body(g, acc):
            for u in range(SLOT_GROUP):
                acc = acc + term(t, g * SLOT_GROUP + u)
            return acc

        n_groups = jnp.maximum(_n_groups(cnt_ref[0, 0, t]), STATIC_GROUPS)
        acc = lax.fori_loop(STATIC_GROUPS, n_groups, group_body, accs[0] + accs[1])
        rows = _rows_of(t, rpe)
        o_ref[rows, :] = x_ref[rows, :] + gt * acc
        return 0

    lax.fori_loop(0, tb, token_body, 0)


def peer_up_tile(k, idx, coef, cnt, x2r, gate2r, table2, seq):
    rpe = table2.shape[0] // (N_KEYS * N_KEYS)
    t = x2r.shape[0] // rpe
    tb = min(EXPERT_TOKENS, seq)
    per_seq = seq // tb
    n_blk = t // tb
    return pl.pallas_call(
        _up_kernel,
        grid=(n_blk,),
        in_specs=[_smem_spec((tb, PEER_SLOTS), lambda i: (k * n_blk + i, 0)),
                  _smem_spec((tb, PEER_SLOTS), lambda i: (k * n_blk + i, 0)),
                  _smem_spec((1, 1, tb), lambda i: (k, 0, i)),
                  pl.BlockSpec((tb * rpe, LANES), lambda i: (i, 0)),
                  pl.BlockSpec((rpe, LANES), lambda i: (i // per_seq, 0)),
                  _tile_spec(rpe, lambda i: (k, 0))],
        out_specs=pl.BlockSpec((tb * rpe, LANES), lambda i: (i, 0)),
        out_shape=jax.ShapeDtypeStruct(x2r.shape, F32),
        compiler_params=_cparams("arbitrary"),
        name="peer_up",
    )(idx, coef, cnt, x2r, gate2r, table2)


def _tile_table(table):
    e, d = table.shape
    nt = N_EXPERT_TILES
    per = N_KEYS // nt
    t5 = table.reshape(per, nt, per, nt, d)
    t5 = jnp.stack([jnp.roll(t5[:, s], s, axis=2) for s in range(nt)], axis=1)
    return jnp.moveaxis(t5, 3, 0).reshape(e * (d // LANES), LANES)


def peer_layer(x2, mod, p, layer, batch, seq):
    _, _, _, sh_f, sc_f, gt_f = _split_mod(mod)
    t, d = x2.shape
    rpe = d // LANES
    q, h = norm_matmul(x2, p["norm_ffn"][layer], sc_f, sh_f, p["peer_w_query"][layer].astype(BF16), seq, emit_h=True)
    idx, gate, cnt = peer_route(q, p["peer_sub_keys"][layer], rpe)
    idx = idx.reshape(N_EXPERT_TILES * t, PEER_SLOTS)
    cnt = cnt.reshape(N_EXPERT_TILES, 1, t)
    act = peer_down_acts(idx, cnt, h.reshape(t * rpe, LANES), _tile_table(p["peer_down"][layer]))
    coef = peer_coef(act, gate)
    up2 = _tile_table(p["peer_up"][layer])
    xr = x2.reshape(t * rpe, LANES)
    gtr = gt_f.reshape(batch * rpe, LANES)
    for k in range(N_EXPERT_TILES):
        xr = peer_up_tile(k, idx, coef, cnt, xr, gtr, up2, seq)
    return xr.reshape(t, d)


def _final_norm_kernel(x_ref, g_ref, o_ref):
    x = x_ref[...]
    o_ref[...] = x * lax.rsqrt(jnp.mean(x * x, axis=-1, keepdims=True) + EPS) * g_ref[...]


def final_rmsnorm(x2, gain, *, tm=512):
    t, d = x2.shape
    return pl.pallas_call(
        _final_norm_kernel,
        grid=(t // tm,),
        in_specs=[pl.BlockSpec((tm, d), lambda i: (i, 0)), pl.BlockSpec((1, d), lambda i: (0, 0))],
        out_specs=pl.BlockSpec((tm, d), lambda i: (i, 0)),
        out_shape=jax.ShapeDtypeStruct((t, d), F32),
        compiler_params=_cparams("parallel"),
        name="final_norm",
    )(x2, gain.reshape(1, d))


def kernel(x, c, norm_mix, norm_ffn, ada_w, ada_b, even_w_in, even_w_out, ret_norm, diff_lambda, diff_norm, odd_w_in, odd_w_out, gdn_conv, gdn_a_log, gdn_dt_bias, gdn_norm, mla_q_norm, mla_w_uq, mla_kv_norm, mla_w_ukv, peer_w_query, peer_sub_keys, peer_down, peer_up, final_norm):
    p = dict(norm_mix=norm_mix, norm_ffn=norm_ffn, even_w_in=even_w_in, even_w_out=even_w_out, ret_norm=ret_norm,
             diff_lambda=diff_lambda, diff_norm=diff_norm, odd_w_in=odd_w_in, odd_w_out=odd_w_out, gdn_conv=gdn_conv,
             gdn_a_log=gdn_a_log, gdn_dt_bias=gdn_dt_bias, gdn_norm=gdn_norm, mla_q_norm=mla_q_norm,
             mla_w_uq=mla_w_uq, mla_kv_norm=mla_kv_norm, mla_w_ukv=mla_w_ukv, peer_w_query=peer_w_query,
             peer_sub_keys=peer_sub_keys, peer_down=peer_down, peer_up=peer_up)
    batch, seq, d = x.shape
    depth = ada_w.shape[0]
    x2 = x.reshape(batch * seq, d)
    mod = modulation(c, ada_w, ada_b)
    for layer in range(depth):
        if layer % 2 == 0:
            x2 = even_layer_mixer(x2, mod[layer], p, layer, layer // 2, batch, seq)
        else:
            x2 = odd_layer_mixer(x2, mod[layer], p, layer, layer // 2, batch, seq)
        x2 = peer_layer(x2, mod[layer], p, layer, batch, seq)
    return final_rmsnorm(x2, final_norm).reshape(batch, seq, d)
```

```python
import functools
import math

import jax
import jax.numpy as jnp
import numpy as np
from jax import lax
from jax.experimental import pallas as pl
from jax.experimental.pallas import tpu as pltpu

F32 = jnp.float32
BF16 = jnp.bfloat16

CHUNK = 64
EPS = 1e-6
HEAD_DIM = 128
N_HEADS = 8
ROPE_THETA = 10000.0
MLA_Q_RANK = 384
MLA_KV_RANK = 256
MLA_ROPE = 64
N_KEYS = 128
PEER_HEADS = 8
PEER_TOPK = 16
PEER_SLOTS = PEER_HEADS * PEER_TOPK

LANES = 128
SUBLANES = 8
VMEM_LIMIT_BYTES = 56 * 1024 * 1024

N_EXPERT_TILES = 4
EXPERT_TILE = (N_KEYS * N_KEYS) // N_EXPERT_TILES
KEY_BITS = int(math.log2(N_KEYS))
TILE_BITS = int(math.log2(N_EXPERT_TILES))
SLOT_GROUP = 8

NEG_INF = float("-inf")


def _cparams(*sem):
    return pltpu.CompilerParams(dimension_semantics=sem, vmem_limit_bytes=VMEM_LIMIT_BYTES)


def _mod_kernel(c_ref, w_ref, b_ref, o_ref):
    c = c_ref[...]
    cond = c * jax.nn.sigmoid(c)
    o_ref[0] = jnp.dot(cond, w_ref[0], preferred_element_type=F32) + b_ref[0]


def modulation(c, ada_w, ada_b):
    depth, d, n = ada_w.shape
    b = c.shape[0]
    tn = 1536
    return pl.pallas_call(
        _mod_kernel,
        grid=(depth, n // tn),
        in_specs=[pl.BlockSpec((b, d), lambda l, j: (0, 0)),
                  pl.BlockSpec((1, d, tn), lambda l, j: (l, 0, j)),
                  pl.BlockSpec((1, 1, tn), lambda l, j: (l, 0, j))],
        out_specs=pl.BlockSpec((1, b, tn), lambda l, j: (l, 0, j)),
        out_shape=jax.ShapeDtypeStruct((depth, b, n), F32),
        compiler_params=_cparams("parallel", "parallel"),
        name="modulation",
    )(c, ada_w, ada_b.reshape(depth, 1, n))


def _norm_matmul_kernel(x_ref, g_ref, sc_ref, sh_ref, w_ref, *rest, emit_h):
    if emit_h:
        y_ref, h_ref, h_scr = rest
    else:
        y_ref, h_scr = rest

    @pl.when(pl.program_id(1) == 0)
    def _():
        x = x_ref[...]
        xn = x * lax.rsqrt(jnp.mean(x * x, axis=-1, keepdims=True) + EPS)
        h = xn * g_ref[...] * (1.0 + sc_ref[0]) + sh_ref[0]
        h_scr[...] = h.astype(BF16)
        if emit_h:
            h_ref[...] = h

    y_ref[...] = jnp.dot(h_scr[...], w_ref[...], preferred_element_type=F32)


def norm_matmul(x2, gain, scale, shift, w_bf16, seq, *, emit_h=False, tm=1024, tn=512):
    t, d = x2.shape
    n = w_bf16.shape[1]
    tm = min(tm, seq)
    assert seq % tm == 0 and n % tn == 0
    per_seq = seq // tm
    bvec = lambda i, j: (i // per_seq, 0, 0)
    out_shape = [jax.ShapeDtypeStruct((t, n), F32)]
    out_specs = [pl.BlockSpec((tm, tn), lambda i, j: (i, j))]
    if emit_h:
        out_shape.append(jax.ShapeDtypeStruct((t, d), F32))
        out_specs.append(pl.BlockSpec((tm, d), lambda i, j: (i, 0)))
    res = pl.pallas_call(
        functools.partial(_norm_matmul_kernel, emit_h=emit_h),
        grid=(t // tm, n // tn),
        in_specs=[pl.BlockSpec((tm, d), lambda i, j: (i, 0)),
                  pl.BlockSpec((1, d), lambda i, j: (0, 0)),
                  pl.BlockSpec((1, 1, d), bvec),
                  pl.BlockSpec((1, 1, d), bvec),
                  pl.BlockSpec((d, tn), lambda i, j: (0, j))],
        out_specs=out_specs,
        out_shape=out_shape,
        scratch_shapes=[pltpu.VMEM((tm, d), BF16)],
        compiler_params=_cparams("parallel", "arbitrary"),
        name="norm_matmul",
    )(x2, gain.reshape(1, d), scale[:, None, :], shift[:, None, :], w_bf16)
    return res if emit_h else res[0]


def _out_proj_kernel(a1_ref, a2_ref, w1_ref, w2_ref, x_ref, gt_ref, o_ref):
    y = jnp.dot(a1_ref[...], w1_ref[...], preferred_element_type=F32)
    y = y + jnp.dot(a2_ref[...], w2_ref[...], preferred_element_type=F32)
    o_ref[...] = x_ref[...] + gt_ref[0] * y


def out_proj_residual(a1, a2, w_bf16, x2, gate, seq, *, tm=1024, tn=512):
    t, k1 = a1.shape
    k2 = a2.shape[1]
    d = x2.shape[1]
    tm = min(tm, seq)
    per_seq = seq // tm
    return pl.pallas_call(
        _out_proj_kernel,
        grid=(t // tm, d // tn),
        in_specs=[pl.BlockSpec((tm, k1), lambda i, j: (i, 0)),
                  pl.BlockSpec((tm, k2), lambda i, j: (i, 0)),
                  pl.BlockSpec((k1, tn), lambda i, j: (0, j)),
                  pl.BlockSpec((k2, tn), lambda i, j: (0, j)),
                  pl.BlockSpec((tm, tn), lambda i, j: (i, j)),
                  pl.BlockSpec((1, 1, tn), lambda i, j: (i // per_seq, 0, j))],
        out_specs=pl.BlockSpec((tm, tn), lambda i, j: (i, j)),
        out_shape=jax.ShapeDtypeStruct((t, d), F32),
        compiler_params=_cparams("parallel", "parallel"),
        name="out_proj_residual",
    )(a1, a2, w_bf16[:k1], w_bf16[k1:], x2, gate[:, None, :])


def _dot_nt(a, b):
    return lax.dot_general(a, b, (((1,), (1,)), ((), ())), preferred_element_type=F32)


def _dot_tn(a, b):
    return lax.dot_general(a, b, (((0,), (0,)), ((), ())), preferred_element_type=F32)


def _retention_kernel(q_ref, k_ref, v_ref, g_ref, intra_ref, qdec_ref, kdec_ref, cdec_ref, norm_ref,
                      o_ref, state):
    @pl.when(pl.program_id(2) == 0)
    def _():
        state[...] = jnp.zeros_like(state)

    n_chunks = q_ref.shape[0] // CHUNK
    intra = intra_ref[0]
    qdec = qdec_ref[0]
    kdec = kdec_ref[0]
    cdec = cdec_ref[0]
    for c in range(n_chunks):
        rows = pl.ds(c * CHUNK, CHUNK)
        qc = q_ref[rows, :]
        kc = k_ref[rows, :] * (HEAD_DIM ** -0.5)
        vc = v_ref[rows, :]
        st = state[...]
        s = _dot_nt(qc, kc) * intra
        o = jnp.dot(s, vc, preferred_element_type=F32) + jnp.dot(qc * qdec, st, preferred_element_type=F32)
        state[...] = st * cdec + _dot_tn(kc * kdec, vc)
        on = o * lax.rsqrt(jnp.mean(o * o, axis=-1, keepdims=True) + EPS) * norm_ref[...]
        g = g_ref[rows, :]
        o_ref[rows, :] = (on * (g * jax.nn.sigmoid(g))).astype(o_ref.dtype)


def retention_mixer(proj, ret_norm, batch, seq, *, ts=512):
    t = proj.shape[0]
    ts = min(ts, seq)
    per_seq = seq // ts
    h8 = N_HEADS
    pos = jnp.arange(CHUNK, dtype=F32)
    lg = jnp.log1p(-jnp.exp2(-5.0 - jnp.arange(h8, dtype=F32)))[:, None]
    intra = jnp.exp(lg[..., None] * jnp.abs(pos[:, None] - pos[None, :]))
    qdec = jnp.broadcast_to(jnp.exp(lg * (pos + 1.0))[..., None], (h8, CHUNK, HEAD_DIM))
    kdec = jnp.broadcast_to(jnp.exp(lg * (CHUNK - 1.0 - pos))[..., None], (h8, CHUNK, HEAD_DIM))
    cdec = jnp.broadcast_to(jnp.exp(lg * CHUNK)[..., None], (h8, HEAD_DIM, HEAD_DIM))

    def col(off):
        return pl.BlockSpec((ts, HEAD_DIM), lambda b, h, s, off=off: (b * per_seq + s, off + h))

    hspec = lambda shape: pl.BlockSpec((1,) + shape, lambda b, h, s: (h, 0, 0))
    return pl.pallas_call(
        _retention_kernel,
        grid=(batch, h8, per_seq),
        in_specs=[col(0), col(h8), col(2 * h8), col(3 * h8),
                  hspec((CHUNK, CHUNK)), hspec((CHUNK, HEAD_DIM)), hspec((CHUNK, HEAD_DIM)),
                  hspec((HEAD_DIM, HEAD_DIM)),
                  pl.BlockSpec((1, HEAD_DIM), lambda b, h, s: (0, 0))],
        out_specs=pl.BlockSpec((ts, HEAD_DIM), lambda b, h, s: (b * per_seq + s, h)),
        out_shape=jax.ShapeDtypeStruct((t, h8 * HEAD_DIM), BF16),
        scratch_shapes=[pltpu.VMEM((HEAD_DIM, HEAD_DIM), F32)],
        compiler_params=_cparams("parallel", "parallel", "arbitrary"),
        name="retention",
    )(proj, proj, proj, proj, intra, qdec, kdec, cdec, ret_norm.reshape(1, HEAD_DIM))


ATTN_TQ = 256


def _diag_mask(tq):
    kc = lax.broadcasted_iota(jnp.int32, (tq, tq), 0) // CHUNK
    qc = lax.broadcasted_iota(jnp.int32, (tq, tq), 1) // CHUNK
    return kc <= qc


def _masked_softmax_t(s_t, mask):
    tq = mask.shape[0]
    ke = s_t.shape[0]
    diag = jnp.where(mask, s_t[ke - tq:], NEG_INF)
    s_t = diag if ke == tq else jnp.concatenate([s_t[:ke - tq], diag], axis=0)
    e = jnp.exp(s_t - jnp.max(s_t, axis=0, keepdims=True))
    return e, 1.0 / jnp.sum(e, axis=0, keepdims=True)


def _diff_attn_kernel(lam_ref, q_ref, k_ref, v_ref, norm_ref, o_ref, *, tq, lambda_init):
    seq = q_ref.shape[0]
    k = k_ref[...]
    half = lax.broadcasted_iota(jnp.int32, k.shape, 1) < (HEAD_DIM // 2)
    k_maps = (jnp.where(half, k, 0.0).astype(BF16), jnp.where(half, 0.0, k).astype(BF16))
    v_t = v_ref[...].T.astype(BF16)
    scale = (HEAD_DIM // 2) ** -0.5
    dl = lam_ref[...]
    lam = (jnp.exp(jnp.sum(dl[0:1] * dl[1:2], axis=-1, keepdims=True))
           - jnp.exp(jnp.sum(dl[2:3] * dl[3:4], axis=-1, keepdims=True)) + lambda_init)
    mask = _diag_mask(tq)
    for qb in range(seq // tq):
        ke = (qb + 1) * tq
        rows = slice(qb * tq, ke)
        q = q_ref[rows, :].astype(BF16)
        (e0, r0), (e1, r1) = [_masked_softmax_t(_dot_nt(km[:ke], q) * scale, mask) for km in k_maps]
        w_t = (e0 * r0 - e1 * (lam * r1)).astype(BF16)
        o = jnp.dot(v_t[:, :ke], w_t, preferred_element_type=F32).T
        on = o * lax.rsqrt(jnp.mean(o * o, axis=-1, keepdims=True) + EPS) * norm_ref[...]
        o_ref[rows, :] = (on * (1.0 - lambda_init)).astype(o_ref.dtype)


def diff_attention_mixer(proj, diff_lambda, diff_norm, lambda_init, batch, seq):
    t = proj.shape[0]
    col = lambda off: pl.BlockSpec((seq, HEAD_DIM), lambda b, h, off=off: (b, off + h))
    return pl.pallas_call(
        functools.partial(_diff_attn_kernel, tq=min(ATTN_TQ, seq), lambda_init=lambda_init),
        grid=(batch, N_HEADS),
        in_specs=[pl.BlockSpec(diff_lambda.shape, lambda b, h: (0, 0)),
                  col(4 * N_HEADS), col(5 * N_HEADS), col(6 * N_HEADS),
                  pl.BlockSpec((1, HEAD_DIM), lambda b, h: (0, 0))],
        out_specs=pl.BlockSpec((seq, HEAD_DIM), lambda b, h: (b, h)),
        out_shape=jax.ShapeDtypeStruct((t, N_HEADS * HEAD_DIM), BF16),
        compiler_params=_cparams("parallel", "parallel"),
        name="diff_attention",
    )(diff_lambda, proj, proj, proj, diff_norm.reshape(1, HEAD_DIM))


def _split_mod(mod):
    return jnp.split(mod, 6, axis=-1)


def even_layer_mixer(x2, mod, p, layer, j, batch, seq):
    sh_m, sc_m, gt_m, _, _, _ = _split_mod(mod)
    proj = norm_matmul(x2, p["norm_mix"][layer], sc_m, sh_m, p["even_w_in"][j].astype(BF16), seq)
    ret = retention_mixer(proj, p["ret_norm"][j], batch, seq)
    lambda_init = 0.8 - 0.6 * math.exp(-0.3 * layer)
    dif = diff_attention_mixer(proj, p["diff_lambda"][j], p["diff_norm"][j], lambda_init, batch, seq)
    return out_proj_residual(ret, dif, p["even_w_out"][j].astype(BF16), x2, gt_m, seq)


ODD_COLS = 40 * LANES
GATE_BLOCK = 4 * N_HEADS


def _odd_w_in_layout(w):
    d = w.shape[0]
    hd = N_HEADS * HEAD_DIM
    zeros = lambda n: jnp.zeros((d, n), w.dtype)
    o = 4 * hd
    gates = w[:, o:o + 2 * N_HEADS]
    o += 2 * N_HEADS
    cq = w[:, o:o + MLA_Q_RANK]
    o += MLA_Q_RANK
    ckv = w[:, o:o + MLA_KV_RANK]
    o += MLA_KV_RANK
    kr = w[:, o:o + MLA_ROPE]
    return jnp.concatenate([w[:, :4 * hd], gates, zeros(LANES - 2 * N_HEADS), cq, ckv, kr,
                            zeros(LANES - MLA_ROPE), zeros(LANES)], axis=1)


def _silu(x):
    return x * jax.nn.sigmoid(x)


def _gdn_conv_kernel(x_ref, w_ref, o_ref):
    c = pl.program_id(1)
    x = x_ref[...]
    w = w_ref[...]
    width = w.shape[0]
    row = lax.broadcasted_iota(jnp.int32, x.shape, 0)
    y = x * w[width - 1:width]
    for sft in range(1, width):
        xs = jnp.where(row >= sft, pltpu.roll(x, sft, axis=0), 0.0)
        y = y + xs * w[width - 1 - sft:width - sft]
    y = _silu(y)
    yn = y * lax.rsqrt(jnp.sum(y * y, axis=-1, keepdims=True) + EPS)
    yn = yn * jnp.where(c < N_HEADS, HEAD_DIM ** -0.5, 1.0)
    o_ref[...] = jnp.where(c < 2 * N_HEADS, yn, y)


def gdn_conv(proj, conv_w, batch, seq):
    t = proj.shape[0]
    n_blk = 3 * N_HEADS
    return pl.pallas_call(
        _gdn_conv_kernel,
        grid=(batch, n_blk),
        in_specs=[pl.BlockSpec((seq, LANES), lambda b, c: (b, c)),
                  pl.BlockSpec((conv_w.shape[0], LANES), lambda b, c: (0, c))],
        out_specs=pl.BlockSpec((seq, LANES), lambda b, c: (b, c)),
        out_shape=jax.ShapeDtypeStruct((t, n_blk * LANES), F32),
        compiler_params=_cparams("parallel", "parallel"),
        name="gdn_conv",
    )(proj, conv_w)


def _dot_hi(a, b):
    return jnp.dot(a, b, preferred_element_type=F32, precision=lax.Precision.HIGHEST)


def _unit_lower_inverses(l_mats):
    n = l_mats[0].shape[0]
    eye = (lax.broadcasted_iota(jnp.int32, (n, n), 0) == lax.broadcasted_iota(jnp.int32, (n, n), 1)).astype(F32)
    invs = [eye - l for l in l_mats]
    powers = [_dot_hi(l, l) for l in l_mats]
    span = 2
    while span < n:
        invs = [inv + _dot_hi(inv, p) for inv, p in zip(invs, powers)]
        span *= 2
        if span < n:
            powers = [_dot_hi(p, p) for p in powers]
    return invs


def _pick_lane(x, lane):
    sel = lax.broadcasted_iota(jnp.int32, x.shape, 1) == lane
    return jnp.sum(jnp.where(sel, x, 0.0), axis=-1, keepdims=True)


def _gdn_kernel(q_ref, k_ref, v_ref, gg_ref, gate_ref, alog_ref, dtb_ref, norm_ref, o_ref, state):
    h = pl.program_id(1)

    @pl.when(pl.program_id(2) == 0)
    def _():
        state[...] = jnp.zeros_like(state)

    n_chunks = q_ref.shape[0] // CHUNK
    ri = lax.broadcasted_iota(jnp.int32, (CHUNK, CHUNK), 0)
    ci = lax.broadcasted_iota(jnp.int32, (CHUNK, CHUNK), 1)
    tri = ci <= ri
    strict = ci < ri
    tri_f = tri.astype(F32)
    a_coef = -jnp.exp(_pick_lane(alog_ref[...], h))
    dt_bias = _pick_lane(dtb_ref[...], h)
    chunk_rows = [pl.ds(c * CHUNK, CHUNK) for c in range(n_chunks)]

    def local(rows):
        k = k_ref[rows, :]
        raw = gate_ref[rows, :]
        beta = jax.nn.sigmoid(_pick_lane(raw, h))
        ga = _pick_lane(raw, N_HEADS + h) + dt_bias
        softplus = jnp.maximum(ga, 0.0) + jnp.log1p(jnp.exp(-jnp.abs(ga)))
        g = jnp.broadcast_to(a_coef * softplus, (CHUNK, HEAD_DIM))
        gc = _dot_hi(tri_f, g)
        g_rows = gc.T[:CHUNK, :]
        decay = jnp.where(tri, jnp.exp(jnp.where(tri, gc[:, :CHUNK] - g_rows, 0.0)), 0.0)
        k_beta = k * beta
        l_mat = jnp.where(strict, _dot_nt(k_beta, k) * decay, 0.0)
        return k, beta, gc, decay, k_beta, l_mat

    locs = [local(rows) for rows in chunk_rows]
    t_invs = _unit_lower_inverses([loc[-1] for loc in locs])
    preps = []
    for rows, (k, beta, gc, decay, k_beta, _), t_inv in zip(chunk_rows, locs, t_invs):
        q = q_ref[rows, :]
        eg = jnp.exp(gc)
        u = jnp.dot(t_inv, v_ref[rows, :] * beta, preferred_element_type=F32)
        w = jnp.dot(t_inv, k_beta * eg, preferred_element_type=F32)
        g_last = gc[CHUNK - 1:CHUNK, :]
        preps.append((u, w, _dot_nt(q, k) * decay, q * eg, k * jnp.exp(g_last - gc), jnp.exp(g_last)))

    for rows, (u, w, intra, q_dec, k_dec, chunk_dec) in zip(chunk_rows, preps):
        st = state[...]
        v_new = u - jnp.dot(w, st, preferred_element_type=F32)
        o = jnp.dot(q_dec, st, preferred_element_type=F32) + jnp.dot(intra, v_new, preferred_element_type=F32)
        state[...] = st * chunk_dec + _dot_tn(k_dec, v_new)
        on = o * lax.rsqrt(jnp.mean(o * o, axis=-1, keepdims=True) + EPS) * norm_ref[...]
        o_ref[rows, :] = (on * _silu(gg_ref[rows, :])).astype(o_ref.dtype)


def gdn_mixer(qkv, proj, a_log, dt_bias, gdn_norm, batch, seq, *, ts=512):
    t = qkv.shape[0]
    ts = min(ts, seq)
    per_seq = seq // ts
    pad = lambda v: jnp.pad(v.reshape(1, -1), ((0, 0), (0, LANES - v.shape[0])))

    def col(off):
        return pl.BlockSpec((ts, HEAD_DIM), lambda b, h, s, off=off: (b * per_seq + s, off + h))

    row1 = pl.BlockSpec((1, LANES), lambda b, h, s: (0, 0))
    return pl.pallas_call(
        _gdn_kernel,
        grid=(batch, N_HEADS, per_seq),
        in_specs=[col(0), col(N_HEADS), col(2 * N_HEADS), col(3 * N_HEADS),
                  pl.BlockSpec((ts, LANES), lambda b, h, s: (b * per_seq + s, GATE_BLOCK)),
                  row1, row1, row1],
        out_specs=pl.BlockSpec((ts, HEAD_DIM), lambda b, h, s: (b * per_seq + s, h)),
        out_shape=jax.ShapeDtypeStruct((t, N_HEADS * HEAD_DIM), BF16),
        scratch_shapes=[pltpu.VMEM((HEAD_DIM, HEAD_DIM), F32)],
        compiler_params=_cparams("parallel", "parallel", "arbitrary"),
        name="gdn",
    )(qkv, qkv, qkv, proj, proj, pad(a_log), pad(dt_bias), gdn_norm.reshape(1, HEAD_DIM))


def _rope_tables(seq):
    half = MLA_ROPE // 2
    inv_freq = ROPE_THETA ** (-jnp.arange(half, dtype=F32) / half)
    ang = jnp.arange(seq, dtype=jnp.int32).astype(F32)[:, None] * inv_freq[None, :]
    cos, sin = jnp.cos(ang), jnp.sin(ang)
    z = jnp.zeros((seq, LANES - MLA_ROPE), F32)
    zh = jnp.zeros((seq, half), F32)
    return (jnp.concatenate([cos, cos, z], axis=1),
            jnp.concatenate([zh, sin, z], axis=1),
            jnp.concatenate([-sin, zh, z], axis=1))


def _rope(y, cos, sin_up, sin_dn):
    half = MLA_ROPE // 2
    return y * cos + pltpu.roll(y, half, axis=1) * sin_up + pltpu.roll(y, LANES - half, axis=1) * sin_dn


def _mla_proj_kernel(cq_ref, ckv_ref, kr_ref, qg_ref, kvg_ref, wqn_ref, wqr_ref, wkv_ref, cos_ref, sup_ref, sdn_ref,
                     qn_ref, qr_ref, kv_ref, kro_ref):
    def rms(x, g):
        return x * lax.rsqrt(jnp.mean(x * x, axis=-1, keepdims=True) + EPS) * g

    cos, sup, sdn = cos_ref[...], sup_ref[...], sdn_ref[...]
    cq = rms(cq_ref[...], qg_ref[...]).astype(BF16)
    qn_ref[...] = jnp.dot(cq, wqn_ref[...], preferred_element_type=F32).astype(qn_ref.dtype)
    qr = jnp.dot(cq, wqr_ref[...], preferred_element_type=F32)
    for h in range(N_HEADS):
        cols = slice(h * LANES, (h + 1) * LANES)
        qr_ref[:, cols] = _rope(qr[:, cols], cos, sup, sdn).astype(qr_ref.dtype)
    ckv = rms(ckv_ref[...], kvg_ref[...]).astype(BF16)
    kv_ref[...] = jnp.dot(ckv, wkv_ref[...], preferred_element_type=F32).astype(kv_ref.dtype)
    kro_ref[...] = _rope(kr_ref[...], cos, sup, sdn).astype(kro_ref.dtype)


def mla_projection(proj, q_norm, w_uq, kv_norm, w_ukv, seq, *, tm=512):
    t = proj.shape[0]
    tm = min(tm, seq)
    per_seq = seq // tm
    hd = N_HEADS * HEAD_DIM
    w3 = w_uq.reshape(MLA_Q_RANK, N_HEADS, HEAD_DIM + MLA_ROPE)
    wqn = w3[:, :, :HEAD_DIM].reshape(MLA_Q_RANK, hd).astype(BF16)
    wqr = jnp.pad(w3[:, :, HEAD_DIM:], ((0, 0), (0, 0), (0, LANES - MLA_ROPE))).reshape(MLA_Q_RANK, hd).astype(BF16)
    cos, sup, sdn = _rope_tables(seq)
    full = lambda a: pl.BlockSpec(a.shape, lambda i: (0,) * a.ndim)
    tab = pl.BlockSpec((tm, LANES), lambda i: (i % per_seq, 0))
    wkv = w_ukv.astype(BF16)
    qg, kvg = q_norm.reshape(1, -1), kv_norm.reshape(1, -1)
    return pl.pallas_call(
        _mla_proj_kernel,
        grid=(t // tm,),
        in_specs=[pl.BlockSpec((tm, MLA_Q_RANK), lambda i: (i, (GATE_BLOCK + 1) * LANES // MLA_Q_RANK)),
                  pl.BlockSpec((tm, MLA_KV_RANK), lambda i: (i, (GATE_BLOCK + 4) * LANES // MLA_KV_RANK)),
                  pl.BlockSpec((tm, LANES), lambda i: (i, GATE_BLOCK + 6)),
                  full(qg), full(kvg), full(wqn), full(wqr), full(wkv), tab, tab, tab],
        out_specs=[pl.BlockSpec((tm, hd), lambda i: (i, 0)), pl.BlockSpec((tm, hd), lambda i: (i, 0)),
                   pl.BlockSpec((tm, 2 * hd), lambda i: (i, 0)), pl.BlockSpec((tm, LANES), lambda i: (i, 0))],
        out_shape=[jax.ShapeDtypeStruct((t, hd), BF16), jax.ShapeDtypeStruct((t, hd), BF16),
                   jax.ShapeDtypeStruct((t, 2 * hd), BF16), jax.ShapeDtypeStruct((t, LANES), BF16)],
        compiler_params=_cparams("parallel"),
        name="mla_projection",
    )(proj, proj, proj, qg, kvg, wqn, wqr, wkv, cos, sup, sdn)


def _mla_attn_kernel(qn_ref, qr_ref, kn_ref, kr_ref, v_ref, o_ref, *, tq):
    seq = qn_ref.shape[0]
    kn = kn_ref[...]
    kr = kr_ref[...]
    v_t = v_ref[...].T
    scale = (HEAD_DIM + MLA_ROPE) ** -0.5
    mask = _diag_mask(tq)
    for qb in range(seq // tq):
        ke = (qb + 1) * tq
        rows = slice(qb * tq, ke)
        s_t = (_dot_nt(kn[:ke], qn_ref[rows, :]) + _dot_nt(kr[:ke], qr_ref[rows, :])) * scale
        e, r = _masked_softmax_t(s_t, mask)
        o_t = jnp.dot(v_t[:, :ke], (e * r).astype(BF16), preferred_element_type=F32)
        o_ref[rows, :] = o_t.T.astype(o_ref.dtype)


def mla_attention_mixer(qn, qr, kv, kr, batch, seq):
    t = qn.shape[0]
    qspec = pl.BlockSpec((seq, HEAD_DIM), lambda b, h: (b, h))
    return pl.pallas_call(
        functools.partial(_mla_attn_kernel, tq=min(ATTN_TQ, seq)),
        grid=(batch, N_HEADS),
        in_specs=[qspec, qspec,
                  pl.BlockSpec((seq, HEAD_DIM), lambda b, h: (b, 2 * h)),
                  pl.BlockSpec((seq, LANES), lambda b, h: (b, 0)),
                  pl.BlockSpec((seq, HEAD_DIM), lambda b, h: (b, 2 * h + 1))],
        out_specs=qspec,
        out_shape=jax.ShapeDtypeStruct((t, N_HEADS * HEAD_DIM), BF16),
        compiler_params=_cparams("parallel", "parallel"),
        name="mla_attention",
    )(qn, qr, kv, kr, kv)


def odd_layer_mixer(x2, mod, p, layer, j, batch, seq):
    sh_m, sc_m, gt_m, _, _, _ = _split_mod(mod)
    w_in = _odd_w_in_layout(p["odd_w_in"][j]).astype(BF16)
    proj = norm_matmul(x2, p["norm_mix"][layer], sc_m, sh_m, w_in, seq)
    qkv = gdn_conv(proj, p["gdn_conv"][j], batch, seq)
    gdn = gdn_mixer(qkv, proj, p["gdn_a_log"][j], p["gdn_dt_bias"][j], p["gdn_norm"][j], batch, seq)
    qn, qr, kv, kr = mla_projection(proj, p["mla_q_norm"][j], p["mla_w_uq"][j], p["mla_kv_norm"][j],
                                    p["mla_w_ukv"][j], seq)
    mla = mla_attention_mixer(qn, qr, kv, kr, batch, seq)
    return out_proj_residual(gdn, mla, p["odd_w_out"][j].astype(BF16), x2, gt_m, seq)


ROUTE_TOKENS = 128


def _top16(*problems):
    n, tt = problems[0].shape
    row = lax.broadcasted_iota(jnp.int32, (n, tt), 0).astype(F32)
    r16 = lax.broadcasted_iota(jnp.int32, (PEER_TOPK, tt), 0)

    def body(k, carry):
        hit = r16 == k
        out = []
        for s, vals, ids in carry:
            m = jnp.max(s, axis=0, keepdims=True)
            i = jnp.min(jnp.where(s == m, row, float(n)), axis=0, keepdims=True)
            out.append((jnp.where(row == i, NEG_INF, s), jnp.where(hit, m, vals), jnp.where(hit, i, ids)))
        return tuple(out)

    zeros = jnp.zeros((PEER_TOPK, tt), F32)
    res = lax.fori_loop(0, PEER_TOPK, body, tuple((s, zeros, zeros) for s in problems))
    res = [(vals, ids) for _, vals, ids in res]
    return res[0] if len(problems) == 1 else res


def _pruned_candidates(v1, v2):
    tt = v1.shape[1]
    sub = lax.broadcasted_iota(jnp.int32, (SUBLANES, tt), 0)
    row = lambda v, r: v[r:r + 1, :]
    blocks = [row(v1, 0) + v2[:SUBLANES], row(v1, 0) + v2[SUBLANES:]]
    blocks += [row(v1, a) + v2[:SUBLANES] for a in range(1, 5)]
    a567 = jnp.where(sub < 2, row(v1, 5), jnp.where(sub < 4, row(v1, 6), row(v1, 7)))
    b01 = jnp.where((sub & 1) == 0, row(v2, 0), row(v2, 1))
    blocks.append(jnp.where(sub < 6, a567 + b01, NEG_INF))
    blocks.append(v1[SUBLANES:] + row(v2, 0))
    return jnp.concatenate(blocks, axis=0)


def _candidate_ranks(pos):
    p16 = pos - 16.0
    p48 = pos - 48.0
    a_mid = 1.0 + jnp.floor(p16 * 0.125)
    a_hi = 5.0 + jnp.floor(p48 * 0.5)
    a = jnp.where(pos < 16.0, 0.0, jnp.where(pos < 48.0, a_mid, jnp.where(pos < 56.0, a_hi, p48)))
    b = jnp.where(pos < 16.0, pos, jnp.where(pos < 48.0, p16 - 8.0 * (a_mid - 1.0),
                                             jnp.where(pos < 56.0, p48 - 2.0 * (a_hi - 5.0), 0.0)))
    return a, b


def _select_rows(table, sel):
    out = jnp.zeros_like(sel)
    for a in range(table.shape[0]):
        out = out + jnp.where(sel == float(a), table[a:a + 1, :], 0.0)
    return out


def _route_kernel(q_ref, keys_ref, idx_ref, gate_ref, cnt_ref, idx_all, gate_all, *, rows_per_expert):
    def head_body(hd, _):
        scores = []
        for half in range(2):
            col = pl.multiple_of((2 * hd + half) * LANES, LANES)
            scores.append(_dot_nt(keys_ref[half], q_ref[:, pl.ds(col, LANES)]))
        (v1, i1), (v2, i2) = _top16(*scores)
        best, pos = _top16(_pruned_candidates(v1, v2))
        a_sel, b_sel = _candidate_ranks(pos)
        expert = _select_rows(i1, a_sel) * N_KEYS + _select_rows(i2, b_sel)
        e = jnp.exp(best - jnp.max(best, axis=0, keepdims=True))
        rows = pl.ds(pl.multiple_of(hd * PEER_TOPK, PEER_TOPK), PEER_TOPK)
        idx_all[rows, :] = expert.astype(jnp.int32)
        gate_all[rows, :] = e / jnp.sum(e, axis=0, keepdims=True)
        return 0

    lax.fori_loop(0, PEER_HEADS, head_body, 0)

    idx = idx_all[...]
    gate = gate_all[...]
    key1 = idx >> KEY_BITS
    key2 = idx & (N_KEYS - 1)
    tid = (key1 + key2) & (N_EXPERT_TILES - 1)
    local = ((key1 << (KEY_BITS - TILE_BITS)) + (key2 >> TILE_BITS)) * rows_per_expert
    n = PEER_SLOTS
    before = (lax.broadcasted_iota(jnp.int32, (n, n), 1) < lax.broadcasted_iota(jnp.int32, (n, n), 0)).astype(F32)
    for k in range(N_EXPERT_TILES):
        mem = tid == k
        shift = jnp.dot(before, jnp.where(mem, 0.0, 1.0), preferred_element_type=F32).astype(jnp.int32)
        xi = jnp.where(mem, local, 0)
        xg = jnp.where(mem, gate, 0.0)
        xd = jnp.where(mem, shift, 0)
        for b in range(int(math.log2(n))):
            step = 1 << b
            mv = ((xd >> b) & 1) == 1
            pull = lambda a: pltpu.roll(a, n - step, axis=0)
            inc = pull(mv.astype(jnp.int32)) == 1
            xi = jnp.where(inc, pull(xi), jnp.where(mv, 0, xi))
            xg = jnp.where(inc, pull(xg), jnp.where(mv, 0.0, xg))
            xd = jnp.where(inc, pull(xd), jnp.where(mv, 0, xd))
        idx_ref[k] = xi.T
        gate_ref[k * n:(k + 1) * n, :] = xg
        cnt_ref[k:k + 1, :] = jnp.sum(mem.astype(jnp.int32), axis=0, keepdims=True)


def peer_route(q, sub_keys, rows_per_expert):
    t, d = q.shape
    tt = ROUTE_TOKENS
    n = N_EXPERT_TILES * PEER_SLOTS
    return pl.pallas_call(
        functools.partial(_route_kernel, rows_per_expert=rows_per_expert),
        grid=(t // tt,),
        in_specs=[pl.BlockSpec((tt, d), lambda i: (i, 0)),
                  pl.BlockSpec(sub_keys.shape, lambda i: (0, 0, 0))],
        out_specs=[pl.BlockSpec((N_EXPERT_TILES, tt, PEER_SLOTS), lambda i: (0, i, 0)),
                   pl.BlockSpec((n, tt), lambda i: (0, i)),
                   pl.BlockSpec((N_EXPERT_TILES, tt), lambda i: (0, i))],
        out_shape=[jax.ShapeDtypeStruct((N_EXPERT_TILES, t, PEER_SLOTS), jnp.int32),
                   jax.ShapeDtypeStruct((n, t), F32),
                   jax.ShapeDtypeStruct((N_EXPERT_TILES, t), jnp.int32)],
        scratch_shapes=[pltpu.VMEM((PEER_SLOTS, tt), jnp.int32), pltpu.VMEM((PEER_SLOTS, tt), F32)],
        compiler_params=_cparams("parallel"),
        name="peer_route",
    )(q, sub_keys)


EXPERT_TOKENS = 128


def _n_groups(cnt):
    return (cnt + (SLOT_GROUP - 1)) // SLOT_GROUP


STATIC_SLOTS = 40
STATIC_GROUPS = STATIC_SLOTS // SLOT_GROUP
FOLD_ORDER = (0, 4, 2, 6, 1, 5, 3, 7)


def _rows_of(i, n):
    return pl.ds(pl.multiple_of(i * n, n), n)


def _fold_group(parts):
    sub = lax.broadcasted_iota(jnp.int32, parts[0].shape, 0)
    xs = [parts[i] for i in FOLD_ORDER]
    half = SUBLANES // 2
    while half >= 1:
        keep = (sub & half) == 0
        xs = [jnp.where(keep, x, pltpu.roll(y, half, axis=0)) + jnp.where(keep, pltpu.roll(x, SUBLANES - half, axis=0), y)
              for x, y in zip(xs[0::2], xs[1::2])]
        half //= 2
    return xs[0]


def _down_kernel(idx_ref, cnt_ref, h_ref, tbl_ref, act_ref, part_scr):
    tb = act_ref.shape[1]
    rpe = h_ref.shape[0] // tb
    lane = lax.broadcasted_iota(jnp.int32, (SUBLANES, tb), 1)

    def partial(off, hv):
        p = tbl_ref[pl.ds(pl.multiple_of(off, rpe), rpe), :] * hv
        out = p[:SUBLANES]
        for c in range(1, rpe // SUBLANES):
            out = out + p[c * SUBLANES:(c + 1) * SUBLANES]
        return out

    def group(t, g, hv):
        return _fold_group([partial(idx_ref[t, g * SLOT_GROUP + u], hv) for u in range(SLOT_GROUP)])

    act_ref[STATIC_SLOTS:, :] = jnp.zeros((PEER_SLOTS - STATIC_SLOTS, tb), F32)

    def tok_body(t, _):
        hv = h_ref[_rows_of(t, rpe), :]
        for g in range(STATIC_GROUPS):
            part_scr[_rows_of(t * STATIC_GROUPS + g, SUBLANES), :] = group(t, g, hv)
        n = cnt_ref[0, 0, t]

        @pl.when(n > STATIC_SLOTS)
        def _():
            def group_body(g, _):
                rows = _rows_of(g, SLOT_GROUP)
                col = jnp.sum(group(t, g, hv), axis=-1, keepdims=True)
                act_ref[rows, :] = jnp.where(lane == t, col, act_ref[rows, :])
                return 0

            lax.fori_loop(STATIC_GROUPS, _n_groups(n), group_body, 0)

        return 0

    lax.fori_loop(0, tb, tok_body, 0)

    for g in range(STATIC_GROUPS):
        acc = jnp.zeros((SUBLANES, tb), F32)
        for t in range(tb):
            c = part_scr[(t * STATIC_GROUPS + g) * SUBLANES:(t * STATIC_GROUPS + g + 1) * SUBLANES, :]
            acc = jnp.where(lane == t, jnp.sum(c, axis=-1, keepdims=True), acc)
        act_ref[g * SLOT_GROUP:(g + 1) * SLOT_GROUP, :] = acc


def _smem_spec(block, index_map):
    return pl.BlockSpec(block, index_map, memory_space=pltpu.SMEM)


def _tile_spec(rpe, index_map):
    return pl.BlockSpec((EXPERT_TILE * rpe, LANES), index_map, pipeline_mode=pl.Buffered(1))


def peer_down_acts(idx, cnt, h2, table2):
    rpe = table2.shape[0] // (N_KEYS * N_KEYS)
    t = h2.shape[0] // rpe
    tb = EXPERT_TOKENS
    return pl.pallas_call(
        _down_kernel,
        grid=(N_EXPERT_TILES, t // tb),
        in_specs=[_smem_spec((tb, PEER_SLOTS), lambda k, i: (k * (t // tb) + i, 0)),
                  _smem_spec((1, 1, tb), lambda k, i: (k, 0, i)),
                  pl.BlockSpec((tb * rpe, LANES), lambda k, i: (i, 0)),
                  _tile_spec(rpe, lambda k, i: (k, 0))],
        out_specs=pl.BlockSpec((PEER_SLOTS, tb), lambda k, i: (k, i)),
        out_shape=jax.ShapeDtypeStruct((N_EXPERT_TILES * PEER_SLOTS, t), F32),
        scratch_shapes=[pltpu.VMEM((tb * STATIC_GROUPS * SUBLANES, LANES), F32)],
        compiler_params=_cparams("arbitrary", "arbitrary"),
        name="peer_down",
    )(idx, cnt, h2, table2)


def _coef_kernel(act_ref, gate_ref, o_ref):
    a = act_ref[...]
    coef = gate_ref[...] * (0.5 * a * (1.0 + lax.erf(a * (2.0 ** -0.5))))
    o_ref[...] = coef.T


def peer_coef(act, gate):
    n, t = gate.shape
    tb = 512 if t % 512 == 0 else EXPERT_TOKENS
    per_tile = t // tb
    spec = pl.BlockSpec((PEER_SLOTS, tb), lambda k, i: (k, i))
    return pl.pallas_call(
        _coef_kernel,
        grid=(N_EXPERT_TILES, per_tile),
        in_specs=[spec, spec],
        out_specs=pl.BlockSpec((tb, PEER_SLOTS), lambda k, i: (k * per_tile + i, 0)),
        out_shape=jax.ShapeDtypeStruct((N_EXPERT_TILES * t, PEER_SLOTS), F32),
        compiler_params=_cparams("parallel", "parallel"),
        name="peer_coef",
    )(act, gate)


def _up_kernel(idx_ref, coef_ref, cnt_ref, x_ref, gt_ref, tbl_ref, o_ref):
    tb = cnt_ref.shape[-1]
    rpe = x_ref.shape[0] // tb
    gt = gt_ref[...]

    def term(t, j):
        return coef_ref[t, j] * tbl_ref[pl.ds(pl.multiple_of(idx_ref[t, j], rpe), rpe), :]

    def token_body(t, _):
        accs = [jnp.zeros((rpe, LANES), F32), jnp.zeros((rpe, LANES), F32)]
        for j in range(STATIC_SLOTS):
            accs[j % 2] = accs[j % 2] + term(t, j)

        def group_body(g, acc):
            for u in range(SLOT_GROUP):
                acc = acc + term(t, g * SLOT_GROUP + u)
            return acc

        n_groups = jnp.maximum(_n_groups(cnt_ref[0, 0, t]), STATIC_GROUPS)
        acc = lax.fori_loop(STATIC_GROUPS, n_groups, group_body, accs[0] + accs[1])
        rows = _rows_of(t, rpe)
        o_ref[rows, :] = x_ref[rows, :] + gt * acc
        return 0

    lax.fori_loop(0, tb, token_body, 0)


def peer_up_tile(k, idx, coef, cnt, x2r, gate2r, table2, seq):
    rpe = table2.shape[0] // (N_KEYS * N_KEYS)
    t = x2r.shape[0] // rpe
    tb = min(EXPERT_TOKENS, seq)
    per_seq = seq // tb
    n_blk = t // tb
    return pl.pallas_call(
        _up_kernel,
        grid=(n_blk,),
        in_specs=[_smem_spec((tb, PEER_SLOTS), lambda i: (k * n_blk + i, 0)),
                  _smem_spec((tb, PEER_SLOTS), lambda i: (k * n_blk + i, 0)),
                  _smem_spec((1, 1, tb), lambda i: (k, 0, i)),
                  pl.BlockSpec((tb * rpe, LANES), lambda i: (i, 0)),
                  pl.BlockSpec((rpe, LANES), lambda i: (i // per_seq, 0)),
                  _tile_spec(rpe, lambda i: (k, 0))],
        out_specs=pl.BlockSpec((tb * rpe, LANES), lambda i: (i, 0)),
        out_shape=jax.ShapeDtypeStruct(x2r.shape, F32),
        compiler_params=_cparams("arbitrary"),
        name="peer_up",
    )(idx, coef, cnt, x2r, gate2r, table2)


def _tile_table(table):
    e, d = table.shape
    nt = N_EXPERT_TILES
    per = N_KEYS // nt
    t5 = table.reshape(per, nt, per, nt, d)
    t5 = jnp.stack([jnp.roll(t5[:, s], s, axis=2) for s in range(nt)], axis=1)
    return jnp.moveaxis(t5, 3, 0).reshape(e * (d // LANES), LANES)


def peer_layer(x2, mod, p, layer, batch, seq):
    _, _, _, sh_f, sc_f, gt_f = _split_mod(mod)
    t, d = x2.shape
    rpe = d // LANES
    q, h = norm_matmul(x2, p["norm_ffn"][layer], sc_f, sh_f, p["peer_w_query"][layer].astype(BF16), seq, emit_h=True)
    idx, gate, cnt = peer_route(q, p["peer_sub_keys"][layer], rpe)
    idx = idx.reshape(N_EXPERT_TILES * t, PEER_SLOTS)
    cnt = cnt.reshape(N_EXPERT_TILES, 1, t)
    act = peer_down_acts(idx, cnt, h.reshape(t * rpe, LANES), _tile_table(p["peer_down"][layer]))
    coef = peer_coef(act, gate)
    up2 = _tile_table(p["peer_up"][layer])
    xr = x2.reshape(t * rpe, LANES)
    gtr = gt_f.reshape(batch * rpe, LANES)
    for k in range(N_EXPERT_TILES):
        xr = peer_up_tile(k, idx, coef, cnt, xr, gtr, up2, seq)
    return xr.reshape(t, d)


def _final_norm_kernel(x_ref, g_ref, o_ref):
    x = x_ref[...]
    o_ref[...] = x * lax.rsqrt(jnp.mean(x * x, axis=-1, keepdims=True) + EPS) * g_ref[...]


def final_rmsnorm(x2, gain, *, tm=512):
    t, d = x2.shape
    return pl.pallas_call(
        _final_norm_kernel,
        grid=(t // tm,),
        in_specs=[pl.BlockSpec((tm, d), lambda i: (i, 0)), pl.BlockSpec((1, d), lambda i: (0, 0))],
        out_specs=pl.BlockSpec((tm, d), lambda i: (i, 0)),
        out_shape=jax.ShapeDtypeStruct((t, d), F32),
        compiler_params=_cparams("parallel"),
        name="final_norm",
    )(x2, gain.reshape(1, d))


def kernel(x, c, norm_mix, norm_ffn, ada_w, ada_b, even_w_in, even_w_out, ret_norm, diff_lambda, diff_norm, odd_w_in, odd_w_out, gdn_conv, gdn_a_log, gdn_dt_bias, gdn_norm, mla_q_norm, mla_w_uq, mla_kv_norm, mla_w_ukv, peer_w_query, peer_sub_keys, peer_down, peer_up, final_norm):
    p = dict(norm_mix=norm_mix, norm_ffn=norm_ffn, even_w_in=even_w_in, even_w_out=even_w_out, ret_norm=ret_norm,
             diff_lambda=diff_lambda, diff_norm=diff_norm, odd_w_in=odd_w_in, odd_w_out=odd_w_out, gdn_conv=gdn_conv,
             gdn_a_log=gdn_a_log, gdn_dt_bias=gdn_dt_bias, gdn_norm=gdn_norm, mla_q_norm=mla_q_norm,
             mla_w_uq=mla_w_uq, mla_kv_norm=mla_kv_norm, mla_w_ukv=mla_w_ukv, peer_w_query=peer_w_query,
             peer_sub_keys=peer_sub_keys, peer_down=peer_down, peer_up=peer_up)
    batch, seq, d = x.shape
    depth = ada_w.shape[0]
    x2 = x.reshape(batch * seq, d)
    mod = modulation(c, ada_w, ada_b)
    for layer in range(depth):
        if layer % 2 == 0:
            x2 = even_layer_mixer(x2, mod[layer], p, layer, layer // 2, batch, seq)
        else:
            x2 = odd_layer_mixer(x2, mod[layer], p, layer, layer // 2, batch, seq)
        x2 = peer_layer(x2, mod[layer], p, layer, batch, seq)
    return final_rmsnorm(x2, final_norm).reshape(batch, seq, d)
```

```python
import functools
import math

import jax
import jax.numpy as jnp
import numpy as np
from jax import lax
from jax.experimental import pallas as pl
from jax.experimental.pallas import tpu as pltpu

F32 = jnp.float32
BF16 = jnp.bfloat16

CHUNK = 64
EPS = 1e-6
HEAD_DIM = 128
N_HEADS = 8
ROPE_THETA = 10000.0
MLA_Q_RANK = 384
MLA_KV_RANK = 256
MLA_ROPE = 64
N_KEYS = 128
PEER_HEADS = 8
PEER_TOPK = 16
PEER_SLOTS = PEER_HEADS * PEER_TOPK

LANES = 128
SUBLANES = 8
VMEM_LIMIT_BYTES = 56 * 1024 * 1024

N_EXPERT_TILES = 4
EXPERT_TILE = (N_KEYS * N_KEYS) // N_EXPERT_TILES
KEY_BITS = int(math.log2(N_KEYS))
TILE_BITS = int(math.log2(N_EXPERT_TILES))
SLOT_GROUP = 8

NEG_INF = float("-inf")


def _cparams(*sem):
    return pltpu.CompilerParams(dimension_semantics=sem, vmem_limit_bytes=VMEM_LIMIT_BYTES)


def _mod_kernel(c_ref, w_ref, b_ref, o_ref):
    c = c_ref[...]
    cond = c * jax.nn.sigmoid(c)
    o_ref[0] = jnp.dot(cond, w_ref[0], preferred_element_type=F32) + b_ref[0]


def modulation(c, ada_w, ada_b):
    depth, d, n = ada_w.shape
    b = c.shape[0]
    tn = 1536
    return pl.pallas_call(
        _mod_kernel,
        grid=(depth, n // tn),
        in_specs=[pl.BlockSpec((b, d), lambda l, j: (0, 0)),
                  pl.BlockSpec((1, d, tn), lambda l, j: (l, 0, j)),
                  pl.BlockSpec((1, 1, tn), lambda l, j: (l, 0, j))],
        out_specs=pl.BlockSpec((1, b, tn), lambda l, j: (l, 0, j)),
        out_shape=jax.ShapeDtypeStruct((depth, b, n), F32),
        compiler_params=_cparams("parallel", "parallel"),
        name="modulation",
    )(c, ada_w, ada_b.reshape(depth, 1, n))


def _norm_matmul_kernel(x_ref, g_ref, sc_ref, sh_ref, w_ref, *rest, emit_h):
    if emit_h:
        y_ref, h_ref, h_scr = rest
    else:
        y_ref, h_scr = rest

    @pl.when(pl.program_id(1) == 0)
    def _():
        x = x_ref[...]
        xn = x * lax.rsqrt(jnp.mean(x * x, axis=-1, keepdims=True) + EPS)
        h = xn * g_ref[...] * (1.0 + sc_ref[0]) + sh_ref[0]
        h_scr[...] = h.astype(BF16)
        if emit_h:
            h_ref[...] = h

    y_ref[...] = jnp.dot(h_scr[...], w_ref[...], preferred_element_type=F32)


def norm_matmul(x2, gain, scale, shift, w_bf16, seq, *, emit_h=False, tm=1024, tn=512):
    t, d = x2.shape
    n = w_bf16.shape[1]
    tm = min(tm, seq)
    assert seq % tm == 0 and n % tn == 0
    per_seq = seq // tm
    bvec = lambda i, j: (i // per_seq, 0, 0)
    out_shape = [jax.ShapeDtypeStruct((t, n), F32)]
    out_specs = [pl.BlockSpec((tm, tn), lambda i, j: (i, j))]
    if emit_h:
        out_shape.append(jax.ShapeDtypeStruct((t, d), F32))
        out_specs.append(pl.BlockSpec((tm, d), lambda i, j: (i, 0)))
    res = pl.pallas_call(
        functools.partial(_norm_matmul_kernel, emit_h=emit_h),
        grid=(t // tm, n // tn),
        in_specs=[pl.BlockSpec((tm, d), lambda i, j: (i, 0)),
                  pl.BlockSpec((1, d), lambda i, j: (0, 0)),
                  pl.BlockSpec((1, 1, d), bvec),
                  pl.BlockSpec((1, 1, d), bvec),
                  pl.BlockSpec((d, tn), lambda i, j: (0, j))],
        out_specs=out_specs,
        out_shape=out_shape,
        scratch_shapes=[pltpu.VMEM((tm, d), BF16)],
        compiler_params=_cparams("parallel", "arbitrary"),
        name="norm_matmul",
    )(x2, gain.reshape(1, d), scale[:, None, :], shift[:, None, :], w_bf16)
    return res if emit_h else res[0]


def _out_proj_kernel(a1_ref, a2_ref, w1_ref, w2_ref, x_ref, gt_ref, o_ref):
    y = jnp.dot(a1_ref[...], w1_ref[...], preferred_element_type=F32)
    y = y + jnp.dot(a2_ref[...], w2_ref[...], preferred_element_type=F32)
    o_ref[...] = x_ref[...] + gt_ref[0] * y


def out_proj_residual(a1, a2, w_bf16, x2, gate, seq, *, tm=1024, tn=512):
    t, k1 = a1.shape
    k2 = a2.shape[1]
    d = x2.shape[1]
    tm = min(tm, seq)
    per_seq = seq // tm
    return pl.pallas_call(
        _out_proj_kernel,
        grid=(t // tm, d // tn),
        in_specs=[pl.BlockSpec((tm, k1), lambda i, j: (i, 0)),
                  pl.BlockSpec((tm, k2), lambda i, j: (i, 0)),
                  pl.BlockSpec((k1, tn), lambda i, j: (0, j)),
                  pl.BlockSpec((k2, tn), lambda i, j: (0, j)),
                  pl.BlockSpec((tm, tn), lambda i, j: (i, j)),
                  pl.BlockSpec((1, 1, tn), lambda i, j: (i // per_seq, 0, j))],
        out_specs=pl.BlockSpec((tm, tn), lambda i, j: (i, j)),
        out_shape=jax.ShapeDtypeStruct((t, d), F32),
        compiler_params=_cparams("parallel", "parallel"),
        name="out_proj_residual",
    )(a1, a2, w_bf16[:k1], w_bf16[k1:], x2, gate[:, None, :])


def _dot_nt(a, b):
    return lax.dot_general(a, b, (((1,), (1,)), ((), ())), preferred_element_type=F32)


def _dot_tn(a, b):
    return lax.dot_general(a, b, (((0,), (0,)), ((), ())), preferred_element_type=F32)


def _retention_kernel(q_ref, k_ref, v_ref, g_ref, intra_ref, qdec_ref, kdec_ref, cdec_ref, norm_ref,
                      o_ref, state):
    @pl.when(pl.program_id(2) == 0)
    def _():
        state[...] = jnp.zeros_like(state)

    n_chunks = q_ref.shape[0] // CHUNK
    intra = intra_ref[0]
    qdec = qdec_ref[0]
    kdec = kdec_ref[0]
    cdec = cdec_ref[0]
    for c in range(n_chunks):
        rows = pl.ds(c * CHUNK, CHUNK)
        qc = q_ref[rows, :]
        kc = k_ref[rows, :] * (HEAD_DIM ** -0.5)
        vc = v_ref[rows, :]
        st = state[...]
        s = _dot_nt(qc, kc) * intra
        o = jnp.dot(s, vc, preferred_element_type=F32) + jnp.dot(qc * qdec, st, preferred_element_type=F32)
        state[...] = st * cdec + _dot_tn(kc * kdec, vc)
        on = o * lax.rsqrt(jnp.mean(o * o, axis=-1, keepdims=True) + EPS) * norm_ref[...]
        g = g_ref[rows, :]
        o_ref[rows, :] = (on * (g * jax.nn.sigmoid(g))).astype(o_ref.dtype)


def retention_mixer(proj, ret_norm, batch, seq, *, ts=512):
    t = proj.shape[0]
    ts = min(ts, seq)
    per_seq = seq // ts
    h8 = N_HEADS
    pos = jnp.arange(CHUNK, dtype=F32)
    lg = jnp.log1p(-jnp.exp2(-5.0 - jnp.arange(h8, dtype=F32)))[:, None]
    intra = jnp.exp(lg[..., None] * jnp.abs(pos[:, None] - pos[None, :]))
    qdec = jnp.broadcast_to(jnp.exp(lg * (pos + 1.0))[..., None], (h8, CHUNK, HEAD_DIM))
    kdec = jnp.broadcast_to(jnp.exp(lg * (CHUNK - 1.0 - pos))[..., None], (h8, CHUNK, HEAD_DIM))
    cdec = jnp.broadcast_to(jnp.exp(lg * CHUNK)[..., None], (h8, HEAD_DIM, HEAD_DIM))

    def col(off):
        return pl.BlockSpec((ts, HEAD_DIM), lambda b, h, s, off=off: (b * per_seq + s, off + h))

    hspec = lambda shape: pl.BlockSpec((1,) + shape, lambda b, h, s: (h, 0, 0))
    return pl.pallas_call(
        _retention_kernel,
        grid=(batch, h8, per_seq),
        in_specs=[col(0), col(h8), col(2 * h8), col(3 * h8),
                  hspec((CHUNK, CHUNK)), hspec((CHUNK, HEAD_DIM)), hspec((CHUNK, HEAD_DIM)),
                  hspec((HEAD_DIM, HEAD_DIM)),
                  pl.BlockSpec((1, HEAD_DIM), lambda b, h, s: (0, 0))],
        out_specs=pl.BlockSpec((ts, HEAD_DIM), lambda b, h, s: (b * per_seq + s, h)),
        out_shape=jax.ShapeDtypeStruct((t, h8 * HEAD_DIM), BF16),
        scratch_shapes=[pltpu.VMEM((HEAD_DIM, HEAD_DIM), F32)],
        compiler_params=_cparams("parallel", "parallel", "arbitrary"),
        name="retention",
    )(proj, proj, proj, proj, intra, qdec, kdec, cdec, ret_norm.reshape(1, HEAD_DIM))


ATTN_TQ = 256


def _diag_mask(tq):
    kc = lax.broadcasted_iota(jnp.int32, (tq, tq), 0) // CHUNK
    qc = lax.broadcasted_iota(jnp.int32, (tq, tq), 1) // CHUNK
    return kc <= qc


def _masked_softmax_t(s_t, mask):
    tq = mask.shape[0]
    ke = s_t.shape[0]
    diag = jnp.where(mask, s_t[ke - tq:], NEG_INF)
    s_t = diag if ke == tq else jnp.concatenate([s_t[:ke - tq], diag], axis=0)
    e = jnp.exp(s_t - jnp.max(s_t, axis=0, keepdims=True))
    return e, 1.0 / jnp.sum(e, axis=0, keepdims=True)


def _diff_attn_kernel(lam_ref, q_ref, k_ref, v_ref, norm_ref, o_ref, *, tq, lambda_init):
    seq = q_ref.shape[0]
    k = k_ref[...]
    half = lax.broadcasted_iota(jnp.int32, k.shape, 1) < (HEAD_DIM // 2)
    k_maps = (jnp.where(half, k, 0.0).astype(BF16), jnp.where(half, 0.0, k).astype(BF16))
    v_t = v_ref[...].T.astype(BF16)
    scale = (HEAD_DIM // 2) ** -0.5
    dl = lam_ref[...]
    lam = (jnp.exp(jnp.sum(dl[0:1] * dl[1:2], axis=-1, keepdims=True))
           - jnp.exp(jnp.sum(dl[2:3] * dl[3:4], axis=-1, keepdims=True)) + lambda_init)
    mask = _diag_mask(tq)
    for qb in range(seq // tq):
        ke = (qb + 1) * tq
        rows = slice(qb * tq, ke)
        q = q_ref[rows, :].astype(BF16)
        (e0, r0), (e1, r1) = [_masked_softmax_t(_dot_nt(km[:ke], q) * scale, mask) for km in k_maps]
        w_t = (e0 * r0 - e1 * (lam * r1)).astype(BF16)
        o = jnp.dot(v_t[:, :ke], w_t, preferred_element_type=F32).T
        on = o * lax.rsqrt(jnp.mean(o * o, axis=-1, keepdims=True) + EPS) * norm_ref[...]
        o_ref[rows, :] = (on * (1.0 - lambda_init)).astype(o_ref.dtype)


def diff_attention_mixer(proj, diff_lambda, diff_norm, lambda_init, batch, seq):
    t = proj.shape[0]
    col = lambda off: pl.BlockSpec((seq, HEAD_DIM), lambda b, h, off=off: (b, off + h))
    return pl.pallas_call(
        functools.partial(_diff_attn_kernel, tq=min(ATTN_TQ, seq), lambda_init=lambda_init),
        grid=(batch, N_HEADS),
        in_specs=[pl.BlockSpec(diff_lambda.shape, lambda b, h: (0, 0)),
                  col(4 * N_HEADS), col(5 * N_HEADS), col(6 * N_HEADS),
                  pl.BlockSpec((1, HEAD_DIM), lambda b, h: (0, 0))],
        out_specs=pl.BlockSpec((seq, HEAD_DIM), lambda b, h: (b, h)),
        out_shape=jax.ShapeDtypeStruct((t, N_HEADS * HEAD_DIM), BF16),
        compiler_params=_cparams("parallel", "parallel"),
        name="diff_attention",
    )(diff_lambda, proj, proj, proj, diff_norm.reshape(1, HEAD_DIM))


def _split_mod(mod):
    return jnp.split(mod, 6, axis=-1)


def even_layer_mixer(x2, mod, p, layer, j, batch, seq):
    sh_m, sc_m, gt_m, _, _, _ = _split_mod(mod)
    proj = norm_matmul(x2, p["norm_mix"][layer], sc_m, sh_m, p["even_w_in"][j].astype(BF16), seq)
    ret = retention_mixer(proj, p["ret_norm"][j], batch, seq)
    lambda_init = 0.8 - 0.6 * math.exp(-0.3 * layer)
    dif = diff_attention_mixer(proj, p["diff_lambda"][j], p["diff_norm"][j], lambda_init, batch, seq)
    return out_proj_residual(ret, dif, p["even_w_out"][j].astype(BF16), x2, gt_m, seq)


ODD_COLS = 40 * LANES
GATE_BLOCK = 4 * N_HEADS


def _odd_w_in_layout(w):
    d = w.shape[0]
    hd = N_HEADS * HEAD_DIM
    zeros = lambda n: jnp.zeros((d, n), w.dtype)
    o = 4 * hd
    gates = w[:, o:o + 2 * N_HEADS]
    o += 2 * N_HEADS
    cq = w[:, o:o + MLA_Q_RANK]
    o += MLA_Q_RANK
    ckv = w[:, o:o + MLA_KV_RANK]
    o += MLA_KV_RANK
    kr = w[:, o:o + MLA_ROPE]
    return jnp.concatenate([w[:, :4 * hd], gates, zeros(LANES - 2 * N_HEADS), cq, ckv, kr,
                            zeros(LANES - MLA_ROPE), zeros(LANES)], axis=1)


def _silu(x):
    return x * jax.nn.sigmoid(x)


def _gdn_conv_kernel(x_ref, w_ref, o_ref):
    c = pl.program_id(1)
    x = x_ref[...]
    w = w_ref[...]
    width = w.shape[0]
    row = lax.broadcasted_iota(jnp.int32, x.shape, 0)
    y = x * w[width - 1:width]
    for sft in range(1, width):
        xs = jnp.where(row >= sft, pltpu.roll(x, sft, axis=0), 0.0)
        y = y + xs * w[width - 1 - sft:width - sft]
    y = _silu(y)
    yn = y * lax.rsqrt(jnp.sum(y * y, axis=-1, keepdims=True) + EPS)
    yn = yn * jnp.where(c < N_HEADS, HEAD_DIM ** -0.5, 1.0)
    o_ref[...] = jnp.where(c < 2 * N_HEADS, yn, y)


def gdn_conv(proj, conv_w, batch, seq):
    t = proj.shape[0]
    n_blk = 3 * N_HEADS
    return pl.pallas_call(
        _gdn_conv_kernel,
        grid=(batch, n_blk),
        in_specs=[pl.BlockSpec((seq, LANES), lambda b, c: (b, c)),
                  pl.BlockSpec((conv_w.shape[0], LANES), lambda b, c: (0, c))],
        out_specs=pl.BlockSpec((seq, LANES), lambda b, c: (b, c)),
        out_shape=jax.ShapeDtypeStruct((t, n_blk * LANES), F32),
        compiler_params=_cparams("parallel", "parallel"),
        name="gdn_conv",
    )(proj, conv_w)


def _dot_hi(a, b):
    return jnp.dot(a, b, preferred_element_type=F32, precision=lax.Precision.HIGHEST)


def _unit_lower_inverses(l_mats):
    n = l_mats[0].shape[0]
    eye = (lax.broadcasted_iota(jnp.int32, (n, n), 0) == lax.broadcasted_iota(jnp.int32, (n, n), 1)).astype(F32)
    invs = [eye - l for l in l_mats]
    powers = [_dot_hi(l, l) for l in l_mats]
    span = 2
    while span < n:
        invs = [inv + _dot_hi(inv, p) for inv, p in zip(invs, powers)]
        span *= 2
        if span < n:
            powers = [_dot_hi(p, p) for p in powers]
    return invs


def _pick_lane(x, lane):
    sel = lax.broadcasted_iota(jnp.int32, x.shape, 1) == lane
    return jnp.sum(jnp.where(sel, x, 0.0), axis=-1, keepdims=True)


def _gdn_kernel(q_ref, k_ref, v_ref, gg_ref, gate_ref, alog_ref, dtb_ref, norm_ref, o_ref, state):
    h = pl.program_id(1)

    @pl.when(pl.program_id(2) == 0)
    def _():
        state[...] = jnp.zeros_like(state)

    n_chunks = q_ref.shape[0] // CHUNK
    ri = lax.broadcasted_iota(jnp.int32, (CHUNK, CHUNK), 0)
    ci = lax.broadcasted_iota(jnp.int32, (CHUNK, CHUNK), 1)
    tri = ci <= ri
    strict = ci < ri
    tri_f = tri.astype(F32)
    a_coef = -jnp.exp(_pick_lane(alog_ref[...], h))
    dt_bias = _pick_lane(dtb_ref[...], h)
    chunk_rows = [pl.ds(c * CHUNK, CHUNK) for c in range(n_chunks)]

    def local(rows):
        k = k_ref[rows, :]
        raw = gate_ref[rows, :]
        beta = jax.nn.sigmoid(_pick_lane(raw, h))
        ga = _pick_lane(raw, N_HEADS + h) + dt_bias
        softplus = jnp.maximum(ga, 0.0) + jnp.log1p(jnp.exp(-jnp.abs(ga)))
        g = jnp.broadcast_to(a_coef * softplus, (CHUNK, HEAD_DIM))
        gc = _dot_hi(tri_f, g)
        g_rows = gc.T[:CHUNK, :]
        decay = jnp.where(tri, jnp.exp(jnp.where(tri, gc[:, :CHUNK] - g_rows, 0.0)), 0.0)
        k_beta = k * beta
        l_mat = jnp.where(strict, _dot_nt(k_beta, k) * decay, 0.0)
        return k, beta, gc, decay, k_beta, l_mat

    locs = [local(rows) for rows in chunk_rows]
    t_invs = _unit_lower_inverses([loc[-1] for loc in locs])
    preps = []
    for rows, (k, beta, gc, decay, k_beta, _), t_inv in zip(chunk_rows, locs, t_invs):
        q = q_ref[rows, :]
        eg = jnp.exp(gc)
        u = jnp.dot(t_inv, v_ref[rows, :] * beta, preferred_element_type=F32)
        w = jnp.dot(t_inv, k_beta * eg, preferred_element_type=F32)
        g_last = gc[CHUNK - 1:CHUNK, :]
        preps.append((u, w, _dot_nt(q, k) * decay, q * eg, k * jnp.exp(g_last - gc), jnp.exp(g_last)))

    for rows, (u, w, intra, q_dec, k_dec, chunk_dec) in zip(chunk_rows, preps):
        st = state[...]
        v_new = u - jnp.dot(w, st, preferred_element_type=F32)
        o = jnp.dot(q_dec, st, preferred_element_type=F32) + jnp.dot(intra, v_new, preferred_element_type=F32)
        state[...] = st * chunk_dec + _dot_tn(k_dec, v_new)
        on = o * lax.rsqrt(jnp.mean(o * o, axis=-1, keepdims=True) + EPS) * norm_ref[...]
        o_ref[rows, :] = (on * _silu(gg_ref[rows, :])).astype(o_ref.dtype)


def gdn_mixer(qkv, proj, a_log, dt_bias, gdn_norm, batch, seq, *, ts=512):
    t = qkv.shape[0]
    ts = min(ts, seq)
    per_seq = seq // ts
    pad = lambda v: jnp.pad(v.reshape(1, -1), ((0, 0), (0, LANES - v.shape[0])))

    def col(off):
        return pl.BlockSpec((ts, HEAD_DIM), lambda b, h, s, off=off: (b * per_seq + s, off + h))

    row1 = pl.BlockSpec((1, LANES), lambda b, h, s: (0, 0))
    return pl.pallas_call(
        _gdn_kernel,
        grid=(batch, N_HEADS, per_seq),
        in_specs=[col(0), col(N_HEADS), col(2 * N_HEADS), col(3 * N_HEADS),
                  pl.BlockSpec((ts, LANES), lambda b, h, s: (b * per_seq + s, GATE_BLOCK)),
                  row1, row1, row1],
        out_specs=pl.BlockSpec((ts, HEAD_DIM), lambda b, h, s: (b * per_seq + s, h)),
        out_shape=jax.ShapeDtypeStruct((t, N_HEADS * HEAD_DIM), BF16),
        scratch_shapes=[pltpu.VMEM((HEAD_DIM, HEAD_DIM), F32)],
        compiler_params=_cparams("parallel", "parallel", "arbitrary"),
        name="gdn",
    )(qkv, qkv, qkv, proj, proj, pad(a_log), pad(dt_bias), gdn_norm.reshape(1, HEAD_DIM))


def _rope_tables(seq):
    half = MLA_ROPE // 2
    inv_freq = ROPE_THETA ** (-jnp.arange(half, dtype=F32) / half)
    ang = jnp.arange(seq, dtype=jnp.int32).astype(F32)[:, None] * inv_freq[None, :]
    cos, sin = jnp.cos(ang), jnp.sin(ang)
    z = jnp.zeros((seq, LANES - MLA_ROPE), F32)
    zh = jnp.zeros((seq, half), F32)
    return (jnp.concatenate([cos, cos, z], axis=1),
            jnp.concatenate([zh, sin, z], axis=1),
            jnp.concatenate([-sin, zh, z], axis=1))


def _rope(y, cos, sin_up, sin_dn):
    half = MLA_ROPE // 2
    return y * cos + pltpu.roll(y, half, axis=1) * sin_up + pltpu.roll(y, LANES - half, axis=1) * sin_dn


def _mla_proj_kernel(cq_ref, ckv_ref, kr_ref, qg_ref, kvg_ref, wqn_ref, wqr_ref, wkv_ref, cos_ref, sup_ref, sdn_ref,
                     qn_ref, qr_ref, kv_ref, kro_ref):
    def rms(x, g):
        return x * lax.rsqrt(jnp.mean(x * x, axis=-1, keepdims=True) + EPS) * g

    cos, sup, sdn = cos_ref[...], sup_ref[...], sdn_ref[...]
    cq = rms(cq_ref[...], qg_ref[...]).astype(BF16)
    qn_ref[...] = jnp.dot(cq, wqn_ref[...], preferred_element_type=F32).astype(qn_ref.dtype)
    qr = jnp.dot(cq, wqr_ref[...], preferred_element_type=F32)
    for h in range(N_HEADS):
        cols = slice(h * LANES, (h + 1) * LANES)
        qr_ref[:, cols] = _rope(qr[:, cols], cos, sup, sdn).astype(qr_ref.dtype)
    ckv = rms(ckv_ref[...], kvg_ref[...]).astype(BF16)
    kv_ref[...] = jnp.dot(ckv, wkv_ref[...], preferred_element_type=F32).astype(kv_ref.dtype)
    kro_ref[...] = _rope(kr_ref[...], cos, sup, sdn).astype(kro_ref.dtype)


def mla_projection(proj, q_norm, w_uq, kv_norm, w_ukv, seq, *, tm=512):
    t = proj.shape[0]
    tm = min(tm, seq)
    per_seq = seq // tm
    hd = N_HEADS * HEAD_DIM
    w3 = w_uq.reshape(MLA_Q_RANK, N_HEADS, HEAD_DIM + MLA_ROPE)
    wqn = w3[:, :, :HEAD_DIM].reshape(MLA_Q_RANK, hd).astype(BF16)
    wqr = jnp.pad(w3[:, :, HEAD_DIM:], ((0, 0), (0, 0), (0, LANES - MLA_ROPE))).reshape(MLA_Q_RANK, hd).astype(BF16)
    cos, sup, sdn = _rope_tables(seq)
    full = lambda a: pl.BlockSpec(a.shape, lambda i: (0,) * a.ndim)
    tab = pl.BlockSpec((tm, LANES), lambda i: (i % per_seq, 0))
    wkv = w_ukv.astype(BF16)
    qg, kvg = q_norm.reshape(1, -1), kv_norm.reshape(1, -1)
    return pl.pallas_call(
        _mla_proj_kernel,
        grid=(t // tm,),
        in_specs=[pl.BlockSpec((tm, MLA_Q_RANK), lambda i: (i, (GATE_BLOCK + 1) * LANES // MLA_Q_RANK)),
                  pl.BlockSpec((tm, MLA_KV_RANK), lambda i: (i, (GATE_BLOCK + 4) * LANES // MLA_KV_RANK)),
                  pl.BlockSpec((tm, LANES), lambda i: (i, GATE_BLOCK + 6)),
                  full(qg), full(kvg), full(wqn), full(wqr), full(wkv), tab, tab, tab],
        out_specs=[pl.BlockSpec((tm, hd), lambda i: (i, 0)), pl.BlockSpec((tm, hd), lambda i: (i, 0)),
                   pl.BlockSpec((tm, 2 * hd), lambda i: (i, 0)), pl.BlockSpec((tm, LANES), lambda i: (i, 0))],
        out_shape=[jax.ShapeDtypeStruct((t, hd), BF16), jax.ShapeDtypeStruct((t, hd), BF16),
                   jax.ShapeDtypeStruct((t, 2 * hd), BF16), jax.ShapeDtypeStruct((t, LANES), BF16)],
        compiler_params=_cparams("parallel"),
        name="mla_projection",
    )(proj, proj, proj, qg, kvg, wqn, wqr, wkv, cos, sup, sdn)


def _mla_attn_kernel(qn_ref, qr_ref, kn_ref, kr_ref, v_ref, o_ref, *, tq):
    seq = qn_ref.shape[0]
    kn = kn_ref[...]
    kr = kr_ref[...]
    v_t = v_ref[...].T
    scale = (HEAD_DIM + MLA_ROPE) ** -0.5
    mask = _diag_mask(tq)
    for qb in range(seq // tq):
        ke = (qb + 1) * tq
        rows = slice(qb * tq, ke)
        s_t = (_dot_nt(kn[:ke], qn_ref[rows, :]) + _dot_nt(kr[:ke], qr_ref[rows, :])) * scale
        e, r = _masked_softmax_t(s_t, mask)
        o_t = jnp.dot(v_t[:, :ke], (e * r).astype(BF16), preferred_element_type=F32)
        o_ref[rows, :] = o_t.T.astype(o_ref.dtype)


def mla_attention_mixer(qn, qr, kv, kr, batch, seq):
    t = qn.shape[0]
    qspec = pl.BlockSpec((seq, HEAD_DIM), lambda b, h: (b, h))
    return pl.pallas_call(
        functools.partial(_mla_attn_kernel, tq=min(ATTN_TQ, seq)),
        grid=(batch, N_HEADS),
        in_specs=[qspec, qspec,
                  pl.BlockSpec((seq, HEAD_DIM), lambda b, h: (b, 2 * h)),
                  pl.BlockSpec((seq, LANES), lambda b, h: (b, 0)),
                  pl.BlockSpec((seq, HEAD_DIM), lambda b, h: (b, 2 * h + 1))],
        out_specs=qspec,
        out_shape=jax.ShapeDtypeStruct((t, N_HEADS * HEAD_DIM), BF16),
        compiler_params=_cparams("parallel", "parallel"),
        name="mla_attention",
    )(qn, qr, kv, kr, kv)


def odd_layer_mixer(x2, mod, p, layer, j, batch, seq):
    sh_m, sc_m, gt_m, _, _, _ = _split_mod(mod)
    w_in = _odd_w_in_layout(p["odd_w_in"][j]).astype(BF16)
    proj = norm_matmul(x2, p["norm_mix"][layer], sc_m, sh_m, w_in, seq)
    qkv = gdn_conv(proj, p["gdn_conv"][j], batch, seq)
    gdn = gdn_mixer(qkv, proj, p["gdn_a_log"][j], p["gdn_dt_bias"][j], p["gdn_norm"][j], batch, seq)
    qn, qr, kv, kr = mla_projection(proj, p["mla_q_norm"][j], p["mla_w_uq"][j], p["mla_kv_norm"][j],
                                    p["mla_w_ukv"][j], seq)
    mla = mla_attention_mixer(qn, qr, kv, kr, batch, seq)
    return out_proj_residual(gdn, mla, p["odd_w_out"][j].astype(BF16), x2, gt_m, seq)


ROUTE_TOKENS = 128


def _top16(*problems):
    n, tt = problems[0].shape
    row = lax.broadcasted_iota(jnp.int32, (n, tt), 0).astype(F32)
    r16 = lax.broadcasted_iota(jnp.int32, (PEER_TOPK, tt), 0)

    def body(k, carry):
        hit = r16 == k
        out = []
        for s, vals, ids in carry:
            m = jnp.max(s, axis=0, keepdims=True)
            i = jnp.min(jnp.where(s == m, row, float(n)), axis=0, keepdims=True)
            out.append((jnp.where(row == i, NEG_INF, s), jnp.where(hit, m, vals), jnp.where(hit, i, ids)))
        return tuple(out)

    zeros = jnp.zeros((PEER_TOPK, tt), F32)
    res = lax.fori_loop(0, PEER_TOPK, body, tuple((s, zeros, zeros) for s in problems))
    res = [(vals, ids) for _, vals, ids in res]
    return res[0] if len(problems) == 1 else res


def _pruned_candidates(v1, v2):
    tt = v1.shape[1]
    sub = lax.broadcasted_iota(jnp.int32, (SUBLANES, tt), 0)
    row = lambda v, r: v[r:r + 1, :]
    blocks = [row(v1, 0) + v2[:SUBLANES], row(v1, 0) + v2[SUBLANES:]]
    blocks += [row(v1, a) + v2[:SUBLANES] for a in range(1, 5)]
    a567 = jnp.where(sub < 2, row(v1, 5), jnp.where(sub < 4, row(v1, 6), row(v1, 7)))
    b01 = jnp.where((sub & 1) == 0, row(v2, 0), row(v2, 1))
    blocks.append(jnp.where(sub < 6, a567 + b01, NEG_INF))
    blocks.append(v1[SUBLANES:] + row(v2, 0))
    return jnp.concatenate(blocks, axis=0)


def _candidate_ranks(pos):
    p16 = pos - 16.0
    p48 = pos - 48.0
    a_mid = 1.0 + jnp.floor(p16 * 0.125)
    a_hi = 5.0 + jnp.floor(p48 * 0.5)
    a = jnp.where(pos < 16.0, 0.0, jnp.where(pos < 48.0, a_mid, jnp.where(pos < 56.0, a_hi, p48)))
    b = jnp.where(pos < 16.0, pos, jnp.where(pos < 48.0, p16 - 8.0 * (a_mid - 1.0),
                                             jnp.where(pos < 56.0, p48 - 2.0 * (a_hi - 5.0), 0.0)))
    return a, b


def _select_rows(table, sel):
    out = jnp.zeros_like(sel)
    for a in range(table.shape[0]):
        out = out + jnp.where(sel == float(a), table[a:a + 1, :], 0.0)
    return out


def _route_kernel(q_ref, keys_ref, idx_ref, gate_ref, cnt_ref, idx_all, gate_all, *, rows_per_expert):
    def head_body(hd, _):
        scores = []
        for half in range(2):
            col = pl.multiple_of((2 * hd + half) * LANES, LANES)
            scores.append(_dot_nt(keys_ref[half], q_ref[:, pl.ds(col, LANES)]))
        (v1, i1), (v2, i2) = _top16(*scores)
        best, pos = _top16(_pruned_candidates(v1, v2))
        a_sel, b_sel = _candidate_ranks(pos)
        expert = _select_rows(i1, a_sel) * N_KEYS + _select_rows(i2, b_sel)
        e = jnp.exp(best - jnp.max(best, axis=0, keepdims=True))
        rows = pl.ds(pl.multiple_of(hd * PEER_TOPK, PEER_TOPK), PEER_TOPK)
        idx_all[rows, :] = expert.astype(jnp.int32)
        gate_all[rows, :] = e / jnp.sum(e, axis=0, keepdims=True)
        return 0

    lax.fori_loop(0, PEER_HEADS, head_body, 0)

    idx = idx_all[...]
    gate = gate_all[...]
    key1 = idx >> KEY_BITS
    key2 = idx & (N_KEYS - 1)
    per_bits = KEY_BITS - TILE_BITS
    tid = (key1 + (key2 >> per_bits)) & (N_EXPERT_TILES - 1)
    local = ((key1 << per_bits) + (key2 & ((1 << per_bits) - 1))) * rows_per_expert
    n = PEER_SLOTS
    before = (lax.broadcasted_iota(jnp.int32, (n, n), 1) < lax.broadcasted_iota(jnp.int32, (n, n), 0)).astype(F32)
    for k in range(N_EXPERT_TILES):
        mem = tid == k
        shift = jnp.dot(before, jnp.where(mem, 0.0, 1.0), preferred_element_type=F32).astype(jnp.int32)
        xi = jnp.where(mem, local, 0)
        xg = jnp.where(mem, gate, 0.0)
        xd = jnp.where(mem, shift, 0)
        for b in range(int(math.log2(n))):
            step = 1 << b
            mv = ((xd >> b) & 1) == 1
            pull = lambda a: pltpu.roll(a, n - step, axis=0)
            inc = pull(mv.astype(jnp.int32)) == 1
            xi = jnp.where(inc, pull(xi), jnp.where(mv, 0, xi))
            xg = jnp.where(inc, pull(xg), jnp.where(mv, 0.0, xg))
            xd = jnp.where(inc, pull(xd), jnp.where(mv, 0, xd))
        idx_ref[k] = xi.T
        gate_ref[k * n:(k + 1) * n, :] = xg
        cnt_ref[k:k + 1, :] = jnp.sum(mem.astype(jnp.int32), axis=0, keepdims=True)


def peer_route(q, sub_keys, rows_per_expert):
    t, d = q.shape
    tt = ROUTE_TOKENS
    n = N_EXPERT_TILES * PEER_SLOTS
    return pl.pallas_call(
        functools.partial(_route_kernel, rows_per_expert=rows_per_expert),
        grid=(t // tt,),
        in_specs=[pl.BlockSpec((tt, d), lambda i: (i, 0)),
                  pl.BlockSpec(sub_keys.shape, lambda i: (0, 0, 0))],
        out_specs=[pl.BlockSpec((N_EXPERT_TILES, tt, PEER_SLOTS), lambda i: (0, i, 0)),
                   pl.BlockSpec((n, tt), lambda i: (0, i)),
                   pl.BlockSpec((N_EXPERT_TILES, tt), lambda i: (0, i))],
        out_shape=[jax.ShapeDtypeStruct((N_EXPERT_TILES, t, PEER_SLOTS), jnp.int32),
                   jax.ShapeDtypeStruct((n, t), F32),
                   jax.ShapeDtypeStruct((N_EXPERT_TILES, t), jnp.int32)],
        scratch_shapes=[pltpu.VMEM((PEER_SLOTS, tt), jnp.int32), pltpu.VMEM((PEER_SLOTS, tt), F32)],
        compiler_params=_cparams("parallel"),
        name="peer_route",
    )(q, sub_keys)


EXPERT_TOKENS = 128


def _n_groups(cnt):
    return (cnt + (SLOT_GROUP - 1)) // SLOT_GROUP


STATIC_SLOTS = 40
STATIC_GROUPS = STATIC_SLOTS // SLOT_GROUP
FOLD_ORDER = (0, 4, 2, 6, 1, 5, 3, 7)
TOKEN_UNROLL = 2


def _rows_of(i, n):
    return pl.ds(pl.multiple_of(i * n, n), n)


def _fold_group(parts):
    sub = lax.broadcasted_iota(jnp.int32, parts[0].shape, 0)
    xs = [parts[i] for i in FOLD_ORDER]
    half = SUBLANES // 2
    while half >= 1:
        keep = (sub & half) == 0
        xs = [jnp.where(keep, x, pltpu.roll(y, half, axis=0)) + jnp.where(keep, pltpu.roll(x, SUBLANES - half, axis=0), y)
              for x, y in zip(xs[0::2], xs[1::2])]
        half //= 2
    return xs[0]


def _down_kernel(idx_ref, cnt_ref, h_ref, tbl_ref, act_ref, part_scr):
    tb = act_ref.shape[1]
    rpe = h_ref.shape[0] // tb
    lane = lax.broadcasted_iota(jnp.int32, (SUBLANES, tb), 1)

    def partial(off, hv):
        p = tbl_ref[pl.ds(pl.multiple_of(off, rpe), rpe), :] * hv
        out = p[:SUBLANES]
        for c in range(1, rpe // SUBLANES):
            out = out + p[c * SUBLANES:(c + 1) * SUBLANES]
        return out

    def group(t, g, hv):
        return _fold_group([partial(idx_ref[t, g * SLOT_GROUP + u], hv) for u in range(SLOT_GROUP)])

    act_ref[STATIC_SLOTS:, :] = jnp.zeros((PEER_SLOTS - STATIC_SLOTS, tb), F32)

    def overflow(t, hv):
        n = cnt_ref[0, 0, t]

        @pl.when(n > STATIC_SLOTS)
        def _():
            def group_body(g, _):
                rows = _rows_of(g, SLOT_GROUP)
                col = jnp.sum(group(t, g, hv), axis=-1, keepdims=True)
                act_ref[rows, :] = jnp.where(lane == t, col, act_ref[rows, :])
                return 0

            lax.fori_loop(STATIC_GROUPS, _n_groups(n), group_body, 0)

    def tok_body(i, _):
        tokens = [i * TOKEN_UNROLL + r for r in range(TOKEN_UNROLL)]
        hvs = [h_ref[_rows_of(t, rpe), :] for t in tokens]
        for t, hv in zip(tokens, hvs):
            for g in range(STATIC_GROUPS):
                part_scr[_rows_of(t * STATIC_GROUPS + g, SUBLANES), :] = group(t, g, hv)
        for t, hv in zip(tokens, hvs):
            overflow(t, hv)
        return 0

    lax.fori_loop(0, tb // TOKEN_UNROLL, tok_body, 0)

    for g in range(STATIC_GROUPS):
        acc = jnp.zeros((SUBLANES, tb), F32)
        for t in range(tb):
            c = part_scr[(t * STATIC_GROUPS + g) * SUBLANES:(t * STATIC_GROUPS + g + 1) * SUBLANES, :]
            acc = jnp.where(lane == t, jnp.sum(c, axis=-1, keepdims=True), acc)
        act_ref[g * SLOT_GROUP:(g + 1) * SLOT_GROUP, :] = acc


def _smem_spec(block, index_map):
    return pl.BlockSpec(block, index_map, memory_space=pltpu.SMEM)


def _tile_spec(rpe, index_map):
    return pl.BlockSpec((EXPERT_TILE * rpe, LANES), index_map, pipeline_mode=pl.Buffered(1))


def peer_down_acts(idx, cnt, h2, table2):
    rpe = table2.shape[0] // (N_KEYS * N_KEYS)
    t = h2.shape[0] // rpe
    tb = EXPERT_TOKENS
    return pl.pallas_call(
        _down_kernel,
        grid=(N_EXPERT_TILES, t // tb),
        in_specs=[_smem_spec((tb, PEER_SLOTS), lambda k, i: (k * (t // tb) + i, 0)),
                  _smem_spec((1, 1, tb), lambda k, i: (k, 0, i)),
                  pl.BlockSpec((tb * rpe, LANES), lambda k, i: (i, 0)),
                  _tile_spec(rpe, lambda k, i: (k, 0))],
        out_specs=pl.BlockSpec((PEER_SLOTS, tb), lambda k, i: (k, i)),
        out_shape=jax.ShapeDtypeStruct((N_EXPERT_TILES * PEER_SLOTS, t), F32),
        scratch_shapes=[pltpu.VMEM((tb * STATIC_GROUPS * SUBLANES, LANES), F32)],
        compiler_params=_cparams("arbitrary", "arbitrary"),
        name="peer_down",
    )(idx, cnt, h2, table2)


def _coef_kernel(act_ref, gate_ref, o_ref):
    a = act_ref[...]
    coef = gate_ref[...] * (0.5 * a * (1.0 + lax.erf(a * (2.0 ** -0.5))))
    o_ref[...] = coef.T


def peer_coef(act, gate):
    n, t = gate.shape
    tb = 512 if t % 512 == 0 else EXPERT_TOKENS
    per_tile = t // tb
    spec = pl.BlockSpec((PEER_SLOTS, tb), lambda k, i: (k, i))
    return pl.pallas_call(
        _coef_kernel,
        grid=(N_EXPERT_TILES, per_tile),
        in_specs=[spec, spec],
        out_specs=pl.BlockSpec((tb, PEER_SLOTS), lambda k, i: (k * per_tile + i, 0)),
        out_shape=jax.ShapeDtypeStruct((N_EXPERT_TILES * t, PEER_SLOTS), F32),
        compiler_params=_cparams("parallel", "parallel"),
        name="peer_coef",
    )(act, gate)


def _up_kernel(idx_ref, coef_ref, cnt_ref, x_ref, gt_ref, tbl_ref, o_ref):
    tb = cnt_ref.shape[-1]
    rpe = x_ref.shape[0] // tb
    gt = gt_ref[...]

    def term(t, j):
        return coef_ref[t, j] * tbl_ref[pl.ds(pl.multiple_of(idx_ref[t, j], rpe), rpe), :]

    def token_body(t, _):
        accs = [jnp.zeros((rpe, LANES), F32), jnp.zeros((rpe, LANES), F32)]
        for j in range(STATIC_SLOTS):
            accs[j % 2] = accs[j % 2] + term(t, j)

        def group_body(g, acc):
            for u in range(SLOT_GROUP):
                acc = acc + term(t, g * SLOT_GROUP + u)
            return acc

        n_groups = jnp.maximum(_n_groups(cnt_ref[0, 0, t]), STATIC_GROUPS)
        acc = lax.fori_loop(STATIC_GROUPS, n_groups, group_body, accs[0] + accs[1])
        rows = _rows_of(t, rpe)
        o_ref[rows, :] = x_ref[rows, :] + gt * acc
        return 0

    lax.fori_loop(0, tb, token_body, 0)


def peer_up_tile(k, idx, coef, cnt, x2r, gate2r, table2, seq):
    rpe = table2.shape[0] // (N_KEYS * N_KEYS)
    t = x2r.shape[0] // rpe
    tb = min(EXPERT_TOKENS, seq)
    per_seq = seq // tb
    n_blk = t // tb
    return pl.pallas_call(
        _up_kernel,
        grid=(n_blk,),
        in_specs=[_smem_spec((tb, PEER_SLOTS), lambda i: (k * n_blk + i, 0)),
                  _smem_spec((tb, PEER_SLOTS), lambda i: (k * n_blk + i, 0)),
                  _smem_spec((1, 1, tb), lambda i: (k, 0, i)),
                  pl.BlockSpec((tb * rpe, LANES), lambda i: (i, 0)),
                  pl.BlockSpec((rpe, LANES), lambda i: (i // per_seq, 0)),
                  _tile_spec(rpe, lambda i: (k, 0))],
        out_specs=pl.BlockSpec((tb * rpe, LANES), lambda i: (i, 0)),
        out_shape=jax.ShapeDtypeStruct(x2r.shape, F32),
        compiler_params=_cparams("arbitrary"),
        name="peer_up",
    )(idx, coef, cnt, x2r, gate2r, table2)


def _tile_tables_kernel(x_ref, o_ref):
    nt = N_EXPERT_TILES
    per = N_KEYS // nt
    rpe = x_ref.shape[1] // LANES
    key1 = pl.program_id(1)
    for r in range(nt):
        tile = (key1 + r) % nt
        for c in range(rpe):
            o_ref[tile, pl.ds(c, per, stride=rpe), :] = x_ref[r * per:(r + 1) * per, c * LANES:(c + 1) * LANES]


def tile_tables(tables):
    depth, e, d = tables.shape
    nt = N_EXPERT_TILES
    per = N_KEYS // nt
    rpe = d // LANES
    out = pl.pallas_call(
        _tile_tables_kernel,
        grid=(depth, N_KEYS),
        in_specs=[pl.BlockSpec((None, N_KEYS, d), lambda l, k1: (l, k1, 0))],
        out_specs=pl.BlockSpec((None, nt, None, per * rpe, LANES), lambda l, k1: (l, 0, k1, 0, 0)),
        out_shape=jax.ShapeDtypeStruct((depth, nt, N_KEYS, per * rpe, LANES), tables.dtype),
        compiler_params=_cparams("parallel", "parallel"),
        name="tile_tables",
    )(tables)
    return out.reshape(depth, e * rpe, LANES)


def peer_layer(x2, mod, p, layer, batch, seq):
    _, _, _, sh_f, sc_f, gt_f = _split_mod(mod)
    t, d = x2.shape
    rpe = d // LANES
    q, h = norm_matmul(x2, p["norm_ffn"][layer], sc_f, sh_f, p["peer_w_query"][layer].astype(BF16), seq, emit_h=True)
    idx, gate, cnt = peer_route(q, p["peer_sub_keys"][layer], rpe)
    idx = idx.reshape(N_EXPERT_TILES * t, PEER_SLOTS)
    cnt = cnt.reshape(N_EXPERT_TILES, 1, t)
    act = peer_down_acts(idx, cnt, h.reshape(t * rpe, LANES), p["peer_down_rows"][layer])
    coef = peer_coef(act, gate)
    up2 = p["peer_up_rows"][layer]
    xr = x2.reshape(t * rpe, LANES)
    gtr = gt_f.reshape(batch * rpe, LANES)
    for k in range(N_EXPERT_TILES):
        xr = peer_up_tile(k, idx, coef, cnt, xr, gtr, up2, seq)
    return xr.reshape(t, d)


def _final_norm_kernel(x_ref, g_ref, o_ref):
    x = x_ref[...]
    o_ref[...] = x * lax.rsqrt(jnp.mean(x * x, axis=-1, keepdims=True) + EPS) * g_ref[...]


def final_rmsnorm(x2, gain, *, tm=512):
    t, d = x2.shape
    return pl.pallas_call(
        _final_norm_kernel,
        grid=(t // tm,),
        in_specs=[pl.BlockSpec((tm, d), lambda i: (i, 0)), pl.BlockSpec((1, d), lambda i: (0, 0))],
        out_specs=pl.BlockSpec((tm, d), lambda i: (i, 0)),
        out_shape=jax.ShapeDtypeStruct((t, d), F32),
        compiler_params=_cparams("parallel"),
        name="final_norm",
    )(x2, gain.reshape(1, d))


def kernel(x, c, norm_mix, norm_ffn, ada_w, ada_b, even_w_in, even_w_out, ret_norm, diff_lambda, diff_norm, odd_w_in, odd_w_out, gdn_conv, gdn_a_log, gdn_dt_bias, gdn_norm, mla_q_norm, mla_w_uq, mla_kv_norm, mla_w_ukv, peer_w_query, peer_sub_keys, peer_down, peer_up, final_norm):
    p = dict(norm_mix=norm_mix, norm_ffn=norm_ffn, even_w_in=even_w_in, even_w_out=even_w_out, ret_norm=ret_norm,
             diff_lambda=diff_lambda, diff_norm=diff_norm, odd_w_in=odd_w_in, odd_w_out=odd_w_out, gdn_conv=gdn_conv,
             gdn_a_log=gdn_a_log, gdn_dt_bias=gdn_dt_bias, gdn_norm=gdn_norm, mla_q_norm=mla_q_norm,
             mla_w_uq=mla_w_uq, mla_kv_norm=mla_kv_norm, mla_w_ukv=mla_w_ukv, peer_w_query=peer_w_query,
             peer_sub_keys=peer_sub_keys, peer_down_rows=tile_tables(peer_down), peer_up_rows=tile_tables(peer_up))
    batch, seq, d = x.shape
    depth = ada_w.shape[0]
    x2 = x.reshape(batch * seq, d)
    mod = modulation(c, ada_w, ada_b)
    for layer in range(depth):
        if layer % 2 == 0:
            x2 = even_layer_mixer(x2, mod[layer], p, layer, layer // 2, batch, seq)
        else:
            x2 = odd_layer_mixer(x2, mod[layer], p, layer, layer // 2, batch, seq)
        x2 = peer_layer(x2, mod[layer], p, layer, batch, seq)
    return final_rmsnorm(x2, final_norm).reshape(batch, seq, d)
```

```python
import functools
import math

import jax
import jax.numpy as jnp
import numpy as np
from jax import lax
from jax.experimental import pallas as pl
from jax.experimental.pallas import tpu as pltpu

F32 = jnp.float32
BF16 = jnp.bfloat16

CHUNK = 64
EPS = 1e-6
HEAD_DIM = 128
N_HEADS = 8
ROPE_THETA = 10000.0
MLA_Q_RANK = 384
MLA_KV_RANK = 256
MLA_ROPE = 64
N_KEYS = 128
PEER_HEADS = 8
PEER_TOPK = 16
PEER_SLOTS = PEER_HEADS * PEER_TOPK

LANES = 128
SUBLANES = 8
VMEM_LIMIT_BYTES = 56 * 1024 * 1024

N_EXPERT_TILES = 4
EXPERT_TILE = (N_KEYS * N_KEYS) // N_EXPERT_TILES
KEY_BITS = int(math.log2(N_KEYS))
TILE_BITS = int(math.log2(N_EXPERT_TILES))
SLOT_GROUP = 8

NEG_INF = float("-inf")


def _cparams(*sem):
    return pltpu.CompilerParams(dimension_semantics=sem, vmem_limit_bytes=VMEM_LIMIT_BYTES)


def _mod_kernel(c_ref, w_ref, b_ref, o_ref):
    c = c_ref[...]
    cond = c * jax.nn.sigmoid(c)
    o_ref[0] = jnp.dot(cond, w_ref[0], preferred_element_type=F32) + b_ref[0]


def modulation(c, ada_w, ada_b):
    depth, d, n = ada_w.shape
    b = c.shape[0]
    tn = 1536
    return pl.pallas_call(
        _mod_kernel,
        grid=(depth, n // tn),
        in_specs=[pl.BlockSpec((b, d), lambda l, j: (0, 0)),
                  pl.BlockSpec((1, d, tn), lambda l, j: (l, 0, j)),
                  pl.BlockSpec((1, 1, tn), lambda l, j: (l, 0, j))],
        out_specs=pl.BlockSpec((1, b, tn), lambda l, j: (l, 0, j)),
        out_shape=jax.ShapeDtypeStruct((depth, b, n), F32),
        compiler_params=_cparams("parallel", "parallel"),
        name="modulation",
    )(c, ada_w, ada_b.reshape(depth, 1, n))


def _norm_matmul_kernel(x_ref, g_ref, sc_ref, sh_ref, w_ref, *rest, emit_h):
    if emit_h:
        y_ref, h_ref, h_scr = rest
    else:
        y_ref, h_scr = rest

    @pl.when(pl.program_id(1) == 0)
    def _():
        x = x_ref[...]
        xn = x * lax.rsqrt(jnp.mean(x * x, axis=-1, keepdims=True) + EPS)
        h = xn * g_ref[...] * (1.0 + sc_ref[0]) + sh_ref[0]
        h_scr[...] = h.astype(BF16)
        if emit_h:
            h_ref[...] = h

    y_ref[...] = jnp.dot(h_scr[...], w_ref[...], preferred_element_type=F32)


def norm_matmul(x2, gain, scale, shift, w_bf16, seq, *, emit_h=False, tm=1024, tn=512):
    t, d = x2.shape
    n = w_bf16.shape[1]
    tm = min(tm, seq)
    assert seq % tm == 0 and n % tn == 0
    per_seq = seq // tm
    bvec = lambda i, j: (i // per_seq, 0, 0)
    out_shape = [jax.ShapeDtypeStruct((t, n), F32)]
    out_specs = [pl.BlockSpec((tm, tn), lambda i, j: (i, j))]
    if emit_h:
        out_shape.append(jax.ShapeDtypeStruct((t, d), F32))
        out_specs.append(pl.BlockSpec((tm, d), lambda i, j: (i, 0)))
    res = pl.pallas_call(
        functools.partial(_norm_matmul_kernel, emit_h=emit_h),
        grid=(t // tm, n // tn),
        in_specs=[pl.BlockSpec((tm, d), lambda i, j: (i, 0)),
                  pl.BlockSpec((1, d), lambda i, j: (0, 0)),
                  pl.BlockSpec((1, 1, d), bvec),
                  pl.BlockSpec((1, 1, d), bvec),
                  pl.BlockSpec((d, tn), lambda i, j: (0, j))],
        out_specs=out_specs,
        out_shape=out_shape,
        scratch_shapes=[pltpu.VMEM((tm, d), BF16)],
        compiler_params=_cparams("parallel", "arbitrary"),
        name="norm_matmul",
    )(x2, gain.reshape(1, d), scale[:, None, :], shift[:, None, :], w_bf16)
    return res if emit_h else res[0]


def _out_proj_kernel(a1_ref, a2_ref, w1_ref, w2_ref, x_ref, gt_ref, o_ref):
    y = jnp.dot(a1_ref[...], w1_ref[...], preferred_element_type=F32)
    y = y + jnp.dot(a2_ref[...], w2_ref[...], preferred_element_type=F32)
    o_ref[...] = x_ref[...] + gt_ref[0] * y


def out_proj_residual(a1, a2, w_bf16, x2, gate, seq, *, tm=1024, tn=512):
    t, k1 = a1.shape
    k2 = a2.shape[1]
    d = x2.shape[1]
    tm = min(tm, seq)
    per_seq = seq // tm
    return pl.pallas_call(
        _out_proj_kernel,
        grid=(t // tm, d // tn),
        in_specs=[pl.BlockSpec((tm, k1), lambda i, j: (i, 0)),
                  pl.BlockSpec((tm, k2), lambda i, j: (i, 0)),
                  pl.BlockSpec((k1, tn), lambda i, j: (0, j)),
                  pl.BlockSpec((k2, tn), lambda i, j: (0, j)),
                  pl.BlockSpec((tm, tn), lambda i, j: (i, j)),
                  pl.BlockSpec((1, 1, tn), lambda i, j: (i // per_seq, 0, j))],
        out_specs=pl.BlockSpec((tm, tn), lambda i, j: (i, j)),
        out_shape=jax.ShapeDtypeStruct((t, d), F32),
        compiler_params=_cparams("parallel", "parallel"),
        name="out_proj_residual",
    )(a1, a2, w_bf16[:k1], w_bf16[k1:], x2, gate[:, None, :])


def _dot_nt(a, b):
    return lax.dot_general(a, b, (((1,), (1,)), ((), ())), preferred_element_type=F32)


def _dot_tn(a, b):
    return lax.dot_general(a, b, (((0,), (0,)), ((), ())), preferred_element_type=F32)


def _retention_kernel(q_ref, k_ref, v_ref, g_ref, intra_ref, qdec_ref, kdec_ref, cdec_ref, norm_ref,
                      o_ref, state):
    @pl.when(pl.program_id(2) == 0)
    def _():
        state[...] = jnp.zeros_like(state)

    n_chunks = q_ref.shape[0] // CHUNK
    intra = intra_ref[0]
    qdec = qdec_ref[0]
    kdec = kdec_ref[0]
    cdec = cdec_ref[0]
    chunk_rows = [pl.ds(c * CHUNK, CHUNK) for c in range(n_chunks)]
    qs = [q_ref[rows, :] for rows in chunk_rows]
    ks = [k_ref[rows, :] * (HEAD_DIM ** -0.5) for rows in chunk_rows]
    vs = [v_ref[rows, :] for rows in chunk_rows]
    scores = [_dot_nt(qc, kc) * intra for qc, kc in zip(qs, ks)]
    updates = [_dot_tn(kc * kdec, vc) for kc, vc in zip(ks, vs)]
    states = [state[...]]
    for upd in updates:
        states.append(states[-1] * cdec + upd)
    state[...] = states[-1]
    for rows, qc, vc, s, st in zip(chunk_rows, qs, vs, scores, states):
        o = jnp.dot(s, vc, preferred_element_type=F32) + jnp.dot(qc * qdec, st, preferred_element_type=F32)
        on = o * lax.rsqrt(jnp.mean(o * o, axis=-1, keepdims=True) + EPS) * norm_ref[...]
        g = g_ref[rows, :]
        o_ref[rows, :] = (on * (g * jax.nn.sigmoid(g))).astype(o_ref.dtype)


def retention_mixer(proj, ret_norm, batch, seq, *, ts=512):
    t = proj.shape[0]
    ts = min(ts, seq)
    per_seq = seq // ts
    h8 = N_HEADS
    pos = jnp.arange(CHUNK, dtype=F32)
    lg = jnp.log1p(-jnp.exp2(-5.0 - jnp.arange(h8, dtype=F32)))[:, None]
    intra = jnp.exp(lg[..., None] * jnp.abs(pos[:, None] - pos[None, :]))
    qdec = jnp.broadcast_to(jnp.exp(lg * (pos + 1.0))[..., None], (h8, CHUNK, HEAD_DIM))
    kdec = jnp.broadcast_to(jnp.exp(lg * (CHUNK - 1.0 - pos))[..., None], (h8, CHUNK, HEAD_DIM))
    cdec = jnp.broadcast_to(jnp.exp(lg * CHUNK)[..., None], (h8, HEAD_DIM, HEAD_DIM))

    def col(off):
        return pl.BlockSpec((ts, HEAD_DIM), lambda b, h, s, off=off: (b * per_seq + s, off + h))

    hspec = lambda shape: pl.BlockSpec((1,) + shape, lambda b, h, s: (h, 0, 0))
    return pl.pallas_call(
        _retention_kernel,
        grid=(batch, h8, per_seq),
        in_specs=[col(0), col(h8), col(2 * h8), col(3 * h8),
                  hspec((CHUNK, CHUNK)), hspec((CHUNK, HEAD_DIM)), hspec((CHUNK, HEAD_DIM)),
                  hspec((HEAD_DIM, HEAD_DIM)),
                  pl.BlockSpec((1, HEAD_DIM), lambda b, h, s: (0, 0))],
        out_specs=pl.BlockSpec((ts, HEAD_DIM), lambda b, h, s: (b * per_seq + s, h)),
        out_shape=jax.ShapeDtypeStruct((t, h8 * HEAD_DIM), BF16),
        scratch_shapes=[pltpu.VMEM((HEAD_DIM, HEAD_DIM), F32)],
        compiler_params=_cparams("parallel", "parallel", "arbitrary"),
        name="retention",
    )(proj, proj, proj, proj, intra, qdec, kdec, cdec, ret_norm.reshape(1, HEAD_DIM))


ATTN_TQ = 256


def _diag_mask(tq):
    kc = lax.broadcasted_iota(jnp.int32, (tq, tq), 0) // CHUNK
    qc = lax.broadcasted_iota(jnp.int32, (tq, tq), 1) // CHUNK
    return kc <= qc


def _softmax_t(s_t, scale, mask):
    tq = mask.shape[0]
    ke = s_t.shape[0]
    y = s_t * (scale * math.log2(math.e))
    diag = jnp.where(mask, y[ke - tq:], NEG_INF)
    y = diag if ke == tq else jnp.concatenate([y[:ke - tq], diag], axis=0)
    e = jnp.exp2(y - jnp.max(y, axis=0, keepdims=True))
    return e, 1.0 / jnp.sum(e, axis=0, keepdims=True)


def _diff_attn_kernel(lam_ref, q_ref, k_ref, v_ref, norm_ref, o_ref, *, tq, lambda_init):
    seq = q_ref.shape[0]
    k = k_ref[...]
    half = lax.broadcasted_iota(jnp.int32, k.shape, 1) < (HEAD_DIM // 2)
    k_maps = (jnp.where(half, k, 0.0).astype(BF16), jnp.where(half, 0.0, k).astype(BF16))
    v_t = v_ref[...].T.astype(BF16)
    scale = (HEAD_DIM // 2) ** -0.5
    dl = lam_ref[...]
    lam = (jnp.exp(jnp.sum(dl[0:1] * dl[1:2], axis=-1, keepdims=True))
           - jnp.exp(jnp.sum(dl[2:3] * dl[3:4], axis=-1, keepdims=True)) + lambda_init)
    mask = _diag_mask(tq)
    blocks = [slice(qb * tq, (qb + 1) * tq) for qb in range(seq // tq)]
    qs = [q_ref[rows, :].astype(BF16) for rows in blocks]
    scores = [[_dot_nt(km[:rows.stop], q) for km in k_maps] for rows, q in zip(blocks, qs)]
    probs = [[_softmax_t(s_t, scale, mask) for s_t in pair] for pair in scores]
    for rows, ((e0, r0), (e1, r1)) in zip(blocks, probs):
        w_t = (e0 * r0 - e1 * (lam * r1)).astype(BF16)
        o = jnp.dot(v_t[:, :rows.stop], w_t, preferred_element_type=F32).T
        on = o * lax.rsqrt(jnp.mean(o * o, axis=-1, keepdims=True) + EPS) * norm_ref[...]
        o_ref[rows, :] = (on * (1.0 - lambda_init)).astype(o_ref.dtype)


def diff_attention_mixer(proj, diff_lambda, diff_norm, lambda_init, batch, seq):
    t = proj.shape[0]
    col = lambda off: pl.BlockSpec((seq, HEAD_DIM), lambda b, h, off=off: (b, off + h))
    return pl.pallas_call(
        functools.partial(_diff_attn_kernel, tq=min(ATTN_TQ, seq), lambda_init=lambda_init),
        grid=(batch, N_HEADS),
        in_specs=[pl.BlockSpec(diff_lambda.shape, lambda b, h: (0, 0)),
                  col(4 * N_HEADS), col(5 * N_HEADS), col(6 * N_HEADS),
                  pl.BlockSpec((1, HEAD_DIM), lambda b, h: (0, 0))],
        out_specs=pl.BlockSpec((seq, HEAD_DIM), lambda b, h: (b, h)),
        out_shape=jax.ShapeDtypeStruct((t, N_HEADS * HEAD_DIM), BF16),
        compiler_params=_cparams("parallel", "parallel"),
        name="diff_attention",
    )(diff_lambda, proj, proj, proj, diff_norm.reshape(1, HEAD_DIM))


def _split_mod(mod):
    return jnp.split(mod, 6, axis=-1)


def even_layer_mixer(x2, mod, p, layer, j, batch, seq):
    sh_m, sc_m, gt_m, _, _, _ = _split_mod(mod)
    proj = norm_matmul(x2, p["norm_mix"][layer], sc_m, sh_m, p["even_w_in"][j].astype(BF16), seq)
    ret = retention_mixer(proj, p["ret_norm"][j], batch, seq)
    lambda_init = 0.8 - 0.6 * math.exp(-0.3 * layer)
    dif = diff_attention_mixer(proj, p["diff_lambda"][j], p["diff_norm"][j], lambda_init, batch, seq)
    return out_proj_residual(ret, dif, p["even_w_out"][j].astype(BF16), x2, gt_m, seq)


ODD_COLS = 40 * LANES
GATE_BLOCK = 4 * N_HEADS


def _odd_w_in_layout(w):
    d = w.shape[0]
    hd = N_HEADS * HEAD_DIM
    zeros = lambda n: jnp.zeros((d, n), w.dtype)
    o = 4 * hd
    gates = w[:, o:o + 2 * N_HEADS]
    o += 2 * N_HEADS
    cq = w[:, o:o + MLA_Q_RANK]
    o += MLA_Q_RANK
    ckv = w[:, o:o + MLA_KV_RANK]
    o += MLA_KV_RANK
    kr = w[:, o:o + MLA_ROPE]
    return jnp.concatenate([w[:, :4 * hd], gates, zeros(LANES - 2 * N_HEADS), cq, ckv, kr,
                            zeros(LANES - MLA_ROPE), zeros(LANES)], axis=1)


def _silu(x):
    return x * jax.nn.sigmoid(x)


def _gdn_conv_kernel(x_ref, w_ref, o_ref):
    c = pl.program_id(1)
    x = x_ref[...]
    w = w_ref[...]
    width = w.shape[0]
    row = lax.broadcasted_iota(jnp.int32, x.shape, 0)
    y = x * w[width - 1:width]
    for sft in range(1, width):
        xs = jnp.where(row >= sft, pltpu.roll(x, sft, axis=0), 0.0)
        y = y + xs * w[width - 1 - sft:width - sft]
    y = _silu(y)
    yn = y * lax.rsqrt(jnp.sum(y * y, axis=-1, keepdims=True) + EPS)
    yn = yn * jnp.where(c < N_HEADS, HEAD_DIM ** -0.5, 1.0)
    o_ref[...] = jnp.where(c < 2 * N_HEADS, yn, y)


def gdn_conv(proj, conv_w, batch, seq):
    t = proj.shape[0]
    n_blk = 3 * N_HEADS
    return pl.pallas_call(
        _gdn_conv_kernel,
        grid=(batch, n_blk),
        in_specs=[pl.BlockSpec((seq, LANES), lambda b, c: (b, c)),
                  pl.BlockSpec((conv_w.shape[0], LANES), lambda b, c: (0, c))],
        out_specs=pl.BlockSpec((seq, LANES), lambda b, c: (b, c)),
        out_shape=jax.ShapeDtypeStruct((t, n_blk * LANES), F32),
        compiler_params=_cparams("parallel", "parallel"),
        name="gdn_conv",
    )(proj, conv_w)


def _dot_hi(a, b):
    def split(x):
        hi = x.astype(BF16)
        return hi, (x - hi.astype(F32)).astype(BF16)

    a_hi, a_lo = split(a)
    b_hi, b_lo = split(b)
    dot = functools.partial(jnp.dot, preferred_element_type=F32)
    return dot(a_hi, b_hi) + (dot(a_hi, b_lo) + dot(a_lo, b_hi))


def _unit_lower_inverses(l_mats):
    n = l_mats[0].shape[0]
    eye = (lax.broadcasted_iota(jnp.int32, (n, n), 0) == lax.broadcasted_iota(jnp.int32, (n, n), 1)).astype(F32)
    invs = [eye - l for l in l_mats]
    powers = [_dot_hi(l, l) for l in l_mats]
    span = 2
    while span < n:
        invs = [inv + _dot_hi(inv, p) for inv, p in zip(invs, powers)]
        span *= 2
        if span < n:
            powers = [_dot_hi(p, p) for p in powers]
    return invs


def _pick_lane(x, lane):
    sel = lax.broadcasted_iota(jnp.int32, x.shape, 1) == lane
    return jnp.sum(jnp.where(sel, x, 0.0), axis=-1, keepdims=True)


HEADS_PER_STEP = 2


def _gdn_kernel(q_ref, k_ref, v_ref, gg_ref, gate_ref, alog_ref, dtb_ref, norm_ref, o_ref, state):
    @pl.when(pl.program_id(2) == 0)
    def _():
        state[...] = jnp.zeros_like(state)

    n_chunks = q_ref.shape[0] // CHUNK
    ri = lax.broadcasted_iota(jnp.int32, (CHUNK, CHUNK), 0)
    ci = lax.broadcasted_iota(jnp.int32, (CHUNK, CHUNK), 1)
    tri = ci <= ri
    strict = ci < ri
    tri_f = tri.astype(F32)
    heads = [pl.program_id(1) * HEADS_PER_STEP + r for r in range(HEADS_PER_STEP)]
    a_coefs = [-jnp.exp(_pick_lane(alog_ref[...], h)) for h in heads]
    dt_biases = [_pick_lane(dtb_ref[...], h) for h in heads]
    units = [(pl.ds(c * CHUNK, CHUNK), slice(r * HEAD_DIM, (r + 1) * HEAD_DIM), r)
             for c in range(n_chunks) for r in range(HEADS_PER_STEP)]

    def local(rows, cols, r):
        k = k_ref[rows, cols]
        raw = gate_ref[rows, :]
        beta = jax.nn.sigmoid(_pick_lane(raw, heads[r]))
        ga = _pick_lane(raw, N_HEADS + heads[r]) + dt_biases[r]
        softplus = jnp.maximum(ga, 0.0) + jnp.log1p(jnp.exp(-jnp.abs(ga)))
        g = jnp.broadcast_to(a_coefs[r] * softplus, (CHUNK, HEAD_DIM))
        gc = _dot_hi(tri_f, g)
        g_rows = gc.T[:CHUNK, :]
        decay = jnp.where(tri, jnp.exp(jnp.where(tri, gc[:, :CHUNK] - g_rows, 0.0)), 0.0)
        k_beta = k * beta
        l_mat = jnp.where(strict, _dot_nt(k_beta, k) * decay, 0.0)
        return k, beta, gc, decay, k_beta, l_mat

    locs = [local(*unit) for unit in units]
    t_invs = _unit_lower_inverses([loc[-1] for loc in locs])
    preps = []
    for (rows, cols, _), (k, beta, gc, decay, k_beta, _), t_inv in zip(units, locs, t_invs):
        q = q_ref[rows, cols]
        eg = jnp.exp(gc)
        u = jnp.dot(t_inv, v_ref[rows, cols] * beta, preferred_element_type=F32)
        w = jnp.dot(t_inv, k_beta * eg, preferred_element_type=F32)
        g_last = gc[CHUNK - 1:CHUNK, :]
        preps.append((u, w, _dot_nt(q, k) * decay, q * eg, k * jnp.exp(g_last - gc), jnp.exp(g_last)))

    for (rows, cols, r), (u, w, intra, q_dec, k_dec, chunk_dec) in zip(units, preps):
        st = state[r]
        v_new = u - jnp.dot(w, st, preferred_element_type=F32)
        o = jnp.dot(q_dec, st, preferred_element_type=F32) + jnp.dot(intra, v_new, preferred_element_type=F32)
        state[r] = st * chunk_dec + _dot_tn(k_dec, v_new)
        on = o * lax.rsqrt(jnp.mean(o * o, axis=-1, keepdims=True) + EPS) * norm_ref[...]
        o_ref[rows, cols] = (on * _silu(gg_ref[rows, cols])).astype(o_ref.dtype)


def gdn_mixer(qkv, proj, a_log, dt_bias, gdn_norm, batch, seq, *, ts=512):
    t = qkv.shape[0]
    ts = min(ts, seq)
    per_seq = seq // ts
    hps = HEADS_PER_STEP
    pad = lambda v: jnp.pad(v.reshape(1, -1), ((0, 0), (0, LANES - v.shape[0])))

    def col(off):
        return pl.BlockSpec((ts, hps * HEAD_DIM), lambda b, h, s, off=off: (b * per_seq + s, off // hps + h))

    row1 = pl.BlockSpec((1, LANES), lambda b, h, s: (0, 0))
    return pl.pallas_call(
        _gdn_kernel,
        grid=(batch, N_HEADS // hps, per_seq),
        in_specs=[col(0), col(N_HEADS), col(2 * N_HEADS), col(3 * N_HEADS),
                  pl.BlockSpec((ts, LANES), lambda b, h, s: (b * per_seq + s, GATE_BLOCK)),
                  row1, row1, row1],
        out_specs=pl.BlockSpec((ts, hps * HEAD_DIM), lambda b, h, s: (b * per_seq + s, h)),
        out_shape=jax.ShapeDtypeStruct((t, N_HEADS * HEAD_DIM), BF16),
        scratch_shapes=[pltpu.VMEM((hps, HEAD_DIM, HEAD_DIM), F32)],
        compiler_params=_cparams("parallel", "parallel", "arbitrary"),
        name="gdn",
    )(qkv, qkv, qkv, proj, proj, pad(a_log), pad(dt_bias), gdn_norm.reshape(1, HEAD_DIM))


def _rope_tables(seq):
    half = MLA_ROPE // 2
    inv_freq = ROPE_THETA ** (-jnp.arange(half, dtype=F32) / half)
    ang = jnp.arange(seq, dtype=jnp.int32).astype(F32)[:, None] * inv_freq[None, :]
    cos, sin = jnp.cos(ang), jnp.sin(ang)
    z = jnp.zeros((seq, LANES - MLA_ROPE), F32)
    zh = jnp.zeros((seq, half), F32)
    return (jnp.concatenate([cos, cos, z], axis=1),
            jnp.concatenate([zh, sin, z], axis=1),
            jnp.concatenate([-sin, zh, z], axis=1))


def _rope(y, cos, sin_up, sin_dn):
    half = MLA_ROPE // 2
    return y * cos + pltpu.roll(y, half, axis=1) * sin_up + pltpu.roll(y, LANES - half, axis=1) * sin_dn


def _mla_proj_kernel(cq_ref, ckv_ref, kr_ref, qg_ref, kvg_ref, wqn_ref, wqr_ref, wkv_ref, cos_ref, sup_ref, sdn_ref,
                     qn_ref, qr_ref, kv_ref, kro_ref):
    def rms(x, g):
        return x * lax.rsqrt(jnp.mean(x * x, axis=-1, keepdims=True) + EPS) * g

    cos, sup, sdn = cos_ref[...], sup_ref[...], sdn_ref[...]
    cq = rms(cq_ref[...], qg_ref[...]).astype(BF16)
    qn_ref[...] = jnp.dot(cq, wqn_ref[...], preferred_element_type=F32).astype(qn_ref.dtype)
    qr = jnp.dot(cq, wqr_ref[...], preferred_element_type=F32)
    for h in range(N_HEADS):
        cols = slice(h * LANES, (h + 1) * LANES)
        qr_ref[:, cols] = _rope(qr[:, cols], cos, sup, sdn).astype(qr_ref.dtype)
    ckv = rms(ckv_ref[...], kvg_ref[...]).astype(BF16)
    kv_ref[...] = jnp.dot(ckv, wkv_ref[...], preferred_element_type=F32).astype(kv_ref.dtype)
    kro_ref[...] = _rope(kr_ref[...], cos, sup, sdn).astype(kro_ref.dtype)


def mla_projection(proj, q_norm, w_uq, kv_norm, w_ukv, seq, *, tm=512):
    t = proj.shape[0]
    tm = min(tm, seq)
    per_seq = seq // tm
    hd = N_HEADS * HEAD_DIM
    w3 = w_uq.reshape(MLA_Q_RANK, N_HEADS, HEAD_DIM + MLA_ROPE)
    wqn = w3[:, :, :HEAD_DIM].reshape(MLA_Q_RANK, hd).astype(BF16)
    wqr = jnp.pad(w3[:, :, HEAD_DIM:], ((0, 0), (0, 0), (0, LANES - MLA_ROPE))).reshape(MLA_Q_RANK, hd).astype(BF16)
    cos, sup, sdn = _rope_tables(seq)
    full = lambda a: pl.BlockSpec(a.shape, lambda i: (0,) * a.ndim)
    tab = pl.BlockSpec((tm, LANES), lambda i: (i % per_seq, 0))
    wkv = w_ukv.astype(BF16)
    qg, kvg = q_norm.reshape(1, -1), kv_norm.reshape(1, -1)
    return pl.pallas_call(
        _mla_proj_kernel,
        grid=(t // tm,),
        in_specs=[pl.BlockSpec((tm, MLA_Q_RANK), lambda i: (i, (GATE_BLOCK + 1) * LANES // MLA_Q_RANK)),
                  pl.BlockSpec((tm, MLA_KV_RANK), lambda i: (i, (GATE_BLOCK + 4) * LANES // MLA_KV_RANK)),
                  pl.BlockSpec((tm, LANES), lambda i: (i, GATE_BLOCK + 6)),
                  full(qg), full(kvg), full(wqn), full(wqr), full(wkv), tab, tab, tab],
        out_specs=[pl.BlockSpec((tm, hd), lambda i: (i, 0)), pl.BlockSpec((tm, hd), lambda i: (i, 0)),
                   pl.BlockSpec((tm, 2 * hd), lambda i: (i, 0)), pl.BlockSpec((tm, LANES), lambda i: (i, 0))],
        out_shape=[jax.ShapeDtypeStruct((t, hd), BF16), jax.ShapeDtypeStruct((t, hd), BF16),
                   jax.ShapeDtypeStruct((t, 2 * hd), BF16), jax.ShapeDtypeStruct((t, LANES), BF16)],
        compiler_params=_cparams("parallel"),
        name="mla_projection",
    )(proj, proj, proj, qg, kvg, wqn, wqr, wkv, cos, sup, sdn)


def _mla_attn_kernel(qn_ref, qr_ref, kn_ref, kr_ref, v_ref, o_ref, *, tq):
    seq = qn_ref.shape[0]
    kn = kn_ref[...]
    kr = kr_ref[...]
    v_t = v_ref[...].T
    scale = (HEAD_DIM + MLA_ROPE) ** -0.5
    mask = _diag_mask(tq)
    blocks = [slice(qb * tq, (qb + 1) * tq) for qb in range(seq // tq)]
    scores = [_dot_nt(kn[:rows.stop], qn_ref[rows, :]) + _dot_nt(kr[:rows.stop], qr_ref[rows, :]) for rows in blocks]
    probs = [_softmax_t(s_t, scale, mask) for s_t in scores]
    for rows, (e, r) in zip(blocks, probs):
        o_t = jnp.dot(v_t[:, :rows.stop], e.astype(BF16), preferred_element_type=F32) * r
        o_ref[rows, :] = o_t.T.astype(o_ref.dtype)


def mla_attention_mixer(qn, qr, kv, kr, batch, seq):
    t = qn.shape[0]
    qspec = pl.BlockSpec((seq, HEAD_DIM), lambda b, h: (b, h))
    return pl.pallas_call(
        functools.partial(_mla_attn_kernel, tq=min(ATTN_TQ, seq)),
        grid=(batch, N_HEADS),
        in_specs=[qspec, qspec,
                  pl.BlockSpec((seq, HEAD_DIM), lambda b, h: (b, 2 * h)),
                  pl.BlockSpec((seq, LANES), lambda b, h: (b, 0)),
                  pl.BlockSpec((seq, HEAD_DIM), lambda b, h: (b, 2 * h + 1))],
        out_specs=qspec,
        out_shape=jax.ShapeDtypeStruct((t, N_HEADS * HEAD_DIM), BF16),
        compiler_params=_cparams("parallel", "parallel"),
        name="mla_attention",
    )(qn, qr, kv, kr, kv)


def odd_layer_mixer(x2, mod, p, layer, j, batch, seq):
    sh_m, sc_m, gt_m, _, _, _ = _split_mod(mod)
    w_in = _odd_w_in_layout(p["odd_w_in"][j]).astype(BF16)
    proj = norm_matmul(x2, p["norm_mix"][layer], sc_m, sh_m, w_in, seq)
    qkv = gdn_conv(proj, p["gdn_conv"][j], batch, seq)
    gdn = gdn_mixer(qkv, proj, p["gdn_a_log"][j], p["gdn_dt_bias"][j], p["gdn_norm"][j], batch, seq)
    qn, qr, kv, kr = mla_projection(proj, p["mla_q_norm"][j], p["mla_w_uq"][j], p["mla_kv_norm"][j],
                                    p["mla_w_ukv"][j], seq)
    mla = mla_attention_mixer(qn, qr, kv, kr, batch, seq)
    return out_proj_residual(gdn, mla, p["odd_w_out"][j].astype(BF16), x2, gt_m, seq)


ROUTE_TOKENS = 128


def _top16(*problems):
    n, tt = problems[0].shape
    row = lax.broadcasted_iota(jnp.int32, (n, tt), 0).astype(F32)
    r16 = lax.broadcasted_iota(jnp.int32, (PEER_TOPK, tt), 0)

    def body(k, carry):
        hit = r16 == k
        out = []
        for s, vals, ids in carry:
            m = jnp.max(s, axis=0, keepdims=True)
            i = jnp.min(jnp.where(s == m, row, float(n)), axis=0, keepdims=True)
            out.append((jnp.where(row == i, NEG_INF, s), jnp.where(hit, m, vals), jnp.where(hit, i, ids)))
        return tuple(out)

    zeros = jnp.zeros((PEER_TOPK, tt), F32)
    res = lax.fori_loop(0, PEER_TOPK, body, tuple((s, zeros, zeros) for s in problems))
    res = [(vals, ids) for _, vals, ids in res]
    return res[0] if len(problems) == 1 else res


def _pruned_candidates(v1, v2):
    tt = v1.shape[1]
    sub = lax.broadcasted_iota(jnp.int32, (SUBLANES, tt), 0)
    row = lambda v, r: v[r:r + 1, :]
    blocks = [row(v1, 0) + v2[:SUBLANES], row(v1, 0) + v2[SUBLANES:]]
    blocks += [row(v1, a) + v2[:SUBLANES] for a in range(1, 5)]
    a567 = jnp.where(sub < 2, row(v1, 5), jnp.where(sub < 4, row(v1, 6), row(v1, 7)))
    b01 = jnp.where((sub & 1) == 0, row(v2, 0), row(v2, 1))
    blocks.append(jnp.where(sub < 6, a567 + b01, NEG_INF))
    blocks.append(v1[SUBLANES:] + row(v2, 0))
    return jnp.concatenate(blocks, axis=0)


def _candidate_ranks(pos):
    p16 = pos - 16.0
    p48 = pos - 48.0
    a_mid = 1.0 + jnp.floor(p16 * 0.125)
    a_hi = 5.0 + jnp.floor(p48 * 0.5)
    a = jnp.where(pos < 16.0, 0.0, jnp.where(pos < 48.0, a_mid, jnp.where(pos < 56.0, a_hi, p48)))
    b = jnp.where(pos < 16.0, pos, jnp.where(pos < 48.0, p16 - 8.0 * (a_mid - 1.0),
                                             jnp.where(pos < 56.0, p48 - 2.0 * (a_hi - 5.0), 0.0)))
    return a, b


def _select_rows(table, sel):
    out = jnp.zeros_like(sel)
    for a in range(table.shape[0]):
        out = out + jnp.where(sel == float(a), table[a:a + 1, :], 0.0)
    return out


def _route_kernel(q_ref, keys_ref, idx_ref, gate_ref, cnt_ref, idx_all, gate_all, *, rows_per_expert):
    keys = [keys_ref[half].astype(BF16) for half in range(2)]

    def head_body(hd, _):
        scores = []
        for half in range(2):
            col = pl.multiple_of((2 * hd + half) * LANES, LANES)
            scores.append(_dot_nt(keys[half], q_ref[:, pl.ds(col, LANES)].astype(BF16)))
        (v1, i1), (v2, i2) = _top16(*scores)
        best, pos = _top16(_pruned_candidates(v1, v2))
        a_sel, b_sel = _candidate_ranks(pos)
        expert = _select_rows(i1, a_sel) * N_KEYS + _select_rows(i2, b_sel)
        e = jnp.exp(best - jnp.max(best, axis=0, keepdims=True))
        rows = pl.ds(pl.multiple_of(hd * PEER_TOPK, PEER_TOPK), PEER_TOPK)
        idx_all[rows, :] = expert.astype(jnp.int32)
        gate_all[rows, :] = e / jnp.sum(e, axis=0, keepdims=True)
        return 0

    lax.fori_loop(0, PEER_HEADS, head_body, 0)

    idx = idx_all[...]
    gate = gate_all[...]
    key1 = idx >> KEY_BITS
    key2 = idx & (N_KEYS - 1)
    per_bits = KEY_BITS - TILE_BITS
    tid = (key1 + (key2 >> per_bits)) & (N_EXPERT_TILES - 1)
    local = ((key1 << per_bits) + (key2 & ((1 << per_bits) - 1))) * rows_per_expert
    n = PEER_SLOTS
    before = (lax.broadcasted_iota(jnp.int32, (n, n), 1) < lax.broadcasted_iota(jnp.int32, (n, n), 0)).astype(F32)
    for k in range(N_EXPERT_TILES):
        mem = tid == k
        shift = jnp.dot(before, jnp.where(mem, 0.0, 1.0), preferred_element_type=F32).astype(jnp.int32)
        xi = jnp.where(mem, local, 0)
        xg = jnp.where(mem, gate, 0.0)
        xd = jnp.where(mem, shift, 0)
        for b in range(int(math.log2(n))):
            step = 1 << b
            mv = ((xd >> b) & 1) == 1
            pull = lambda a: pltpu.roll(a, n - step, axis=0)
            inc = pull(mv.astype(jnp.int32)) == 1
            xi = jnp.where(inc, pull(xi), jnp.where(mv, 0, xi))
            xg = jnp.where(inc, pull(xg), jnp.where(mv, 0.0, xg))
            xd = jnp.where(inc, pull(xd), jnp.where(mv, 0, xd))
        idx_ref[k] = xi.T
        gate_ref[k * n:(k + 1) * n, :] = xg
        cnt_ref[k:k + 1, :] = jnp.sum(mem.astype(jnp.int32), axis=0, keepdims=True)


def peer_route(q, sub_keys, rows_per_expert):
    t, d = q.shape
    tt = ROUTE_TOKENS
    n = N_EXPERT_TILES * PEER_SLOTS
    return pl.pallas_call(
        functools.partial(_route_kernel, rows_per_expert=rows_per_expert),
        grid=(t // tt,),
        in_specs=[pl.BlockSpec((tt, d), lambda i: (i, 0)),
                  pl.BlockSpec(sub_keys.shape, lambda i: (0, 0, 0))],
        out_specs=[pl.BlockSpec((N_EXPERT_TILES, tt, PEER_SLOTS), lambda i: (0, i, 0)),
                   pl.BlockSpec((n, tt), lambda i: (0, i)),
                   pl.BlockSpec((N_EXPERT_TILES, tt), lambda i: (0, i))],
        out_shape=[jax.ShapeDtypeStruct((N_EXPERT_TILES, t, PEER_SLOTS), jnp.int32),
                   jax.ShapeDtypeStruct((n, t), F32),
                   jax.ShapeDtypeStruct((N_EXPERT_TILES, t), jnp.int32)],
        scratch_shapes=[pltpu.VMEM((PEER_SLOTS, tt), jnp.int32), pltpu.VMEM((PEER_SLOTS, tt), F32)],
        compiler_params=_cparams("parallel"),
        name="peer_route",
    )(q, sub_keys)


EXPERT_TOKENS = 128


def _n_groups(cnt):
    return (cnt + (SLOT_GROUP - 1)) // SLOT_GROUP


STATIC_SLOTS = 40
STATIC_GROUPS = STATIC_SLOTS // SLOT_GROUP
FOLD_ORDER = (0, 4, 2, 6, 1, 5, 3, 7)
TOKEN_UNROLL = 2


def _rows_of(i, n):
    return pl.ds(pl.multiple_of(i * n, n), n)


def _fold_group(parts):
    sub = lax.broadcasted_iota(jnp.int32, parts[0].shape, 0)
    xs = [parts[i] for i in FOLD_ORDER]
    half = SUBLANES // 2
    while half >= 1:
        keep = (sub & half) == 0
        xs = [jnp.where(keep, x, pltpu.roll(y, half, axis=0)) + jnp.where(keep, pltpu.roll(x, SUBLANES - half, axis=0), y)
              for x, y in zip(xs[0::2], xs[1::2])]
        half //= 2
    return xs[0]


def _down_kernel(idx_ref, cnt_ref, h_ref, tbl_ref, act_ref, part_scr):
    tb = act_ref.shape[1]
    rpe = h_ref.shape[0] // tb
    lane = lax.broadcasted_iota(jnp.int32, (SUBLANES, tb), 1)

    def partial(off, hv):
        p = tbl_ref[pl.ds(pl.multiple_of(off, rpe), rpe), :] * hv
        out = p[:SUBLANES]
        for c in range(1, rpe // SUBLANES):
            out = out + p[c * SUBLANES:(c + 1) * SUBLANES]
        return out

    def group(t, g, hv):
        return _fold_group([partial(idx_ref[t, g * SLOT_GROUP + u], hv) for u in range(SLOT_GROUP)])

    act_ref[STATIC_SLOTS:, :] = jnp.zeros((PEER_SLOTS - STATIC_SLOTS, tb), F32)

    def overflow(t, hv):
        n = cnt_ref[0, 0, t]

        @pl.when(n > STATIC_SLOTS)
        def _():
            def group_body(g, _):
                rows = _rows_of(g, SLOT_GROUP)
                col = jnp.sum(group(t, g, hv), axis=-1, keepdims=True)
                act_ref[rows, :] = jnp.where(lane == t, col, act_ref[rows, :])
                return 0

            lax.fori_loop(STATIC_GROUPS, _n_groups(n), group_body, 0)

    def tok_body(i, _):
        tokens = [i * TOKEN_UNROLL + r for r in range(TOKEN_UNROLL)]
        hvs = [h_ref[_rows_of(t, rpe), :] for t in tokens]
        for t, hv in zip(tokens, hvs):
            for g in range(STATIC_GROUPS):
                part_scr[_rows_of(t * STATIC_GROUPS + g, SUBLANES), :] = group(t, g, hv)
        for t, hv in zip(tokens, hvs):
            overflow(t, hv)
        return 0

    lax.fori_loop(0, tb // TOKEN_UNROLL, tok_body, 0)

    for g in range(STATIC_GROUPS):
        acc = jnp.zeros((SUBLANES, tb), F32)
        for t in range(tb):
            c = part_scr[(t * STATIC_GROUPS + g) * SUBLANES:(t * STATIC_GROUPS + g + 1) * SUBLANES, :]
            acc = jnp.where(lane == t, jnp.sum(c, axis=-1, keepdims=True), acc)
        act_ref[g * SLOT_GROUP:(g + 1) * SLOT_GROUP, :] = acc


def _smem_spec(block, index_map):
    return pl.BlockSpec(block, index_map, memory_space=pltpu.SMEM)


def _tile_spec(rpe, index_map):
    return pl.BlockSpec((EXPERT_TILE * rpe, LANES), index_map, pipeline_mode=pl.Buffered(1))


def peer_down_acts(idx, cnt, h2, table2):
    rpe = table2.shape[0] // (N_KEYS * N_KEYS)
    t = h2.shape[0] // rpe
    tb = EXPERT_TOKENS
    return pl.pallas_call(
        _down_kernel,
        grid=(N_EXPERT_TILES, t // tb),
        in_specs=[_smem_spec((tb, PEER_SLOTS), lambda k, i: (k * (t // tb) + i, 0)),
                  _smem_spec((1, 1, tb), lambda k, i: (k, 0, i)),
                  pl.BlockSpec((tb * rpe, LANES), lambda k, i: (i, 0)),
                  _tile_spec(rpe, lambda k, i: (k, 0))],
        out_specs=pl.BlockSpec((PEER_SLOTS, tb), lambda k, i: (k, i)),
        out_shape=jax.ShapeDtypeStruct((N_EXPERT_TILES * PEER_SLOTS, t), F32),
        scratch_shapes=[pltpu.VMEM((tb * STATIC_GROUPS * SUBLANES, LANES), F32)],
        compiler_params=_cparams("arbitrary", "arbitrary"),
        name="peer_down",
    )(idx, cnt, h2, table2)


def _coef_kernel(act_ref, gate_ref, o_ref):
    a = act_ref[...]
    coef = gate_ref[...] * (0.5 * a * (1.0 + lax.erf(a * (2.0 ** -0.5))))
    o_ref[...] = coef.T


def peer_coef(act, gate):
    n, t = gate.shape
    tb = 512 if t % 512 == 0 else EXPERT_TOKENS
    per_tile = t // tb
    spec = pl.BlockSpec((PEER_SLOTS, tb), lambda k, i: (k, i))
    return pl.pallas_call(
        _coef_kernel,
        grid=(N_EXPERT_TILES, per_tile),
        in_specs=[spec, spec],
        out_specs=pl.BlockSpec((tb, PEER_SLOTS), lambda k, i: (k * per_tile + i, 0)),
        out_shape=jax.ShapeDtypeStruct((N_EXPERT_TILES * t, PEER_SLOTS), F32),
        compiler_params=_cparams("parallel", "parallel"),
        name="peer_coef",
    )(act, gate)


def _up_kernel(idx_ref, coef_ref, cnt_ref, x_ref, gt_ref, tbl_ref, o_ref):
    tb = cnt_ref.shape[-1]
    rpe = x_ref.shape[0] // tb
    gt = gt_ref[...]

    def term(t, j):
        return coef_ref[t, j] * tbl_ref[pl.ds(pl.multiple_of(idx_ref[t, j], rpe), rpe), :]

    def token_body(t, _):
        accs = [jnp.zeros((rpe, LANES), F32), jnp.zeros((rpe, LANES), F32)]
        for j in range(STATIC_SLOTS):
            accs[j % 2] = accs[j % 2] + term(t, j)

        def group_body(g, acc):
            for u in range(SLOT_GROUP):
                acc = acc + term(t, g * SLOT_GROUP + u)
            return acc

        n_groups = jnp.maximum(_n_groups(cnt_ref[0, 0, t]), STATIC_GROUPS)
        acc = lax.fori_loop(STATIC_GROUPS, n_groups, group_body, accs[0] + accs[1])
        rows = _rows_of(t, rpe)
        o_ref[rows, :] = x_ref[rows, :] + gt * acc
        return 0

    lax.fori_loop(0, tb, token_body, 0)


def peer_up_tile(k, idx, coef, cnt, x2r, gate2r, table2, seq):
    rpe = table2.shape[0] // (N_KEYS * N_KEYS)
    t = x2r.shape[0] // rpe
    tb = min(EXPERT_TOKENS, seq)
    per_seq = seq // tb
    n_blk = t // tb
    return pl.pallas_call(
        _up_kernel,
        grid=(n_blk,),
        in_specs=[_smem_spec((tb, PEER_SLOTS), lambda i: (k * n_blk + i, 0)),
                  _smem_spec((tb, PEER_SLOTS), lambda i: (k * n_blk + i, 0)),
                  _smem_spec((1, 1, tb), lambda i: (k, 0, i)),
                  pl.BlockSpec((tb * rpe, LANES), lambda i: (i, 0)),
                  pl.BlockSpec((rpe, LANES), lambda i: (i // per_seq, 0)),
                  _tile_spec(rpe, lambda i: (k, 0))],
        out_specs=pl.BlockSpec((tb * rpe, LANES), lambda i: (i, 0)),
        out_shape=jax.ShapeDtypeStruct(x2r.shape, F32),
        compiler_params=_cparams("arbitrary"),
        name="peer_up",
    )(idx, coef, cnt, x2r, gate2r, table2)


def _tile_tables_kernel(x_ref, o_ref):
    nt = N_EXPERT_TILES
    per = N_KEYS // nt
    rpe = x_ref.shape[1] // LANES
    key1 = pl.program_id(1)
    for r in range(nt):
        tile = (key1 + r) % nt
        for c in range(rpe):
            o_ref[tile, pl.ds(c, per, stride=rpe), :] = x_ref[r * per:(r + 1) * per, c * LANES:(c + 1) * LANES]


def tile_tables(tables):
    depth, e, d = tables.shape
    nt = N_EXPERT_TILES
    per = N_KEYS // nt
    rpe = d // LANES
    out = pl.pallas_call(
        _tile_tables_kernel,
        grid=(depth, N_KEYS),
        in_specs=[pl.BlockSpec((None, N_KEYS, d), lambda l, k1: (l, k1, 0))],
        out_specs=pl.BlockSpec((None, nt, None, per * rpe, LANES), lambda l, k1: (l, 0, k1, 0, 0)),
        out_shape=jax.ShapeDtypeStruct((depth, nt, N_KEYS, per * rpe, LANES), tables.dtype),
        compiler_params=_cparams("parallel", "parallel"),
        name="tile_tables",
    )(tables)
    return out.reshape(depth, e * rpe, LANES)


def peer_layer(x2, mod, p, layer, batch, seq):
    _, _, _, sh_f, sc_f, gt_f = _split_mod(mod)
    t, d = x2.shape
    rpe = d // LANES
    q, h = norm_matmul(x2, p["norm_ffn"][layer], sc_f, sh_f, p["peer_w_query"][layer].astype(BF16), seq, emit_h=True)
    idx, gate, cnt = peer_route(q, p["peer_sub_keys"][layer], rpe)
    idx = idx.reshape(N_EXPERT_TILES * t, PEER_SLOTS)
    cnt = cnt.reshape(N_EXPERT_TILES, 1, t)
    act = peer_down_acts(idx, cnt, h.reshape(t * rpe, LANES), p["peer_down_rows"][layer])
    coef = peer_coef(act, gate)
    up2 = p["peer_up_rows"][layer]
    xr = x2.reshape(t * rpe, LANES)
    gtr = gt_f.reshape(batch * rpe, LANES)
    for k in range(N_EXPERT_TILES):
        xr = peer_up_tile(k, idx, coef, cnt, xr, gtr, up2, seq)
    return xr.reshape(t, d)


def _final_norm_kernel(x_ref, g_ref, o_ref):
    x = x_ref[...]
    o_ref[...] = x * lax.rsqrt(jnp.mean(x * x, axis=-1, keepdims=True) + EPS) * g_ref[...]


def final_rmsnorm(x2, gain, *, tm=512):
    t, d = x2.shape
    return pl.pallas_call(
        _final_norm_kernel,
        grid=(t // tm,),
        in_specs=[pl.BlockSpec((tm, d), lambda i: (i, 0)), pl.BlockSpec((1, d), lambda i: (0, 0))],
        out_specs=pl.BlockSpec((tm, d), lambda i: (i, 0)),
        out_shape=jax.ShapeDtypeStruct((t, d), F32),
        compiler_params=_cparams("parallel"),
        name="final_norm",
    )(x2, gain.reshape(1, d))


def kernel(x, c, norm_mix, norm_ffn, ada_w, ada_b, even_w_in, even_w_out, ret_norm, diff_lambda, diff_norm, odd_w_in, odd_w_out, gdn_conv, gdn_a_log, gdn_dt_bias, gdn_norm, mla_q_norm, mla_w_uq, mla_kv_norm, mla_w_ukv, peer_w_query, peer_sub_keys, peer_down, peer_up, final_norm):
    p = dict(norm_mix=norm_mix, norm_ffn=norm_ffn, even_w_in=even_w_in, even_w_out=even_w_out, ret_norm=ret_norm,
             diff_lambda=diff_lambda, diff_norm=diff_norm, odd_w_in=odd_w_in, odd_w_out=odd_w_out, gdn_conv=gdn_conv,
             gdn_a_log=gdn_a_log, gdn_dt_bias=gdn_dt_bias, gdn_norm=gdn_norm, mla_q_norm=mla_q_norm,
             mla_w_uq=mla_w_uq, mla_kv_norm=mla_kv_norm, mla_w_ukv=mla_w_ukv, peer_w_query=peer_w_query,
             peer_sub_keys=peer_sub_keys, peer_down_rows=tile_tables(peer_down), peer_up_rows=tile_tables(peer_up))
    batch, seq, d = x.shape
    depth = ada_w.shape[0]
    x2 = x.reshape(batch * seq, d)
    mod = modulation(c, ada_w, ada_b)
    for layer in range(depth):
        if layer % 2 == 0:
            x2 = even_layer_mixer(x2, mod[layer], p, layer, layer // 2, batch, seq)
        else:
            x2 = odd_layer_mixer(x2, mod[layer], p, layer, layer // 2, batch, seq)
        x2 = peer_layer(x2, mod[layer], p, layer, batch, seq)
    return final_rmsnorm(x2, final_norm).reshape(batch, seq, d)
```

```python
import functools
import math

import jax
import jax.numpy as jnp
import numpy as np
from jax import lax
from jax.experimental import pallas as pl
from jax.experimental.pallas import tpu as pltpu

F32 = jnp.float32
BF16 = jnp.bfloat16

CHUNK = 64
EPS = 1e-6
HEAD_DIM = 128
N_HEADS = 8
ROPE_THETA = 10000.0
MLA_Q_RANK = 384
MLA_KV_RANK = 256
MLA_ROPE = 64
N_KEYS = 128
PEER_HEADS = 8
PEER_TOPK = 16
PEER_SLOTS = PEER_HEADS * PEER_TOPK

LANES = 128
SUBLANES = 8
VMEM_LIMIT_BYTES = 56 * 1024 * 1024

N_EXPERT_TILES = 4
EXPERT_TILE = (N_KEYS * N_KEYS) // N_EXPERT_TILES
KEY_BITS = int(math.log2(N_KEYS))
TILE_BITS = int(math.log2(N_EXPERT_TILES))
SLOT_GROUP = 8

NEG_INF = float("-inf")


def _cparams(*sem):
    return pltpu.CompilerParams(dimension_semantics=sem, vmem_limit_bytes=VMEM_LIMIT_BYTES)


def _mod_kernel(c_ref, w_ref, b_ref, o_ref):
    c = c_ref[...]
    cond = c * jax.nn.sigmoid(c)
    o_ref[0] = jnp.dot(cond, w_ref[0], preferred_element_type=F32) + b_ref[0]


def modulation(c, ada_w, ada_b):
    depth, d, n = ada_w.shape
    b = c.shape[0]
    tn = 1536
    return pl.pallas_call(
        _mod_kernel,
        grid=(depth, n // tn),
        in_specs=[pl.BlockSpec((b, d), lambda l, j: (0, 0)),
                  pl.BlockSpec((1, d, tn), lambda l, j: (l, 0, j)),
                  pl.BlockSpec((1, 1, tn), lambda l, j: (l, 0, j))],
        out_specs=pl.BlockSpec((1, b, tn), lambda l, j: (l, 0, j)),
        out_shape=jax.ShapeDtypeStruct((depth, b, n), F32),
        compiler_params=_cparams("parallel", "parallel"),
        name="modulation",
    )(c, ada_w, ada_b.reshape(depth, 1, n))


def _norm_matmul_kernel(x_ref, g_ref, sc_ref, sh_ref, w_ref, *rest, emit_h):
    if emit_h:
        y_ref, h_ref, h_scr = rest
    else:
        y_ref, h_scr = rest

    @pl.when(pl.program_id(1) == 0)
    def _():
        x = x_ref[...]
        xn = x * lax.rsqrt(jnp.mean(x * x, axis=-1, keepdims=True) + EPS)
        h = xn * g_ref[...] * (1.0 + sc_ref[0]) + sh_ref[0]
        h_scr[...] = h.astype(BF16)
        if emit_h:
            h_ref[...] = h

    y_ref[...] = jnp.dot(h_scr[...], w_ref[...], preferred_element_type=F32)


def norm_matmul(x2, gain, scale, shift, w_bf16, seq, *, emit_h=False, tm=1024, tn=512):
    t, d = x2.shape
    n = w_bf16.shape[1]
    tm = min(tm, seq)
    assert seq % tm == 0 and n % tn == 0
    per_seq = seq // tm
    bvec = lambda i, j: (i // per_seq, 0, 0)
    out_shape = [jax.ShapeDtypeStruct((t, n), F32)]
    out_specs = [pl.BlockSpec((tm, tn), lambda i, j: (i, j))]
    if emit_h:
        out_shape.append(jax.ShapeDtypeStruct((t, d), F32))
        out_specs.append(pl.BlockSpec((tm, d), lambda i, j: (i, 0)))
    res = pl.pallas_call(
        functools.partial(_norm_matmul_kernel, emit_h=emit_h),
        grid=(t // tm, n // tn),
        in_specs=[pl.BlockSpec((tm, d), lambda i, j: (i, 0)),
                  pl.BlockSpec((1, d), lambda i, j: (0, 0)),
                  pl.BlockSpec((1, 1, d), bvec),
                  pl.BlockSpec((1, 1, d), bvec),
                  pl.BlockSpec((d, tn), lambda i, j: (0, j))],
        out_specs=out_specs,
        out_shape=out_shape,
        scratch_shapes=[pltpu.VMEM((tm, d), BF16)],
        compiler_params=_cparams("parallel", "arbitrary"),
        name="norm_matmul",
    )(x2, gain.reshape(1, d), scale[:, None, :], shift[:, None, :], w_bf16)
    return res if emit_h else res[0]


def _out_proj_kernel(a1_ref, a2_ref, w1_ref, w2_ref, x_ref, gt_ref, o_ref):
    y = jnp.dot(a1_ref[...], w1_ref[...], preferred_element_type=F32)
    y = y + jnp.dot(a2_ref[...], w2_ref[...], preferred_element_type=F32)
    o_ref[...] = x_ref[...] + gt_ref[0] * y


def out_proj_residual(a1, a2, w_bf16, x2, gate, seq, *, tm=1024, tn=512):
    t, k1 = a1.shape
    k2 = a2.shape[1]
    d = x2.shape[1]
    tm = min(tm, seq)
    per_seq = seq // tm
    return pl.pallas_call(
        _out_proj_kernel,
        grid=(t // tm, d // tn),
        in_specs=[pl.BlockSpec((tm, k1), lambda i, j: (i, 0)),
                  pl.BlockSpec((tm, k2), lambda i, j: (i, 0)),
                  pl.BlockSpec((k1, tn), lambda i, j: (0, j)),
                  pl.BlockSpec((k2, tn), lambda i, j: (0, j)),
                  pl.BlockSpec((tm, tn), lambda i, j: (i, j)),
                  pl.BlockSpec((1, 1, tn), lambda i, j: (i // per_seq, 0, j))],
        out_specs=pl.BlockSpec((tm, tn), lambda i, j: (i, j)),
        out_shape=jax.ShapeDtypeStruct((t, d), F32),
        compiler_params=_cparams("parallel", "parallel"),
        name="out_proj_residual",
    )(a1, a2, w_bf16[:k1], w_bf16[k1:], x2, gate[:, None, :])


def _dot_nt(a, b):
    return lax.dot_general(a, b, (((1,), (1,)), ((), ())), preferred_element_type=F32)


def _dot_tn(a, b):
    return lax.dot_general(a, b, (((0,), (0,)), ((), ())), preferred_element_type=F32)


def _retention_kernel(q_ref, k_ref, v_ref, g_ref, intra_ref, qdec_ref, kdec_ref, cdec_ref, norm_ref,
                      o_ref, state):
    @pl.when(pl.program_id(2) == 0)
    def _():
        state[...] = jnp.zeros_like(state)

    n_chunks = q_ref.shape[0] // CHUNK
    intra = intra_ref[0]
    qdec = qdec_ref[0]
    kdec = kdec_ref[0]
    cdec = cdec_ref[0]
    chunk_rows = [pl.ds(c * CHUNK, CHUNK) for c in range(n_chunks)]
    qs = [q_ref[rows, :] for rows in chunk_rows]
    ks = [k_ref[rows, :] * (HEAD_DIM ** -0.5) for rows in chunk_rows]
    vs = [v_ref[rows, :] for rows in chunk_rows]
    scores = [_dot_nt(qc, kc) * intra for qc, kc in zip(qs, ks)]
    updates = [_dot_tn(kc * kdec, vc) for kc, vc in zip(ks, vs)]
    states = [state[...]]
    for upd in updates:
        states.append(states[-1] * cdec + upd)
    state[...] = states[-1]
    for rows, qc, vc, s, st in zip(chunk_rows, qs, vs, scores, states):
        o = jnp.dot(s, vc, preferred_element_type=F32) + jnp.dot(qc * qdec, st, preferred_element_type=F32)
        on = o * lax.rsqrt(jnp.mean(o * o, axis=-1, keepdims=True) + EPS) * norm_ref[...]
        g = g_ref[rows, :]
        o_ref[rows, :] = (on * (g * jax.nn.sigmoid(g))).astype(o_ref.dtype)


def retention_mixer(proj, ret_norm, batch, seq, *, ts=512):
    t = proj.shape[0]
    ts = min(ts, seq)
    per_seq = seq // ts
    h8 = N_HEADS
    pos = jnp.arange(CHUNK, dtype=F32)
    lg = jnp.log1p(-jnp.exp2(-5.0 - jnp.arange(h8, dtype=F32)))[:, None]
    intra = jnp.exp(lg[..., None] * jnp.abs(pos[:, None] - pos[None, :]))
    qdec = jnp.broadcast_to(jnp.exp(lg * (pos + 1.0))[..., None], (h8, CHUNK, HEAD_DIM))
    kdec = jnp.broadcast_to(jnp.exp(lg * (CHUNK - 1.0 - pos))[..., None], (h8, CHUNK, HEAD_DIM))
    cdec = jnp.broadcast_to(jnp.exp(lg * CHUNK)[..., None], (h8, HEAD_DIM, HEAD_DIM))

    def col(off):
        return pl.BlockSpec((ts, HEAD_DIM), lambda b, h, s, off=off: (b * per_seq + s, off + h))

    hspec = lambda shape: pl.BlockSpec((1,) + shape, lambda b, h, s: (h, 0, 0))
    return pl.pallas_call(
        _retention_kernel,
        grid=(batch, h8, per_seq),
        in_specs=[col(0), col(h8), col(2 * h8), col(3 * h8),
                  hspec((CHUNK, CHUNK)), hspec((CHUNK, HEAD_DIM)), hspec((CHUNK, HEAD_DIM)),
                  hspec((HEAD_DIM, HEAD_DIM)),
                  pl.BlockSpec((1, HEAD_DIM), lambda b, h, s: (0, 0))],
        out_specs=pl.BlockSpec((ts, HEAD_DIM), lambda b, h, s: (b * per_seq + s, h)),
        out_shape=jax.ShapeDtypeStruct((t, h8 * HEAD_DIM), BF16),
        scratch_shapes=[pltpu.VMEM((HEAD_DIM, HEAD_DIM), F32)],
        compiler_params=_cparams("parallel", "parallel", "arbitrary"),
        name="retention",
    )(proj, proj, proj, proj, intra, qdec, kdec, cdec, ret_norm.reshape(1, HEAD_DIM))


ATTN_TQ = 256


def _diag_mask(tq):
    kc = lax.broadcasted_iota(jnp.int32, (tq, tq), 0) // CHUNK
    qc = lax.broadcasted_iota(jnp.int32, (tq, tq), 1) // CHUNK
    return kc <= qc


def _softmax_t(s_t, scale, mask):
    tq = mask.shape[0]
    ke = s_t.shape[0]
    y = s_t * (scale * math.log2(math.e))
    diag = jnp.where(mask, y[ke - tq:], NEG_INF)
    y = diag if ke == tq else jnp.concatenate([y[:ke - tq], diag], axis=0)
    e = jnp.exp2(y - jnp.max(y, axis=0, keepdims=True))
    return e, 1.0 / jnp.sum(e, axis=0, keepdims=True)


def _diff_attn_kernel(lam_ref, q_ref, k_ref, v_ref, norm_ref, o_ref, *, tq, lambda_init):
    seq = q_ref.shape[0]
    k = k_ref[...]
    half = lax.broadcasted_iota(jnp.int32, k.shape, 1) < (HEAD_DIM // 2)
    k_maps = (jnp.where(half, k, 0.0).astype(BF16), jnp.where(half, 0.0, k).astype(BF16))
    v_t = v_ref[...].T.astype(BF16)
    scale = (HEAD_DIM // 2) ** -0.5
    dl = lam_ref[...]
    lam = (jnp.exp(jnp.sum(dl[0:1] * dl[1:2], axis=-1, keepdims=True))
           - jnp.exp(jnp.sum(dl[2:3] * dl[3:4], axis=-1, keepdims=True)) + lambda_init)
    mask = _diag_mask(tq)
    blocks = [slice(qb * tq, (qb + 1) * tq) for qb in range(seq // tq)]
    qs = [q_ref[rows, :].astype(BF16) for rows in blocks]
    scores = [[_dot_nt(km[:rows.stop], q) for km in k_maps] for rows, q in zip(blocks, qs)]
    probs = [[_softmax_t(s_t, scale, mask) for s_t in pair] for pair in scores]
    for rows, ((e0, r0), (e1, r1)) in zip(blocks, probs):
        w_t = (e0 * r0 - e1 * (lam * r1)).astype(BF16)
        o = jnp.dot(v_t[:, :rows.stop], w_t, preferred_element_type=F32).T
        on = o * lax.rsqrt(jnp.mean(o * o, axis=-1, keepdims=True) + EPS) * norm_ref[...]
        o_ref[rows, :] = (on * (1.0 - lambda_init)).astype(o_ref.dtype)


def diff_attention_mixer(proj, diff_lambda, diff_norm, lambda_init, batch, seq):
    t = proj.shape[0]
    col = lambda off: pl.BlockSpec((seq, HEAD_DIM), lambda b, h, off=off: (b, off + h))
    return pl.pallas_call(
        functools.partial(_diff_attn_kernel, tq=min(ATTN_TQ, seq), lambda_init=lambda_init),
        grid=(batch, N_HEADS),
        in_specs=[pl.BlockSpec(diff_lambda.shape, lambda b, h: (0, 0)),
                  col(4 * N_HEADS), col(5 * N_HEADS), col(6 * N_HEADS),
                  pl.BlockSpec((1, HEAD_DIM), lambda b, h: (0, 0))],
        out_specs=pl.BlockSpec((seq, HEAD_DIM), lambda b, h: (b, h)),
        out_shape=jax.ShapeDtypeStruct((t, N_HEADS * HEAD_DIM), BF16),
        compiler_params=_cparams("parallel", "parallel"),
        name="diff_attention",
    )(diff_lambda, proj, proj, proj, diff_norm.reshape(1, HEAD_DIM))


def _split_mod(mod):
    return jnp.split(mod, 6, axis=-1)


def even_layer_mixer(x2, mod, p, layer, j, batch, seq):
    sh_m, sc_m, gt_m, _, _, _ = _split_mod(mod)
    proj = norm_matmul(x2, p["norm_mix"][layer], sc_m, sh_m, p["even_w_in"][j].astype(BF16), seq)
    ret = retention_mixer(proj, p["ret_norm"][j], batch, seq)
    lambda_init = 0.8 - 0.6 * math.exp(-0.3 * layer)
    dif = diff_attention_mixer(proj, p["diff_lambda"][j], p["diff_norm"][j], lambda_init, batch, seq)
    return out_proj_residual(ret, dif, p["even_w_out"][j].astype(BF16), x2, gt_m, seq)


ODD_COLS = 40 * LANES
GATE_BLOCK = 4 * N_HEADS


def _odd_w_in_layout(w):
    d = w.shape[0]
    hd = N_HEADS * HEAD_DIM
    zeros = lambda n: jnp.zeros((d, n), w.dtype)
    o = 4 * hd
    gates = w[:, o:o + 2 * N_HEADS]
    o += 2 * N_HEADS
    cq = w[:, o:o + MLA_Q_RANK]
    o += MLA_Q_RANK
    ckv = w[:, o:o + MLA_KV_RANK]
    o += MLA_KV_RANK
    kr = w[:, o:o + MLA_ROPE]
    return jnp.concatenate([w[:, :4 * hd], gates, zeros(LANES - 2 * N_HEADS), cq, ckv, kr,
                            zeros(LANES - MLA_ROPE), zeros(LANES)], axis=1)


def _silu(x):
    return x * jax.nn.sigmoid(x)


def _gdn_conv_kernel(x_ref, w_ref, o_ref):
    c = pl.program_id(1)
    x = x_ref[...]
    w = w_ref[...]
    width = w.shape[0]
    row = lax.broadcasted_iota(jnp.int32, x.shape, 0)
    y = x * w[width - 1:width]
    for sft in range(1, width):
        xs = jnp.where(row >= sft, pltpu.roll(x, sft, axis=0), 0.0)
        y = y + xs * w[width - 1 - sft:width - sft]
    y = _silu(y)
    yn = y * lax.rsqrt(jnp.sum(y * y, axis=-1, keepdims=True) + EPS)
    yn = yn * jnp.where(c < N_HEADS, HEAD_DIM ** -0.5, 1.0)
    o_ref[...] = jnp.where(c < 2 * N_HEADS, yn, y)


def gdn_conv(proj, conv_w, batch, seq):
    t = proj.shape[0]
    n_blk = 3 * N_HEADS
    return pl.pallas_call(
        _gdn_conv_kernel,
        grid=(batch, n_blk),
        in_specs=[pl.BlockSpec((seq, LANES), lambda b, c: (b, c)),
                  pl.BlockSpec((conv_w.shape[0], LANES), lambda b, c: (0, c))],
        out_specs=pl.BlockSpec((seq, LANES), lambda b, c: (b, c)),
        out_shape=jax.ShapeDtypeStruct((t, n_blk * LANES), F32),
        compiler_params=_cparams("parallel", "parallel"),
        name="gdn_conv",
    )(proj, conv_w)


def _dot_hi(a, b):
    def split(x):
        hi = x.astype(BF16)
        return hi, (x - hi.astype(F32)).astype(BF16)

    a_hi, a_lo = split(a)
    b_hi, b_lo = split(b)
    dot = functools.partial(jnp.dot, preferred_element_type=F32)
    return dot(a_hi, b_hi) + (dot(a_hi, b_lo) + dot(a_lo, b_hi))


def _unit_lower_inverses(l_mats):
    n = l_mats[0].shape[0]
    eye = (lax.broadcasted_iota(jnp.int32, (n, n), 0) == lax.broadcasted_iota(jnp.int32, (n, n), 1)).astype(F32)
    invs = [eye - l for l in l_mats]
    powers = [_dot_hi(l, l) for l in l_mats]
    span = 2
    while span < n:
        invs = [inv + _dot_hi(inv, p) for inv, p in zip(invs, powers)]
        span *= 2
        if span < n:
            powers = [_dot_hi(p, p) for p in powers]
    return invs


def _pick_lane(x, lane):
    sel = lax.broadcasted_iota(jnp.int32, x.shape, 1) == lane
    return jnp.sum(jnp.where(sel, x, 0.0), axis=-1, keepdims=True)


HEADS_PER_STEP = 2


def _gdn_kernel(q_ref, k_ref, v_ref, gg_ref, gate_ref, alog_ref, dtb_ref, norm_ref, o_ref, state):
    @pl.when(pl.program_id(2) == 0)
    def _():
        state[...] = jnp.zeros_like(state)

    n_chunks = q_ref.shape[0] // CHUNK
    ri = lax.broadcasted_iota(jnp.int32, (CHUNK, CHUNK), 0)
    ci = lax.broadcasted_iota(jnp.int32, (CHUNK, CHUNK), 1)
    tri = ci <= ri
    strict = ci < ri
    tri_f = tri.astype(F32)
    heads = [pl.program_id(1) * HEADS_PER_STEP + r for r in range(HEADS_PER_STEP)]
    a_coefs = [-jnp.exp(_pick_lane(alog_ref[...], h)) for h in heads]
    dt_biases = [_pick_lane(dtb_ref[...], h) for h in heads]
    units = [(pl.ds(c * CHUNK, CHUNK), slice(r * HEAD_DIM, (r + 1) * HEAD_DIM), r)
             for c in range(n_chunks) for r in range(HEADS_PER_STEP)]

    def local(rows, cols, r):
        k = k_ref[rows, cols]
        raw = gate_ref[rows, :]
        beta = jax.nn.sigmoid(_pick_lane(raw, heads[r]))
        ga = _pick_lane(raw, N_HEADS + heads[r]) + dt_biases[r]
        softplus = jnp.maximum(ga, 0.0) + jnp.log1p(jnp.exp(-jnp.abs(ga)))
        g = jnp.broadcast_to(a_coefs[r] * softplus, (CHUNK, HEAD_DIM))
        gc = _dot_hi(tri_f, g)
        g_rows = gc.T[:CHUNK, :]
        decay = jnp.where(tri, jnp.exp(jnp.where(tri, gc[:, :CHUNK] - g_rows, 0.0)), 0.0)
        k_beta = k * beta
        l_mat = jnp.where(strict, _dot_nt(k_beta, k) * decay, 0.0)
        return k, beta, gc, decay, k_beta, l_mat

    locs = [local(*unit) for unit in units]
    t_invs = _unit_lower_inverses([loc[-1] for loc in locs])
    preps = []
    for (rows, cols, _), (k, beta, gc, decay, k_beta, _), t_inv in zip(units, locs, t_invs):
        q = q_ref[rows, cols]
        eg = jnp.exp(gc)
        u = jnp.dot(t_inv, v_ref[rows, cols] * beta, preferred_element_type=F32)
        w = jnp.dot(t_inv, k_beta * eg, preferred_element_type=F32)
        g_last = gc[CHUNK - 1:CHUNK, :]
        preps.append((u, w, _dot_nt(q, k) * decay, q * eg, k * jnp.exp(g_last - gc), jnp.exp(g_last)))

    for (rows, cols, r), (u, w, intra, q_dec, k_dec, chunk_dec) in zip(units, preps):
        st = state[r]
        v_new = u - jnp.dot(w, st, preferred_element_type=F32)
        o = jnp.dot(q_dec, st, preferred_element_type=F32) + jnp.dot(intra, v_new, preferred_element_type=F32)
        state[r] = st * chunk_dec + _dot_tn(k_dec, v_new)
        on = o * lax.rsqrt(jnp.mean(o * o, axis=-1, keepdims=True) + EPS) * norm_ref[...]
        o_ref[rows, cols] = (on * _silu(gg_ref[rows, cols])).astype(o_ref.dtype)


def gdn_mixer(qkv, proj, a_log, dt_bias, gdn_norm, batch, seq, *, ts=512):
    t = qkv.shape[0]
    ts = min(ts, seq)
    per_seq = seq // ts
    hps = HEADS_PER_STEP
    pad = lambda v: jnp.pad(v.reshape(1, -1), ((0, 0), (0, LANES - v.shape[0])))

    def col(off):
        return pl.BlockSpec((ts, hps * HEAD_DIM), lambda b, h, s, off=off: (b * per_seq + s, off // hps + h))

    row1 = pl.BlockSpec((1, LANES), lambda b, h, s: (0, 0))
    return pl.pallas_call(
        _gdn_kernel,
        grid=(batch, N_HEADS // hps, per_seq),
        in_specs=[col(0), col(N_HEADS), col(2 * N_HEADS), col(3 * N_HEADS),
                  pl.BlockSpec((ts, LANES), lambda b, h, s: (b * per_seq + s, GATE_BLOCK)),
                  row1, row1, row1],
        out_specs=pl.BlockSpec((ts, hps * HEAD_DIM), lambda b, h, s: (b * per_seq + s, h)),
        out_shape=jax.ShapeDtypeStruct((t, N_HEADS * HEAD_DIM), BF16),
        scratch_shapes=[pltpu.VMEM((hps, HEAD_DIM, HEAD_DIM), F32)],
        compiler_params=_cparams("parallel", "parallel", "arbitrary"),
        name="gdn",
    )(qkv, qkv, qkv, proj, proj, pad(a_log), pad(dt_bias), gdn_norm.reshape(1, HEAD_DIM))


def _rope_tables(seq):
    half = MLA_ROPE // 2
    inv_freq = ROPE_THETA ** (-jnp.arange(half, dtype=F32) / half)
    ang = jnp.arange(seq, dtype=jnp.int32).astype(F32)[:, None] * inv_freq[None, :]
    cos, sin = jnp.cos(ang), jnp.sin(ang)
    z = jnp.zeros((seq, LANES - MLA_ROPE), F32)
    zh = jnp.zeros((seq, half), F32)
    return (jnp.concatenate([cos, cos, z], axis=1),
            jnp.concatenate([zh, sin, z], axis=1),
            jnp.concatenate([-sin, zh, z], axis=1))


def _rope(y, cos, sin_up, sin_dn):
    half = MLA_ROPE // 2
    return y * cos + pltpu.roll(y, half, axis=1) * sin_up + pltpu.roll(y, LANES - half, axis=1) * sin_dn


def _mla_proj_kernel(cq_ref, ckv_ref, kr_ref, qg_ref, kvg_ref, wqn_ref, wqr_ref, wkv_ref, cos_ref, sup_ref, sdn_ref,
                     qn_ref, qr_ref, kv_ref, kro_ref):
    def rms(x, g):
        return x * lax.rsqrt(jnp.mean(x * x, axis=-1, keepdims=True) + EPS) * g

    cos, sup, sdn = cos_ref[...], sup_ref[...], sdn_ref[...]
    cq = rms(cq_ref[...], qg_ref[...]).astype(BF16)
    qn_ref[...] = jnp.dot(cq, wqn_ref[...], preferred_element_type=F32).astype(qn_ref.dtype)
    qr = jnp.dot(cq, wqr_ref[...], preferred_element_type=F32)
    for h in range(N_HEADS):
        cols = slice(h * LANES, (h + 1) * LANES)
        qr_ref[:, cols] = _rope(qr[:, cols], cos, sup, sdn).astype(qr_ref.dtype)
    ckv = rms(ckv_ref[...], kvg_ref[...]).astype(BF16)
    kv_ref[...] = jnp.dot(ckv, wkv_ref[...], preferred_element_type=F32).astype(kv_ref.dtype)
    kro_ref[...] = _rope(kr_ref[...], cos, sup, sdn).astype(kro_ref.dtype)


def mla_projection(proj, q_norm, w_uq, kv_norm, w_ukv, seq, *, tm=512):
    t = proj.shape[0]
    tm = min(tm, seq)
    per_seq = seq // tm
    hd = N_HEADS * HEAD_DIM
    w3 = w_uq.reshape(MLA_Q_RANK, N_HEADS, HEAD_DIM + MLA_ROPE)
    wqn = w3[:, :, :HEAD_DIM].reshape(MLA_Q_RANK, hd).astype(BF16)
    wqr = jnp.pad(w3[:, :, HEAD_DIM:], ((0, 0), (0, 0), (0, LANES - MLA_ROPE))).reshape(MLA_Q_RANK, hd).astype(BF16)
    cos, sup, sdn = _rope_tables(seq)
    full = lambda a: pl.BlockSpec(a.shape, lambda i: (0,) * a.ndim)
    tab = pl.BlockSpec((tm, LANES), lambda i: (i % per_seq, 0))
    wkv = w_ukv.astype(BF16)
    qg, kvg = q_norm.reshape(1, -1), kv_norm.reshape(1, -1)
    return pl.pallas_call(
        _mla_proj_kernel,
        grid=(t // tm,),
        in_specs=[pl.BlockSpec((tm, MLA_Q_RANK), lambda i: (i, (GATE_BLOCK + 1) * LANES // MLA_Q_RANK)),
                  pl.BlockSpec((tm, MLA_KV_RANK), lambda i: (i, (GATE_BLOCK + 4) * LANES // MLA_KV_RANK)),
                  pl.BlockSpec((tm, LANES), lambda i: (i, GATE_BLOCK + 6)),
                  full(qg), full(kvg), full(wqn), full(wqr), full(wkv), tab, tab, tab],
        out_specs=[pl.BlockSpec((tm, hd), lambda i: (i, 0)), pl.BlockSpec((tm, hd), lambda i: (i, 0)),
                   pl.BlockSpec((tm, 2 * hd), lambda i: (i, 0)), pl.BlockSpec((tm, LANES), lambda i: (i, 0))],
        out_shape=[jax.ShapeDtypeStruct((t, hd), BF16), jax.ShapeDtypeStruct((t, hd), BF16),
                   jax.ShapeDtypeStruct((t, 2 * hd), BF16), jax.ShapeDtypeStruct((t, LANES), BF16)],
        compiler_params=_cparams("parallel"),
        name="mla_projection",
    )(proj, proj, proj, qg, kvg, wqn, wqr, wkv, cos, sup, sdn)


def _mla_attn_kernel(qn_ref, qr_ref, kn_ref, kr_ref, v_ref, o_ref, *, tq):
    seq = qn_ref.shape[0]
    kn = kn_ref[...]
    kr = kr_ref[...]
    v_t = v_ref[...].T
    scale = (HEAD_DIM + MLA_ROPE) ** -0.5
    mask = _diag_mask(tq)
    blocks = [slice(qb * tq, (qb + 1) * tq) for qb in range(seq // tq)]
    scores = [_dot_nt(kn[:rows.stop], qn_ref[rows, :]) + _dot_nt(kr[:rows.stop], qr_ref[rows, :]) for rows in blocks]
    probs = [_softmax_t(s_t, scale, mask) for s_t in scores]
    for rows, (e, r) in zip(blocks, probs):
        o_t = jnp.dot(v_t[:, :rows.stop], e.astype(BF16), preferred_element_type=F32) * r
        o_ref[rows, :] = o_t.T.astype(o_ref.dtype)


def mla_attention_mixer(qn, qr, kv, kr, batch, seq):
    t = qn.shape[0]
    qspec = pl.BlockSpec((seq, HEAD_DIM), lambda b, h: (b, h))
    return pl.pallas_call(
        functools.partial(_mla_attn_kernel, tq=min(ATTN_TQ, seq)),
        grid=(batch, N_HEADS),
        in_specs=[qspec, qspec,
                  pl.BlockSpec((seq, HEAD_DIM), lambda b, h: (b, 2 * h)),
                  pl.BlockSpec((seq, LANES), lambda b, h: (b, 0)),
                  pl.BlockSpec((seq, HEAD_DIM), lambda b, h: (b, 2 * h + 1))],
        out_specs=qspec,
        out_shape=jax.ShapeDtypeStruct((t, N_HEADS * HEAD_DIM), BF16),
        compiler_params=_cparams("parallel", "parallel"),
        name="mla_attention",
    )(qn, qr, kv, kr, kv)


def odd_layer_mixer(x2, mod, p, layer, j, batch, seq):
    sh_m, sc_m, gt_m, _, _, _ = _split_mod(mod)
    w_in = _odd_w_in_layout(p["odd_w_in"][j]).astype(BF16)
    proj = norm_matmul(x2, p["norm_mix"][layer], sc_m, sh_m, w_in, seq)
    qkv = gdn_conv(proj, p["gdn_conv"][j], batch, seq)
    gdn = gdn_mixer(qkv, proj, p["gdn_a_log"][j], p["gdn_dt_bias"][j], p["gdn_norm"][j], batch, seq)
    qn, qr, kv, kr = mla_projection(proj, p["mla_q_norm"][j], p["mla_w_uq"][j], p["mla_kv_norm"][j],
                                    p["mla_w_ukv"][j], seq)
    mla = mla_attention_mixer(qn, qr, kv, kr, batch, seq)
    return out_proj_residual(gdn, mla, p["odd_w_out"][j].astype(BF16), x2, gt_m, seq)


ROUTE_TOKENS = 128
ROUTE_HEADS_PER_STEP = 4


def _top16(*problems):
    n, tt = problems[0].shape
    row = lax.broadcasted_iota(jnp.int32, (n, tt), 0).astype(F32)
    r16 = lax.broadcasted_iota(jnp.int32, (PEER_TOPK, tt), 0)

    def body(k, carry):
        hit = r16 == k
        out = []
        for s, vals, ids in carry:
            m = jnp.max(s, axis=0, keepdims=True)
            i = jnp.min(jnp.where(s == m, row, float(n)), axis=0, keepdims=True)
            out.append((jnp.where(row == i, NEG_INF, s), jnp.where(hit, m, vals), jnp.where(hit, i, ids)))
        return tuple(out)

    zeros = jnp.zeros((PEER_TOPK, tt), F32)
    res = lax.fori_loop(0, PEER_TOPK, body, tuple((s, zeros, zeros) for s in problems))
    res = [(vals, ids) for _, vals, ids in res]
    return res[0] if len(problems) == 1 else res


def _pruned_candidates(v1, v2):
    tt = v1.shape[1]
    sub = lax.broadcasted_iota(jnp.int32, (SUBLANES, tt), 0)
    row = lambda v, r: v[r:r + 1, :]
    blocks = [row(v1, 0) + v2[:SUBLANES], row(v1, 0) + v2[SUBLANES:]]
    blocks += [row(v1, a) + v2[:SUBLANES] for a in range(1, 5)]
    a567 = jnp.where(sub < 2, row(v1, 5), jnp.where(sub < 4, row(v1, 6), row(v1, 7)))
    b01 = jnp.where((sub & 1) == 0, row(v2, 0), row(v2, 1))
    blocks.append(jnp.where(sub < 6, a567 + b01, NEG_INF))
    blocks.append(v1[SUBLANES:] + row(v2, 0))
    return jnp.concatenate(blocks, axis=0)


def _candidate_ranks(pos):
    p16 = pos - 16.0
    p48 = pos - 48.0
    a_mid = 1.0 + jnp.floor(p16 * 0.125)
    a_hi = 5.0 + jnp.floor(p48 * 0.5)
    a = jnp.where(pos < 16.0, 0.0, jnp.where(pos < 48.0, a_mid, jnp.where(pos < 56.0, a_hi, p48)))
    b = jnp.where(pos < 16.0, pos, jnp.where(pos < 48.0, p16 - 8.0 * (a_mid - 1.0),
                                             jnp.where(pos < 56.0, p48 - 2.0 * (a_hi - 5.0), 0.0)))
    return a, b


def _select_rows(table, sel):
    out = jnp.zeros_like(sel)
    for a in range(table.shape[0]):
        out = out + jnp.where(sel == float(a), table[a:a + 1, :], 0.0)
    return out


def _route_kernel(q_ref, keys_ref, idx_ref, gate_ref, cnt_ref, idx_all, gate_all, *, rows_per_expert):
    keys = [keys_ref[half].astype(BF16) for half in range(2)]

    def head_body(it, _):
        heads = [it * ROUTE_HEADS_PER_STEP + r for r in range(ROUTE_HEADS_PER_STEP)]
        tops = []
        for hd in heads:
            scores = []
            for half in range(2):
                col = pl.multiple_of((2 * hd + half) * LANES, LANES)
                scores.append(_dot_nt(keys[half], q_ref[:, pl.ds(col, LANES)].astype(BF16)))
            tops.append(_top16(*scores))
        finals = _top16(*[_pruned_candidates(v1, v2) for (v1, _), (v2, _) in tops])
        if ROUTE_HEADS_PER_STEP == 1:
            finals = [finals]
        for hd, ((_, i1), (_, i2)), (best, pos) in zip(heads, tops, finals):
            a_sel, b_sel = _candidate_ranks(pos)
            expert = _select_rows(i1, a_sel) * N_KEYS + _select_rows(i2, b_sel)
            e = jnp.exp(best - jnp.max(best, axis=0, keepdims=True))
            rows = pl.ds(pl.multiple_of(hd * PEER_TOPK, PEER_TOPK), PEER_TOPK)
            idx_all[rows, :] = expert.astype(jnp.int32)
            gate_all[rows, :] = e / jnp.sum(e, axis=0, keepdims=True)
        return 0

    lax.fori_loop(0, PEER_HEADS // ROUTE_HEADS_PER_STEP, head_body, 0)

    idx = idx_all[...]
    gate = gate_all[...]
    key1 = idx >> KEY_BITS
    key2 = idx & (N_KEYS - 1)
    per_bits = KEY_BITS - TILE_BITS
    tid = (key1 + (key2 >> per_bits)) & (N_EXPERT_TILES - 1)
    local = ((key1 << per_bits) + (key2 & ((1 << per_bits) - 1))) * rows_per_expert
    n = PEER_SLOTS
    before = (lax.broadcasted_iota(jnp.int32, (n, n), 1) < lax.broadcasted_iota(jnp.int32, (n, n), 0)).astype(F32)
    for k in range(N_EXPERT_TILES):
        mem = tid == k
        shift = jnp.dot(before, jnp.where(mem, 0.0, 1.0), preferred_element_type=F32).astype(jnp.int32)
        xi = jnp.where(mem, local, 0)
        xg = jnp.where(mem, gate, 0.0)
        xd = jnp.where(mem, shift, 0)
        for b in range(int(math.log2(n))):
            step = 1 << b
            mv = ((xd >> b) & 1) == 1
            pull = lambda a: pltpu.roll(a, n - step, axis=0)
            inc = pull(mv.astype(jnp.int32)) == 1
            xi = jnp.where(inc, pull(xi), jnp.where(mv, 0, xi))
            xg = jnp.where(inc, pull(xg), jnp.where(mv, 0.0, xg))
            xd = jnp.where(inc, pull(xd), jnp.where(mv, 0, xd))
        idx_ref[k] = xi.T
        gate_ref[k * n:(k + 1) * n, :] = xg
        cnt_ref[k:k + 1, :] = jnp.sum(mem.astype(jnp.int32), axis=0, keepdims=True)


def peer_route(q, sub_keys, rows_per_expert):
    t, d = q.shape
    tt = ROUTE_TOKENS
    n = N_EXPERT_TILES * PEER_SLOTS
    return pl.pallas_call(
        functools.partial(_route_kernel, rows_per_expert=rows_per_expert),
        grid=(t // tt,),
        in_specs=[pl.BlockSpec((tt, d), lambda i: (i, 0)),
                  pl.BlockSpec(sub_keys.shape, lambda i: (0, 0, 0))],
        out_specs=[pl.BlockSpec((N_EXPERT_TILES, tt, PEER_SLOTS), lambda i: (0, i, 0)),
                   pl.BlockSpec((n, tt), lambda i: (0, i)),
                   pl.BlockSpec((N_EXPERT_TILES, tt), lambda i: (0, i))],
        out_shape=[jax.ShapeDtypeStruct((N_EXPERT_TILES, t, PEER_SLOTS), jnp.int32),
                   jax.ShapeDtypeStruct((n, t), F32),
                   jax.ShapeDtypeStruct((N_EXPERT_TILES, t), jnp.int32)],
        scratch_shapes=[pltpu.VMEM((PEER_SLOTS, tt), jnp.int32), pltpu.VMEM((PEER_SLOTS, tt), F32)],
        compiler_params=_cparams("parallel"),
        name="peer_route",
    )(q, sub_keys)


EXPERT_TOKENS = 128


def _n_groups(cnt):
    return lax.shift_right_logical(cnt + (SLOT_GROUP - 1), int(math.log2(SLOT_GROUP)))


STATIC_SLOTS = 40
STATIC_GROUPS = STATIC_SLOTS // SLOT_GROUP
FOLD_ORDER = (0, 4, 2, 6, 1, 5, 3, 7)
TOKEN_UNROLL = 2


def _rows_of(i, n):
    return pl.ds(pl.multiple_of(i * n, n), n)


def _fold_group(parts):
    sub = lax.broadcasted_iota(jnp.int32, parts[0].shape, 0)
    xs = [parts[i] for i in FOLD_ORDER]
    half = SUBLANES // 2
    while half >= 1:
        keep = (sub & half) == 0
        xs = [jnp.where(keep, x, pltpu.roll(y, half, axis=0)) + jnp.where(keep, pltpu.roll(x, SUBLANES - half, axis=0), y)
              for x, y in zip(xs[0::2], xs[1::2])]
        half //= 2
    return xs[0]


def _down_kernel(idx_ref, cnt_ref, h_ref, tbl_ref, act_ref, part_scr):
    tb = act_ref.shape[1]
    rpe = h_ref.shape[0] // tb
    lane = lax.broadcasted_iota(jnp.int32, (SUBLANES, tb), 1)

    def partial(off, hv):
        p = tbl_ref[pl.ds(pl.multiple_of(off, rpe), rpe), :] * hv
        out = p[:SUBLANES]
        for c in range(1, rpe // SUBLANES):
            out = out + p[c * SUBLANES:(c + 1) * SUBLANES]
        return out

    def group(t, g, hv):
        return _fold_group([partial(idx_ref[t, g * SLOT_GROUP + u], hv) for u in range(SLOT_GROUP)])

    act_ref[STATIC_SLOTS:, :] = jnp.zeros((PEER_SLOTS - STATIC_SLOTS, tb), F32)

    def overflow(t, hv):
        n = cnt_ref[0, 0, t]

        @pl.when(n > STATIC_SLOTS)
        def _():
            def group_body(g, _):
                rows = _rows_of(g, SLOT_GROUP)
                col = jnp.sum(group(t, g, hv), axis=-1, keepdims=True)
                act_ref[rows, :] = jnp.where(lane == t, col, act_ref[rows, :])
                return 0

            lax.fori_loop(STATIC_GROUPS, _n_groups(n), group_body, 0)

    def tok_body(i, _):
        tokens = [i * TOKEN_UNROLL + r for r in range(TOKEN_UNROLL)]
        hvs = [h_ref[_rows_of(t, rpe), :] for t in tokens]
        for t, hv in zip(tokens, hvs):
            for g in range(STATIC_GROUPS):
                part_scr[_rows_of(t * STATIC_GROUPS + g, SUBLANES), :] = group(t, g, hv)
        for t, hv in zip(tokens, hvs):
            overflow(t, hv)
        return 0

    lax.fori_loop(0, tb // TOKEN_UNROLL, tok_body, 0)

    for g in range(STATIC_GROUPS):
        acc = jnp.zeros((SUBLANES, tb), F32)
        for t in range(tb):
            c = part_scr[(t * STATIC_GROUPS + g) * SUBLANES:(t * STATIC_GROUPS + g + 1) * SUBLANES, :]
            acc = jnp.where(lane == t, jnp.sum(c, axis=-1, keepdims=True), acc)
        act_ref[g * SLOT_GROUP:(g + 1) * SLOT_GROUP, :] = acc


def _smem_spec(block, index_map):
    return pl.BlockSpec(block, index_map, memory_space=pltpu.SMEM)


def _tile_spec(rpe, index_map):
    return pl.BlockSpec((EXPERT_TILE * rpe, LANES), index_map, pipeline_mode=pl.Buffered(1))


def peer_down_acts(idx, cnt, h2, table2):
    rpe = table2.shape[0] // (N_KEYS * N_KEYS)
    t = h2.shape[0] // rpe
    tb = EXPERT_TOKENS
    return pl.pallas_call(
        _down_kernel,
        grid=(N_EXPERT_TILES, t // tb),
        in_specs=[_smem_spec((tb, PEER_SLOTS), lambda k, i: (k * (t // tb) + i, 0)),
                  _smem_spec((1, 1, tb), lambda k, i: (k, 0, i)),
                  pl.BlockSpec((tb * rpe, LANES), lambda k, i: (i, 0)),
                  _tile_spec(rpe, lambda k, i: (k, 0))],
        out_specs=pl.BlockSpec((PEER_SLOTS, tb), lambda k, i: (k, i)),
        out_shape=jax.ShapeDtypeStruct((N_EXPERT_TILES * PEER_SLOTS, t), F32),
        scratch_shapes=[pltpu.VMEM((tb * STATIC_GROUPS * SUBLANES, LANES), F32)],
        compiler_params=_cparams("arbitrary", "arbitrary"),
        name="peer_down",
    )(idx, cnt, h2, table2)


def _coef_kernel(act_ref, gate_ref, o_ref):
    a = act_ref[...]
    coef = gate_ref[...] * (0.5 * a * (1.0 + lax.erf(a * (2.0 ** -0.5))))
    o_ref[...] = coef.T


def peer_coef(act, gate):
    n, t = gate.shape
    tb = 512 if t % 512 == 0 else EXPERT_TOKENS
    per_tile = t // tb
    spec = pl.BlockSpec((PEER_SLOTS, tb), lambda k, i: (k, i))
    return pl.pallas_call(
        _coef_kernel,
        grid=(N_EXPERT_TILES, per_tile),
        in_specs=[spec, spec],
        out_specs=pl.BlockSpec((tb, PEER_SLOTS), lambda k, i: (k * per_tile + i, 0)),
        out_shape=jax.ShapeDtypeStruct((N_EXPERT_TILES * t, PEER_SLOTS), F32),
        compiler_params=_cparams("parallel", "parallel"),
        name="peer_coef",
    )(act, gate)


def _up_kernel(idx_ref, coef_ref, cnt_ref, x_ref, gt_ref, tbl_ref, o_ref):
    tb = cnt_ref.shape[-1]
    rpe = x_ref.shape[0] // tb
    gt = gt_ref[...]

    def term(t, j):
        return coef_ref[t, j] * tbl_ref[pl.ds(pl.multiple_of(idx_ref[t, j], rpe), rpe), :]

    def token_body(t, _):
        accs = [jnp.zeros((rpe, LANES), F32), jnp.zeros((rpe, LANES), F32)]
        for j in range(STATIC_SLOTS):
            accs[j % 2] = accs[j % 2] + term(t, j)

        def group_body(g, acc):
            for u in range(SLOT_GROUP):
                acc = acc + term(t, g * SLOT_GROUP + u)
            return acc

        n_groups = jnp.maximum(_n_groups(cnt_ref[0, 0, t]), STATIC_GROUPS)
        acc = lax.fori_loop(STATIC_GROUPS, n_groups, group_body, accs[0] + accs[1])
        rows = _rows_of(t, rpe)
        o_ref[rows, :] = x_ref[rows, :] + gt * acc
        return 0

    lax.fori_loop(0, tb, token_body, 0)


def peer_up_tile(k, idx, coef, cnt, x2r, gate2r, table2, seq):
    rpe = table2.shape[0] // (N_KEYS * N_KEYS)
    t = x2r.shape[0] // rpe
    tb = min(EXPERT_TOKENS, seq)
    per_seq = seq // tb
    n_blk = t // tb
    return pl.pallas_call(
        _up_kernel,
        grid=(n_blk,),
        in_specs=[_smem_spec((tb, PEER_SLOTS), lambda i: (k * n_blk + i, 0)),
                  _smem_spec((tb, PEER_SLOTS), lambda i: (k * n_blk + i, 0)),
                  _smem_spec((1, 1, tb), lambda i: (k, 0, i)),
                  pl.BlockSpec((tb * rpe, LANES), lambda i: (i, 0)),
                  pl.BlockSpec((rpe, LANES), lambda i: (i // per_seq, 0)),
                  _tile_spec(rpe, lambda i: (k, 0))],
        out_specs=pl.BlockSpec((tb * rpe, LANES), lambda i: (i, 0)),
        out_shape=jax.ShapeDtypeStruct(x2r.shape, F32),
        compiler_params=_cparams("arbitrary"),
        name="peer_up",
    )(idx, coef, cnt, x2r, gate2r, table2)


def _tile_tables_kernel(x_ref, o_ref):
    nt = N_EXPERT_TILES
    per = N_KEYS // nt
    rpe = x_ref.shape[1] // LANES
    key1 = pl.program_id(1)
    for r in range(nt):
        tile = (key1 + r) % nt
        for c in range(rpe):
            o_ref[tile, pl.ds(c, per, stride=rpe), :] = x_ref[r * per:(r + 1) * per, c * LANES:(c + 1) * LANES]


def tile_tables(tables):
    depth, e, d = tables.shape
    nt = N_EXPERT_TILES
    per = N_KEYS // nt
    rpe = d // LANES
    out = pl.pallas_call(
        _tile_tables_kernel,
        grid=(depth, N_KEYS),
        in_specs=[pl.BlockSpec((None, N_KEYS, d), lambda l, k1: (l, k1, 0))],
        out_specs=pl.BlockSpec((None, nt, None, per * rpe, LANES), lambda l, k1: (l, 0, k1, 0, 0)),
        out_shape=jax.ShapeDtypeStruct((depth, nt, N_KEYS, per * rpe, LANES), tables.dtype),
        compiler_params=_cparams("parallel", "parallel"),
        name="tile_tables",
    )(tables)
    return out.reshape(depth, e * rpe, LANES)


def peer_layer(x2, mod, p, layer, batch, seq):
    _, _, _, sh_f, sc_f, gt_f = _split_mod(mod)
    t, d = x2.shape
    rpe = d // LANES
    q, h = norm_matmul(x2, p["norm_ffn"][layer], sc_f, sh_f, p["peer_w_query"][layer].astype(BF16), seq, emit_h=True)
    idx, gate, cnt = peer_route(q, p["peer_sub_keys"][layer], rpe)
    idx = idx.reshape(N_EXPERT_TILES * t, PEER_SLOTS)
    cnt = cnt.reshape(N_EXPERT_TILES, 1, t)
    act = peer_down_acts(idx, cnt, h.reshape(t * rpe, LANES), p["peer_down_rows"][layer])
    coef = peer_coef(act, gate)
    up2 = p["peer_up_rows"][layer]
    xr = x2.reshape(t * rpe, LANES)
    gtr = gt_f.reshape(batch * rpe, LANES)
    for k in range(N_EXPERT_TILES):
        xr = peer_up_tile(k, idx, coef, cnt, xr, gtr, up2, seq)
    return xr.reshape(t, d)


def _final_norm_kernel(x_ref, g_ref, o_ref):
    x = x_ref[...]
    o_ref[...] = x * lax.rsqrt(jnp.mean(x * x, axis=-1, keepdims=True) + EPS) * g_ref[...]


def final_rmsnorm(x2, gain, *, tm=512):
    t, d = x2.shape
    return pl.pallas_call(
        _final_norm_kernel,
        grid=(t // tm,),
        in_specs=[pl.BlockSpec((tm, d), lambda i: (i, 0)), pl.BlockSpec((1, d), lambda i: (0, 0))],
        out_specs=pl.BlockSpec((tm, d), lambda i: (i, 0)),
        out_shape=jax.ShapeDtypeStruct((t, d), F32),
        compiler_params=_cparams("parallel"),
        name="final_norm",
    )(x2, gain.reshape(1, d))


def kernel(x, c, norm_mix, norm_ffn, ada_w, ada_b, even_w_in, even_w_out, ret_norm, diff_lambda, diff_norm, odd_w_in, odd_w_out, gdn_conv, gdn_a_log, gdn_dt_bias, gdn_norm, mla_q_norm, mla_w_uq, mla_kv_norm, mla_w_ukv, peer_w_query, peer_sub_keys, peer_down, peer_up, final_norm):
    p = dict(norm_mix=norm_mix, norm_ffn=norm_ffn, even_w_in=even_w_in, even_w_out=even_w_out, ret_norm=ret_norm,
             diff_lambda=diff_lambda, diff_norm=diff_norm, odd_w_in=odd_w_in, odd_w_out=odd_w_out, gdn_conv=gdn_conv,
             gdn_a_log=gdn_a_log, gdn_dt_bias=gdn_dt_bias, gdn_norm=gdn_norm, mla_q_norm=mla_q_norm,
             mla_w_uq=mla_w_uq, mla_kv_norm=mla_kv_norm, mla_w_ukv=mla_w_ukv, peer_w_query=peer_w_query,
             peer_sub_keys=peer_sub_keys, peer_down_rows=tile_tables(peer_down), peer_up_rows=tile_tables(peer_up))
    batch, seq, d = x.shape
    depth = ada_w.shape[0]
    x2 = x.reshape(batch * seq, d)
    mod = modulation(c, ada_w, ada_b)
    for layer in range(depth):
        if layer % 2 == 0:
            x2 = even_layer_mixer(x2, mod[layer], p, layer, layer // 2, batch, seq)
        else:
            x2 = odd_layer_mixer(x2, mod[layer], p, layer, layer // 2, batch, seq)
        x2 = peer_layer(x2, mod[layer], p, layer, batch, seq)
    return final_rmsnorm(x2, final_norm).reshape(batch, seq, d)
```

```python
import functools
import math

import jax
import jax.numpy as jnp
import numpy as np
from jax import lax
from jax.experimental import pallas as pl
from jax.experimental.pallas import tpu as pltpu

F32 = jnp.float32
BF16 = jnp.bfloat16

CHUNK = 64
EPS = 1e-6
HEAD_DIM = 128
N_HEADS = 8
ROPE_THETA = 10000.0
MLA_Q_RANK = 384
MLA_KV_RANK = 256
MLA_ROPE = 64
N_KEYS = 128
PEER_HEADS = 8
PEER_TOPK = 16
PEER_SLOTS = PEER_HEADS * PEER_TOPK

LANES = 128
SUBLANES = 8
VMEM_LIMIT_BYTES = 56 * 1024 * 1024

N_EXPERT_TILES = 4
EXPERT_TILE = (N_KEYS * N_KEYS) // N_EXPERT_TILES
KEY_BITS = int(math.log2(N_KEYS))
TILE_BITS = int(math.log2(N_EXPERT_TILES))
SLOT_GROUP = 8

NEG_INF = float("-inf")


def _cparams(*sem):
    return pltpu.CompilerParams(dimension_semantics=sem, vmem_limit_bytes=VMEM_LIMIT_BYTES)


def _mod_kernel(c_ref, w_ref, b_ref, o_ref):
    c = c_ref[...]
    cond = c * jax.nn.sigmoid(c)
    o_ref[0] = jnp.dot(cond, w_ref[0], preferred_element_type=F32) + b_ref[0]


def modulation(c, ada_w, ada_b):
    depth, d, n = ada_w.shape
    b = c.shape[0]
    tn = 1536
    return pl.pallas_call(
        _mod_kernel,
        grid=(depth, n // tn),
        in_specs=[pl.BlockSpec((b, d), lambda l, j: (0, 0)),
                  pl.BlockSpec((1, d, tn), lambda l, j: (l, 0, j)),
                  pl.BlockSpec((1, 1, tn), lambda l, j: (l, 0, j))],
        out_specs=pl.BlockSpec((1, b, tn), lambda l, j: (l, 0, j)),
        out_shape=jax.ShapeDtypeStruct((depth, b, n), F32),
        compiler_params=_cparams("parallel", "parallel"),
        name="modulation",
    )(c, ada_w, ada_b.reshape(depth, 1, n))


def _norm_matmul_kernel(x_ref, g_ref, sc_ref, sh_ref, w_ref, *rest, emit_h):
    if emit_h:
        y_ref, h_ref, h_scr = rest
    else:
        y_ref, h_scr = rest

    @pl.when(pl.program_id(1) == 0)
    def _():
        x = x_ref[...]
        xn = x * lax.rsqrt(jnp.mean(x * x, axis=-1, keepdims=True) + EPS)
        h = xn * g_ref[...] * (1.0 + sc_ref[0]) + sh_ref[0]
        h_scr[...] = h.astype(BF16)
        if emit_h:
            h_ref[...] = h

    y_ref[...] = jnp.dot(h_scr[...], w_ref[...], preferred_element_type=F32)


def norm_matmul(x2, gain, scale, shift, w_bf16, seq, *, emit_h=False, tm=1024, tn=512):
    t, d = x2.shape
    n = w_bf16.shape[1]
    tm = min(tm, seq)
    assert seq % tm == 0 and n % tn == 0
    per_seq = seq // tm
    bvec = lambda i, j: (i // per_seq, 0, 0)
    out_shape = [jax.ShapeDtypeStruct((t, n), F32)]
    out_specs = [pl.BlockSpec((tm, tn), lambda i, j: (i, j))]
    if emit_h:
        out_shape.append(jax.ShapeDtypeStruct((t, d), F32))
        out_specs.append(pl.BlockSpec((tm, d), lambda i, j: (i, 0)))
    res = pl.pallas_call(
        functools.partial(_norm_matmul_kernel, emit_h=emit_h),
        grid=(t // tm, n // tn),
        in_specs=[pl.BlockSpec((tm, d), lambda i, j: (i, 0)),
                  pl.BlockSpec((1, d), lambda i, j: (0, 0)),
                  pl.BlockSpec((1, 1, d), bvec),
                  pl.BlockSpec((1, 1, d), bvec),
                  pl.BlockSpec((d, tn), lambda i, j: (0, j))],
        out_specs=out_specs,
        out_shape=out_shape,
        scratch_shapes=[pltpu.VMEM((tm, d), BF16)],
        compiler_params=_cparams("parallel", "arbitrary"),
        name="norm_matmul",
    )(x2, gain.reshape(1, d), scale[:, None, :], shift[:, None, :], w_bf16)
    return res if emit_h else res[0]


def _out_proj_kernel(a1_ref, a2_ref, w1_ref, w2_ref, x_ref, gt_ref, o_ref):
    y = jnp.dot(a1_ref[...], w1_ref[...], preferred_element_type=F32)
    y = y + jnp.dot(a2_ref[...], w2_ref[...], preferred_element_type=F32)
    o_ref[...] = x_ref[...] + gt_ref[0] * y


def out_proj_residual(a1, a2, w_bf16, x2, gate, seq, *, tm=1024, tn=512):
    t, k1 = a1.shape
    k2 = a2.shape[1]
    d = x2.shape[1]
    tm = min(tm, seq)
    per_seq = seq // tm
    return pl.pallas_call(
        _out_proj_kernel,
        grid=(t // tm, d // tn),
        in_specs=[pl.BlockSpec((tm, k1), lambda i, j: (i, 0)),
                  pl.BlockSpec((tm, k2), lambda i, j: (i, 0)),
                  pl.BlockSpec((k1, tn), lambda i, j: (0, j)),
                  pl.BlockSpec((k2, tn), lambda i, j: (0, j)),
                  pl.BlockSpec((tm, tn), lambda i, j: (i, j)),
                  pl.BlockSpec((1, 1, tn), lambda i, j: (i // per_seq, 0, j))],
        out_specs=pl.BlockSpec((tm, tn), lambda i, j: (i, j)),
        out_shape=jax.ShapeDtypeStruct((t, d), F32),
        compiler_params=_cparams("parallel", "parallel"),
        name="out_proj_residual",
    )(a1, a2, w_bf16[:k1], w_bf16[k1:], x2, gate[:, None, :])


def _dot_nt(a, b):
    return lax.dot_general(a, b, (((1,), (1,)), ((), ())), preferred_element_type=F32)


def _dot_tn(a, b):
    return lax.dot_general(a, b, (((0,), (0,)), ((), ())), preferred_element_type=F32)


def _retention_kernel(q_ref, k_ref, v_ref, g_ref, intra_ref, qdec_ref, kdec_ref, cdec_ref, norm_ref,
                      o_ref, state):
    @pl.when(pl.program_id(2) == 0)
    def _():
        state[...] = jnp.zeros_like(state)

    n_chunks = q_ref.shape[0] // CHUNK
    intra = intra_ref[0]
    qdec = qdec_ref[0]
    kdec = kdec_ref[0]
    cdec = cdec_ref[0]
    chunk_rows = [pl.ds(c * CHUNK, CHUNK) for c in range(n_chunks)]
    qs = [q_ref[rows, :] for rows in chunk_rows]
    ks = [k_ref[rows, :] * (HEAD_DIM ** -0.5) for rows in chunk_rows]
    vs = [v_ref[rows, :] for rows in chunk_rows]
    scores = [_dot_nt(qc, kc) * intra for qc, kc in zip(qs, ks)]
    updates = [_dot_tn(kc * kdec, vc) for kc, vc in zip(ks, vs)]
    states = [state[...]]
    for upd in updates:
        states.append(states[-1] * cdec + upd)
    state[...] = states[-1]
    for rows, qc, vc, s, st in zip(chunk_rows, qs, vs, scores, states):
        o = jnp.dot(s, vc, preferred_element_type=F32) + jnp.dot(qc * qdec, st, preferred_element_type=F32)
        on = o * lax.rsqrt(jnp.mean(o * o, axis=-1, keepdims=True) + EPS) * norm_ref[...]
        g = g_ref[rows, :]
        o_ref[rows, :] = (on * (g * jax.nn.sigmoid(g))).astype(o_ref.dtype)


def retention_mixer(proj, ret_norm, batch, seq, *, ts=512):
    t = proj.shape[0]
    ts = min(ts, seq)
    per_seq = seq // ts
    h8 = N_HEADS
    pos = jnp.arange(CHUNK, dtype=F32)
    lg = jnp.log1p(-jnp.exp2(-5.0 - jnp.arange(h8, dtype=F32)))[:, None]
    intra = jnp.exp(lg[..., None] * jnp.abs(pos[:, None] - pos[None, :]))
    qdec = jnp.broadcast_to(jnp.exp(lg * (pos + 1.0))[..., None], (h8, CHUNK, HEAD_DIM))
    kdec = jnp.broadcast_to(jnp.exp(lg * (CHUNK - 1.0 - pos))[..., None], (h8, CHUNK, HEAD_DIM))
    cdec = jnp.broadcast_to(jnp.exp(lg * CHUNK)[..., None], (h8, HEAD_DIM, HEAD_DIM))

    def col(off):
        return pl.BlockSpec((ts, HEAD_DIM), lambda b, h, s, off=off: (b * per_seq + s, off + h))

    hspec = lambda shape: pl.BlockSpec((1,) + shape, lambda b, h, s: (h, 0, 0))
    return pl.pallas_call(
        _retention_kernel,
        grid=(batch, h8, per_seq),
        in_specs=[col(0), col(h8), col(2 * h8), col(3 * h8),
                  hspec((CHUNK, CHUNK)), hspec((CHUNK, HEAD_DIM)), hspec((CHUNK, HEAD_DIM)),
                  hspec((HEAD_DIM, HEAD_DIM)),
                  pl.BlockSpec((1, HEAD_DIM), lambda b, h, s: (0, 0))],
        out_specs=pl.BlockSpec((ts, HEAD_DIM), lambda b, h, s: (b * per_seq + s, h)),
        out_shape=jax.ShapeDtypeStruct((t, h8 * HEAD_DIM), BF16),
        scratch_shapes=[pltpu.VMEM((HEAD_DIM, HEAD_DIM), F32)],
        compiler_params=_cparams("parallel", "parallel", "arbitrary"),
        name="retention",
    )(proj, proj, proj, proj, intra, qdec, kdec, cdec, ret_norm.reshape(1, HEAD_DIM))


ATTN_TQ = 256


def _diag_mask(tq):
    kc = lax.broadcasted_iota(jnp.int32, (tq, tq), 0) // CHUNK
    qc = lax.broadcasted_iota(jnp.int32, (tq, tq), 1) // CHUNK
    return kc <= qc


def _softmax_t(s_t, scale, mask):
    tq = mask.shape[0]
    ke = s_t.shape[0]
    y = s_t * (scale * math.log2(math.e))
    diag = jnp.where(mask, y[ke - tq:], NEG_INF)
    y = diag if ke == tq else jnp.concatenate([y[:ke - tq], diag], axis=0)
    e = jnp.exp2(y - jnp.max(y, axis=0, keepdims=True))
    return e, 1.0 / jnp.sum(e, axis=0, keepdims=True)


def _diff_attn_kernel(lam_ref, q_ref, k_ref, v_ref, norm_ref, o_ref, *, tq, lambda_init):
    seq = q_ref.shape[0]
    k = k_ref[...]
    half = lax.broadcasted_iota(jnp.int32, k.shape, 1) < (HEAD_DIM // 2)
    k_maps = (jnp.where(half, k, 0.0).astype(BF16), jnp.where(half, 0.0, k).astype(BF16))
    v_t = v_ref[...].T.astype(BF16)
    scale = (HEAD_DIM // 2) ** -0.5
    dl = lam_ref[...]
    lam = (jnp.exp(jnp.sum(dl[0:1] * dl[1:2], axis=-1, keepdims=True))
           - jnp.exp(jnp.sum(dl[2:3] * dl[3:4], axis=-1, keepdims=True)) + lambda_init)
    mask = _diag_mask(tq)
    blocks = [slice(qb * tq, (qb + 1) * tq) for qb in range(seq // tq)]
    qs = [q_ref[rows, :].astype(BF16) for rows in blocks]
    scores = [[_dot_nt(km[:rows.stop], q) for km in k_maps] for rows, q in zip(blocks, qs)]
    probs = [[_softmax_t(s_t, scale, mask) for s_t in pair] for pair in scores]
    for rows, ((e0, r0), (e1, r1)) in zip(blocks, probs):
        w_t = (e0 * r0 - e1 * (lam * r1)).astype(BF16)
        o = jnp.dot(v_t[:, :rows.stop], w_t, preferred_element_type=F32).T
        on = o * lax.rsqrt(jnp.mean(o * o, axis=-1, keepdims=True) + EPS) * norm_ref[...]
        o_ref[rows, :] = (on * (1.0 - lambda_init)).astype(o_ref.dtype)


def diff_attention_mixer(proj, diff_lambda, diff_norm, lambda_init, batch, seq):
    t = proj.shape[0]
    col = lambda off: pl.BlockSpec((seq, HEAD_DIM), lambda b, h, off=off: (b, off + h))
    return pl.pallas_call(
        functools.partial(_diff_attn_kernel, tq=min(ATTN_TQ, seq), lambda_init=lambda_init),
        grid=(batch, N_HEADS),
        in_specs=[pl.BlockSpec(diff_lambda.shape, lambda b, h: (0, 0)),
                  col(4 * N_HEADS), col(5 * N_HEADS), col(6 * N_HEADS),
                  pl.BlockSpec((1, HEAD_DIM), lambda b, h: (0, 0))],
        out_specs=pl.BlockSpec((seq, HEAD_DIM), lambda b, h: (b, h)),
        out_shape=jax.ShapeDtypeStruct((t, N_HEADS * HEAD_DIM), BF16),
        compiler_params=_cparams("parallel", "parallel"),
        name="diff_attention",
    )(diff_lambda, proj, proj, proj, diff_norm.reshape(1, HEAD_DIM))


def _split_mod(mod):
    return jnp.split(mod, 6, axis=-1)


def even_layer_mixer(x2, mod, p, layer, j, batch, seq):
    sh_m, sc_m, gt_m, _, _, _ = _split_mod(mod)
    proj = norm_matmul(x2, p["norm_mix"][layer], sc_m, sh_m, p["even_w_in"][j].astype(BF16), seq)
    ret = retention_mixer(proj, p["ret_norm"][j], batch, seq)
    lambda_init = 0.8 - 0.6 * math.exp(-0.3 * layer)
    dif = diff_attention_mixer(proj, p["diff_lambda"][j], p["diff_norm"][j], lambda_init, batch, seq)
    return out_proj_residual(ret, dif, p["even_w_out"][j].astype(BF16), x2, gt_m, seq)


ODD_COLS = 40 * LANES
GATE_BLOCK = 4 * N_HEADS


def _odd_w_in_layout(w):
    d = w.shape[0]
    hd = N_HEADS * HEAD_DIM
    zeros = lambda n: jnp.zeros((d, n), w.dtype)
    o = 4 * hd
    gates = w[:, o:o + 2 * N_HEADS]
    o += 2 * N_HEADS
    cq = w[:, o:o + MLA_Q_RANK]
    o += MLA_Q_RANK
    ckv = w[:, o:o + MLA_KV_RANK]
    o += MLA_KV_RANK
    kr = w[:, o:o + MLA_ROPE]
    return jnp.concatenate([w[:, :4 * hd], gates, zeros(LANES - 2 * N_HEADS), cq, ckv, kr,
                            zeros(LANES - MLA_ROPE), zeros(LANES)], axis=1)


def _silu(x):
    return x * jax.nn.sigmoid(x)


def _gdn_conv_kernel(x_ref, w_ref, o_ref):
    c = pl.program_id(1)
    x = x_ref[...]
    w = w_ref[...]
    width = w.shape[0]
    row = lax.broadcasted_iota(jnp.int32, x.shape, 0)
    y = x * w[width - 1:width]
    for sft in range(1, width):
        xs = jnp.where(row >= sft, pltpu.roll(x, sft, axis=0), 0.0)
        y = y + xs * w[width - 1 - sft:width - sft]
    y = _silu(y)
    yn = y * lax.rsqrt(jnp.sum(y * y, axis=-1, keepdims=True) + EPS)
    yn = yn * jnp.where(c < N_HEADS, HEAD_DIM ** -0.5, 1.0)
    o_ref[...] = jnp.where(c < 2 * N_HEADS, yn, y)


def gdn_conv(proj, conv_w, batch, seq):
    t = proj.shape[0]
    n_blk = 3 * N_HEADS
    return pl.pallas_call(
        _gdn_conv_kernel,
        grid=(batch, n_blk),
        in_specs=[pl.BlockSpec((seq, LANES), lambda b, c: (b, c)),
                  pl.BlockSpec((conv_w.shape[0], LANES), lambda b, c: (0, c))],
        out_specs=pl.BlockSpec((seq, LANES), lambda b, c: (b, c)),
        out_shape=jax.ShapeDtypeStruct((t, n_blk * LANES), F32),
        compiler_params=_cparams("parallel", "parallel"),
        name="gdn_conv",
    )(proj, conv_w)


def _dot_hi(a, b):
    def split(x):
        hi = x.astype(BF16)
        return hi, (x - hi.astype(F32)).astype(BF16)

    a_hi, a_lo = split(a)
    b_hi, b_lo = split(b)
    dot = functools.partial(jnp.dot, preferred_element_type=F32)
    return dot(a_hi, b_hi) + (dot(a_hi, b_lo) + dot(a_lo, b_hi))


def _unit_lower_inverses(l_mats):
    n = l_mats[0].shape[0]
    eye = (lax.broadcasted_iota(jnp.int32, (n, n), 0) == lax.broadcasted_iota(jnp.int32, (n, n), 1)).astype(F32)
    invs = [eye - l for l in l_mats]
    powers = [_dot_hi(l, l) for l in l_mats]
    span = 2
    while span < n:
        invs = [inv + _dot_hi(inv, p) for inv, p in zip(invs, powers)]
        span *= 2
        if span < n:
            powers = [_dot_hi(p, p) for p in powers]
    return invs


def _pick_lane(x, lane):
    sel = lax.broadcasted_iota(jnp.int32, x.shape, 1) == lane
    return jnp.sum(jnp.where(sel, x, 0.0), axis=-1, keepdims=True)


HEADS_PER_STEP = 2


def _gdn_kernel(q_ref, k_ref, v_ref, gg_ref, gate_ref, alog_ref, dtb_ref, norm_ref, o_ref, state):
    @pl.when(pl.program_id(2) == 0)
    def _():
        state[...] = jnp.zeros_like(state)

    n_chunks = q_ref.shape[0] // CHUNK
    ri = lax.broadcasted_iota(jnp.int32, (CHUNK, CHUNK), 0)
    ci = lax.broadcasted_iota(jnp.int32, (CHUNK, CHUNK), 1)
    tri = ci <= ri
    strict = ci < ri
    tri_f = tri.astype(F32)
    heads = [pl.program_id(1) * HEADS_PER_STEP + r for r in range(HEADS_PER_STEP)]
    a_coefs = [-jnp.exp(_pick_lane(alog_ref[...], h)) for h in heads]
    dt_biases = [_pick_lane(dtb_ref[...], h) for h in heads]
    units = [(pl.ds(c * CHUNK, CHUNK), slice(r * HEAD_DIM, (r + 1) * HEAD_DIM), r)
             for c in range(n_chunks) for r in range(HEADS_PER_STEP)]

    def local(rows, cols, r):
        k = k_ref[rows, cols]
        raw = gate_ref[rows, :]
        beta = jax.nn.sigmoid(_pick_lane(raw, heads[r]))
        ga = _pick_lane(raw, N_HEADS + heads[r]) + dt_biases[r]
        softplus = jnp.maximum(ga, 0.0) + jnp.log1p(jnp.exp(-jnp.abs(ga)))
        g = jnp.broadcast_to(a_coefs[r] * softplus, (CHUNK, HEAD_DIM))
        gc = _dot_hi(tri_f, g)
        g_rows = gc.T[:CHUNK, :]
        decay = jnp.where(tri, jnp.exp(jnp.where(tri, gc[:, :CHUNK] - g_rows, 0.0)), 0.0)
        k_beta = k * beta
        l_mat = jnp.where(strict, _dot_nt(k_beta, k) * decay, 0.0)
        return k, beta, gc, decay, k_beta, l_mat

    locs = [local(*unit) for unit in units]
    t_invs = _unit_lower_inverses([loc[-1] for loc in locs])
    preps = []
    for (rows, cols, _), (k, beta, gc, decay, k_beta, _), t_inv in zip(units, locs, t_invs):
        q = q_ref[rows, cols]
        eg = jnp.exp(gc)
        u = jnp.dot(t_inv, v_ref[rows, cols] * beta, preferred_element_type=F32)
        w = jnp.dot(t_inv, k_beta * eg, preferred_element_type=F32)
        g_last = gc[CHUNK - 1:CHUNK, :]
        preps.append((u, w, _dot_nt(q, k) * decay, q * eg, k * jnp.exp(g_last - gc), jnp.exp(g_last)))

    for (rows, cols, r), (u, w, intra, q_dec, k_dec, chunk_dec) in zip(units, preps):
        st = state[r]
        v_new = u - jnp.dot(w, st, preferred_element_type=F32)
        o = jnp.dot(q_dec, st, preferred_element_type=F32) + jnp.dot(intra, v_new, preferred_element_type=F32)
        state[r] = st * chunk_dec + _dot_tn(k_dec, v_new)
        on = o * lax.rsqrt(jnp.mean(o * o, axis=-1, keepdims=True) + EPS) * norm_ref[...]
        o_ref[rows, cols] = (on * _silu(gg_ref[rows, cols])).astype(o_ref.dtype)


def gdn_mixer(qkv, proj, a_log, dt_bias, gdn_norm, batch, seq, *, ts=512):
    t = qkv.shape[0]
    ts = min(ts, seq)
    per_seq = seq // ts
    hps = HEADS_PER_STEP
    pad = lambda v: jnp.pad(v.reshape(1, -1), ((0, 0), (0, LANES - v.shape[0])))

    def col(off):
        return pl.BlockSpec((ts, hps * HEAD_DIM), lambda b, h, s, off=off: (b * per_seq + s, off // hps + h))

    row1 = pl.BlockSpec((1, LANES), lambda b, h, s: (0, 0))
    return pl.pallas_call(
        _gdn_kernel,
        grid=(batch, N_HEADS // hps, per_seq),
        in_specs=[col(0), col(N_HEADS), col(2 * N_HEADS), col(3 * N_HEADS),
                  pl.BlockSpec((ts, LANES), lambda b, h, s: (b * per_seq + s, GATE_BLOCK)),
                  row1, row1, row1],
        out_specs=pl.BlockSpec((ts, hps * HEAD_DIM), lambda b, h, s: (b * per_seq + s, h)),
        out_shape=jax.ShapeDtypeStruct((t, N_HEADS * HEAD_DIM), BF16),
        scratch_shapes=[pltpu.VMEM((hps, HEAD_DIM, HEAD_DIM), F32)],
        compiler_params=_cparams("parallel", "parallel", "arbitrary"),
        name="gdn",
    )(qkv, qkv, qkv, proj, proj, pad(a_log), pad(dt_bias), gdn_norm.reshape(1, HEAD_DIM))


def _rope_tables(seq):
    half = MLA_ROPE // 2
    inv_freq = ROPE_THETA ** (-jnp.arange(half, dtype=F32) / half)
    ang = jnp.arange(seq, dtype=jnp.int32).astype(F32)[:, None] * inv_freq[None, :]
    cos, sin = jnp.cos(ang), jnp.sin(ang)
    z = jnp.zeros((seq, LANES - MLA_ROPE), F32)
    zh = jnp.zeros((seq, half), F32)
    return (jnp.concatenate([cos, cos, z], axis=1),
            jnp.concatenate([zh, sin, z], axis=1),
            jnp.concatenate([-sin, zh, z], axis=1))


def _rope(y, cos, sin_up, sin_dn):
    half = MLA_ROPE // 2
    return y * cos + pltpu.roll(y, half, axis=1) * sin_up + pltpu.roll(y, LANES - half, axis=1) * sin_dn


def _mla_proj_kernel(cq_ref, ckv_ref, kr_ref, qg_ref, kvg_ref, wqn_ref, wqr_ref, wkv_ref, cos_ref, sup_ref, sdn_ref,
                     qn_ref, qr_ref, kv_ref, kro_ref):
    def rms(x, g):
        return x * lax.rsqrt(jnp.mean(x * x, axis=-1, keepdims=True) + EPS) * g

    cos, sup, sdn = cos_ref[...], sup_ref[...], sdn_ref[...]
    cq = rms(cq_ref[...], qg_ref[...]).astype(BF16)
    qn_ref[...] = jnp.dot(cq, wqn_ref[...], preferred_element_type=F32).astype(qn_ref.dtype)
    qr = jnp.dot(cq, wqr_ref[...], preferred_element_type=F32)
    for h in range(N_HEADS):
        cols = slice(h * LANES, (h + 1) * LANES)
        qr_ref[:, cols] = _rope(qr[:, cols], cos, sup, sdn).astype(qr_ref.dtype)
    ckv = rms(ckv_ref[...], kvg_ref[...]).astype(BF16)
    kv_ref[...] = jnp.dot(ckv, wkv_ref[...], preferred_element_type=F32).astype(kv_ref.dtype)
    kro_ref[...] = _rope(kr_ref[...], cos, sup, sdn).astype(kro_ref.dtype)


def mla_projection(proj, q_norm, w_uq, kv_norm, w_ukv, seq, *, tm=512):
    t = proj.shape[0]
    tm = min(tm, seq)
    per_seq = seq // tm
    hd = N_HEADS * HEAD_DIM
    w3 = w_uq.reshape(MLA_Q_RANK, N_HEADS, HEAD_DIM + MLA_ROPE)
    wqn = w3[:, :, :HEAD_DIM].reshape(MLA_Q_RANK, hd).astype(BF16)
    wqr = jnp.pad(w3[:, :, HEAD_DIM:], ((0, 0), (0, 0), (0, LANES - MLA_ROPE))).reshape(MLA_Q_RANK, hd).astype(BF16)
    cos, sup, sdn = _rope_tables(seq)
    full = lambda a: pl.BlockSpec(a.shape, lambda i: (0,) * a.ndim)
    tab = pl.BlockSpec((tm, LANES), lambda i: (i % per_seq, 0))
    wkv = w_ukv.astype(BF16)
    qg, kvg = q_norm.reshape(1, -1), kv_norm.reshape(1, -1)
    return pl.pallas_call(
        _mla_proj_kernel,
        grid=(t // tm,),
        in_specs=[pl.BlockSpec((tm, MLA_Q_RANK), lambda i: (i, (GATE_BLOCK + 1) * LANES // MLA_Q_RANK)),
                  pl.BlockSpec((tm, MLA_KV_RANK), lambda i: (i, (GATE_BLOCK + 4) * LANES // MLA_KV_RANK)),
                  pl.BlockSpec((tm, LANES), lambda i: (i, GATE_BLOCK + 6)),
                  full(qg), full(kvg), full(wqn), full(wqr), full(wkv), tab, tab, tab],
        out_specs=[pl.BlockSpec((tm, hd), lambda i: (i, 0)), pl.BlockSpec((tm, hd), lambda i: (i, 0)),
                   pl.BlockSpec((tm, 2 * hd), lambda i: (i, 0)), pl.BlockSpec((tm, LANES), lambda i: (i, 0))],
        out_shape=[jax.ShapeDtypeStruct((t, hd), BF16), jax.ShapeDtypeStruct((t, hd), BF16),
                   jax.ShapeDtypeStruct((t, 2 * hd), BF16), jax.ShapeDtypeStruct((t, LANES), BF16)],
        compiler_params=_cparams("parallel"),
        name="mla_projection",
    )(proj, proj, proj, qg, kvg, wqn, wqr, wkv, cos, sup, sdn)


def _mla_attn_kernel(qn_ref, qr_ref, kn_ref, kr_ref, v_ref, o_ref, *, tq):
    seq = qn_ref.shape[0]
    kn = kn_ref[...]
    kr = kr_ref[...]
    v_t = v_ref[...].T
    scale = (HEAD_DIM + MLA_ROPE) ** -0.5
    mask = _diag_mask(tq)
    blocks = [slice(qb * tq, (qb + 1) * tq) for qb in range(seq // tq)]
    scores = [_dot_nt(kn[:rows.stop], qn_ref[rows, :]) + _dot_nt(kr[:rows.stop], qr_ref[rows, :]) for rows in blocks]
    probs = [_softmax_t(s_t, scale, mask) for s_t in scores]
    for rows, (e, r) in zip(blocks, probs):
        o_t = jnp.dot(v_t[:, :rows.stop], e.astype(BF16), preferred_element_type=F32) * r
        o_ref[rows, :] = o_t.T.astype(o_ref.dtype)


def mla_attention_mixer(qn, qr, kv, kr, batch, seq):
    t = qn.shape[0]
    qspec = pl.BlockSpec((seq, HEAD_DIM), lambda b, h: (b, h))
    return pl.pallas_call(
        functools.partial(_mla_attn_kernel, tq=min(ATTN_TQ, seq)),
        grid=(batch, N_HEADS),
        in_specs=[qspec, qspec,
                  pl.BlockSpec((seq, HEAD_DIM), lambda b, h: (b, 2 * h)),
                  pl.BlockSpec((seq, LANES), lambda b, h: (b, 0)),
                  pl.BlockSpec((seq, HEAD_DIM), lambda b, h: (b, 2 * h + 1))],
        out_specs=qspec,
        out_shape=jax.ShapeDtypeStruct((t, N_HEADS * HEAD_DIM), BF16),
        compiler_params=_cparams("parallel", "parallel"),
        name="mla_attention",
    )(qn, qr, kv, kr, kv)


def odd_layer_mixer(x2, mod, p, layer, j, batch, seq):
    sh_m, sc_m, gt_m, _, _, _ = _split_mod(mod)
    w_in = _odd_w_in_layout(p["odd_w_in"][j]).astype(BF16)
    proj = norm_matmul(x2, p["norm_mix"][layer], sc_m, sh_m, w_in, seq)
    qkv = gdn_conv(proj, p["gdn_conv"][j], batch, seq)
    gdn = gdn_mixer(qkv, proj, p["gdn_a_log"][j], p["gdn_dt_bias"][j], p["gdn_norm"][j], batch, seq)
    qn, qr, kv, kr = mla_projection(proj, p["mla_q_norm"][j], p["mla_w_uq"][j], p["mla_kv_norm"][j],
                                    p["mla_w_ukv"][j], seq)
    mla = mla_attention_mixer(qn, qr, kv, kr, batch, seq)
    return out_proj_residual(gdn, mla, p["odd_w_out"][j].astype(BF16), x2, gt_m, seq)


ROUTE_TOKENS = 128
ROUTE_HEADS_PER_STEP = 4


def _top16(*problems):
    n, tt = problems[0].shape
    row = lax.broadcasted_iota(jnp.int32, (n, tt), 0).astype(F32)
    r16 = lax.broadcasted_iota(jnp.int32, (PEER_TOPK, tt), 0)

    def body(k, carry):
        hit = r16 == k
        out = []
        for s, vals, ids in carry:
            m = jnp.max(s, axis=0, keepdims=True)
            i = jnp.min(jnp.where(s == m, row, float(n)), axis=0, keepdims=True)
            out.append((jnp.where(row == i, NEG_INF, s), jnp.where(hit, m, vals), jnp.where(hit, i, ids)))
        return tuple(out)

    zeros = jnp.zeros((PEER_TOPK, tt), F32)
    res = lax.fori_loop(0, PEER_TOPK, body, tuple((s, zeros, zeros) for s in problems))
    res = [(vals, ids) for _, vals, ids in res]
    return res[0] if len(problems) == 1 else res


def _pruned_candidates(v1, v2):
    tt = v1.shape[1]
    sub = lax.broadcasted_iota(jnp.int32, (SUBLANES, tt), 0)
    row = lambda v, r: v[r:r + 1, :]
    blocks = [row(v1, 0) + v2[:SUBLANES], row(v1, 0) + v2[SUBLANES:]]
    blocks += [row(v1, a) + v2[:SUBLANES] for a in range(1, 5)]
    a567 = jnp.where(sub < 2, row(v1, 5), jnp.where(sub < 4, row(v1, 6), row(v1, 7)))
    b01 = jnp.where((sub & 1) == 0, row(v2, 0), row(v2, 1))
    blocks.append(jnp.where(sub < 6, a567 + b01, NEG_INF))
    blocks.append(v1[SUBLANES:] + row(v2, 0))
    return jnp.concatenate(blocks, axis=0)


def _candidate_ranks(pos):
    p16 = pos - 16.0
    p48 = pos - 48.0
    a_mid = 1.0 + jnp.floor(p16 * 0.125)
    a_hi = 5.0 + jnp.floor(p48 * 0.5)
    a = jnp.where(pos < 16.0, 0.0, jnp.where(pos < 48.0, a_mid, jnp.where(pos < 56.0, a_hi, p48)))
    b = jnp.where(pos < 16.0, pos, jnp.where(pos < 48.0, p16 - 8.0 * (a_mid - 1.0),
                                             jnp.where(pos < 56.0, p48 - 2.0 * (a_hi - 5.0), 0.0)))
    return a, b


def _select_rows(table, sel):
    out = jnp.zeros_like(sel)
    for a in range(table.shape[0]):
        out = out + jnp.where(sel == float(a), table[a:a + 1, :], 0.0)
    return out


def _route_kernel(q_ref, keys_ref, idx_ref, gate_ref, cnt_ref, idx_all, gate_all, *, rows_per_expert):
    keys = [keys_ref[half].astype(BF16) for half in range(2)]

    def head_body(it, _):
        heads = [it * ROUTE_HEADS_PER_STEP + r for r in range(ROUTE_HEADS_PER_STEP)]
        tops = []
        for hd in heads:
            scores = []
            for half in range(2):
                col = pl.multiple_of((2 * hd + half) * LANES, LANES)
                scores.append(_dot_nt(keys[half], q_ref[:, pl.ds(col, LANES)].astype(BF16)))
            tops.append(_top16(*scores))
        finals = _top16(*[_pruned_candidates(v1, v2) for (v1, _), (v2, _) in tops])
        if ROUTE_HEADS_PER_STEP == 1:
            finals = [finals]
        for hd, ((_, i1), (_, i2)), (best, pos) in zip(heads, tops, finals):
            a_sel, b_sel = _candidate_ranks(pos)
            expert = _select_rows(i1, a_sel) * N_KEYS + _select_rows(i2, b_sel)
            e = jnp.exp(best - jnp.max(best, axis=0, keepdims=True))
            rows = pl.ds(pl.multiple_of(hd * PEER_TOPK, PEER_TOPK), PEER_TOPK)
            idx_all[rows, :] = expert.astype(jnp.int32)
            gate_all[rows, :] = e / jnp.sum(e, axis=0, keepdims=True)
        return 0

    lax.fori_loop(0, PEER_HEADS // ROUTE_HEADS_PER_STEP, head_body, 0)

    idx = idx_all[...]
    gate = gate_all[...]
    key1 = idx >> KEY_BITS
    key2 = idx & (N_KEYS - 1)
    per_bits = KEY_BITS - TILE_BITS
    tid = (key1 + (key2 >> per_bits)) & (N_EXPERT_TILES - 1)
    local = ((key1 << per_bits) + (key2 & ((1 << per_bits) - 1))) * rows_per_expert
    n = PEER_SLOTS
    before = (lax.broadcasted_iota(jnp.int32, (n, n), 1) < lax.broadcasted_iota(jnp.int32, (n, n), 0)).astype(F32)
    for k in range(N_EXPERT_TILES):
        mem = tid == k
        shift = jnp.dot(before, jnp.where(mem, 0.0, 1.0), preferred_element_type=F32).astype(jnp.int32)
        xi = jnp.where(mem, local, 0)
        xg = jnp.where(mem, gate, 0.0)
        xd = jnp.where(mem, shift, 0)
        for b in range(int(math.log2(n))):
            step = 1 << b
            mv = ((xd >> b) & 1) == 1
            pull = lambda a: pltpu.roll(a, n - step, axis=0)
            inc = pull(mv.astype(jnp.int32)) == 1
            xi = jnp.where(inc, pull(xi), jnp.where(mv, 0, xi))
            xg = jnp.where(inc, pull(xg), jnp.where(mv, 0.0, xg))
            xd = jnp.where(inc, pull(xd), jnp.where(mv, 0, xd))
        idx_ref[k] = xi.T
        gate_ref[k * n:(k + 1) * n, :] = xg
        cnt_ref[k:k + 1, :] = jnp.sum(mem.astype(jnp.int32), axis=0, keepdims=True)


def peer_route(q, sub_keys, rows_per_expert):
    t, d = q.shape
    tt = ROUTE_TOKENS
    n = N_EXPERT_TILES * PEER_SLOTS
    return pl.pallas_call(
        functools.partial(_route_kernel, rows_per_expert=rows_per_expert),
        grid=(t // tt,),
        in_specs=[pl.BlockSpec((tt, d), lambda i: (i, 0)),
                  pl.BlockSpec(sub_keys.shape, lambda i: (0, 0, 0))],
        out_specs=[pl.BlockSpec((N_EXPERT_TILES, tt, PEER_SLOTS), lambda i: (0, i, 0)),
                   pl.BlockSpec((n, tt), lambda i: (0, i)),
                   pl.BlockSpec((N_EXPERT_TILES, tt), lambda i: (0, i))],
        out_shape=[jax.ShapeDtypeStruct((N_EXPERT_TILES, t, PEER_SLOTS), jnp.int32),
                   jax.ShapeDtypeStruct((n, t), F32),
                   jax.ShapeDtypeStruct((N_EXPERT_TILES, t), jnp.int32)],
        scratch_shapes=[pltpu.VMEM((PEER_SLOTS, tt), jnp.int32), pltpu.VMEM((PEER_SLOTS, tt), F32)],
        compiler_params=_cparams("parallel"),
        name="peer_route",
    )(q, sub_keys)


EXPERT_TOKENS = 128


def _n_groups(cnt):
    return lax.shift_right_logical(cnt + (SLOT_GROUP - 1), int(math.log2(SLOT_GROUP)))


STATIC_SLOTS = 40
STATIC_GROUPS = STATIC_SLOTS // SLOT_GROUP
FOLD_ORDER = (0, 4, 2, 6, 1, 5, 3, 7)
TOKEN_UNROLL = 2


def _rows_of(i, n):
    return pl.ds(pl.multiple_of(i * n, n), n)


def _fold_group(parts):
    sub = lax.broadcasted_iota(jnp.int32, parts[0].shape, 0)
    xs = [parts[i] for i in FOLD_ORDER]
    half = SUBLANES // 2
    while half >= 1:
        keep = (sub & half) == 0
        xs = [jnp.where(keep, x, pltpu.roll(y, half, axis=0)) + jnp.where(keep, pltpu.roll(x, SUBLANES - half, axis=0), y)
              for x, y in zip(xs[0::2], xs[1::2])]
        half //= 2
    return xs[0]


def _down_kernel(idx_ref, cnt_ref, h_ref, gate_ref, tbl_ref, coef_ref, act_ref):
    tb = act_ref.shape[1]
    rpe = h_ref.shape[0] // tb
    lane = lax.broadcasted_iota(jnp.int32, (SUBLANES, tb), 1)

    def partial(off, hv):
        p = tbl_ref[pl.ds(pl.multiple_of(off, rpe), rpe), :] * hv
        out = p[:SUBLANES]
        for c in range(1, rpe // SUBLANES):
            out = out + p[c * SUBLANES:(c + 1) * SUBLANES]
        return out

    def group(t, g, hv):
        return _fold_group([partial(idx_ref[t, g * SLOT_GROUP + u], hv) for u in range(SLOT_GROUP)])

    act_ref[STATIC_SLOTS:, :] = jnp.zeros((PEER_SLOTS - STATIC_SLOTS, tb), F32)

    def overflow(t, hv):
        n = cnt_ref[0, 0, t]

        @pl.when(n > STATIC_SLOTS)
        def _():
            def group_body(g, _):
                rows = _rows_of(g, SLOT_GROUP)
                col = jnp.sum(group(t, g, hv), axis=-1, keepdims=True)
                act_ref[rows, :] = jnp.where(lane == t, col, act_ref[rows, :])
                return 0

            lax.fori_loop(STATIC_GROUPS, _n_groups(n), group_body, 0)

    def absorb(accs, folds, first_token):
        accs = list(accs)
        for r in range(TOKEN_UNROLL):
            for g in range(STATIC_GROUPS):
                col = jnp.sum(folds[r * STATIC_GROUPS + g], axis=-1, keepdims=True)
                accs[g] = jnp.where(lane == first_token + r, col, accs[g])
        return tuple(accs)

    def tok_body(i, carry):
        accs, folds = carry
        accs = absorb(accs, folds, (i - 1) * TOKEN_UNROLL)
        tokens = [i * TOKEN_UNROLL + r for r in range(TOKEN_UNROLL)]
        hvs = [h_ref[_rows_of(t, rpe), :] for t in tokens]
        folds = tuple(group(t, g, hv) for t, hv in zip(tokens, hvs) for g in range(STATIC_GROUPS))
        for t, hv in zip(tokens, hvs):
            overflow(t, hv)
        return accs, folds

    assert tb == LANES
    zeros = act_ref[STATIC_SLOTS:STATIC_SLOTS + SUBLANES, :]
    n_iter = tb // TOKEN_UNROLL
    init = ((zeros,) * STATIC_GROUPS, (zeros,) * (TOKEN_UNROLL * STATIC_GROUPS))
    accs, folds = lax.fori_loop(0, n_iter, tok_body, init)
    accs = absorb(accs, folds, (n_iter - 1) * TOKEN_UNROLL)
    a = jnp.concatenate(list(accs) + [act_ref[STATIC_SLOTS:, :]], axis=0)
    coef = gate_ref[...] * (0.5 * a * (1.0 + lax.erf(a * (2.0 ** -0.5))))
    coef_ref[...] = coef.T


def _smem_spec(block, index_map):
    return pl.BlockSpec(block, index_map, memory_space=pltpu.SMEM)


def _tile_spec(rpe, index_map):
    return pl.BlockSpec((EXPERT_TILE * rpe, LANES), index_map, pipeline_mode=pl.Buffered(1))


def peer_down_coefs(idx, cnt, h2, gate, table2):
    rpe = table2.shape[0] // (N_KEYS * N_KEYS)
    t = h2.shape[0] // rpe
    tb = EXPERT_TOKENS
    n_blk = t // tb
    return pl.pallas_call(
        _down_kernel,
        grid=(N_EXPERT_TILES, n_blk),
        in_specs=[_smem_spec((tb, PEER_SLOTS), lambda k, i: (k * n_blk + i, 0)),
                  _smem_spec((1, 1, tb), lambda k, i: (k, 0, i)),
                  pl.BlockSpec((tb * rpe, LANES), lambda k, i: (i, 0)),
                  pl.BlockSpec((PEER_SLOTS, tb), lambda k, i: (k, i)),
                  _tile_spec(rpe, lambda k, i: (k, 0))],
        out_specs=pl.BlockSpec((tb, PEER_SLOTS), lambda k, i: (k * n_blk + i, 0)),
        out_shape=jax.ShapeDtypeStruct((N_EXPERT_TILES * t, PEER_SLOTS), F32),
        scratch_shapes=[pltpu.VMEM((PEER_SLOTS, tb), F32)],
        compiler_params=_cparams("arbitrary", "arbitrary"),
        name="peer_down",
    )(idx, cnt, h2, gate, table2)


def _up_kernel(idx_ref, coef_ref, cnt_ref, x_ref, gt_ref, tbl_ref, o_ref):
    tb = cnt_ref.shape[-1]
    rpe = x_ref.shape[0] // tb
    gt = gt_ref[...]

    def term(t, j):
        return coef_ref[t, j] * tbl_ref[pl.ds(pl.multiple_of(idx_ref[t, j], rpe), rpe), :]

    def token_body(t, _):
        accs = [jnp.zeros((rpe, LANES), F32), jnp.zeros((rpe, LANES), F32)]
        for j in range(STATIC_SLOTS):
            accs[j % 2] = accs[j % 2] + term(t, j)

        def group_body(g, acc):
            for u in range(SLOT_GROUP):
                acc = acc + term(t, g * SLOT_GROUP + u)
            return acc

        n_groups = jnp.maximum(_n_groups(cnt_ref[0, 0, t]), STATIC_GROUPS)
        acc = lax.fori_loop(STATIC_GROUPS, n_groups, group_body, accs[0] + accs[1])
        rows = _rows_of(t, rpe)
        o_ref[rows, :] = x_ref[rows, :] + gt * acc
        return 0

    lax.fori_loop(0, tb, token_body, 0)


def peer_up_tile(k, idx, coef, cnt, x2r, gate2r, table2, seq):
    rpe = table2.shape[0] // (N_KEYS * N_KEYS)
    t = x2r.shape[0] // rpe
    tb = min(EXPERT_TOKENS, seq)
    per_seq = seq // tb
    n_blk = t // tb
    return pl.pallas_call(
        _up_kernel,
        grid=(n_blk,),
        in_specs=[_smem_spec((tb, PEER_SLOTS), lambda i: (k * n_blk + i, 0)),
                  _smem_spec((tb, PEER_SLOTS), lambda i: (k * n_blk + i, 0)),
                  _smem_spec((1, 1, tb), lambda i: (k, 0, i)),
                  pl.BlockSpec((tb * rpe, LANES), lambda i: (i, 0)),
                  pl.BlockSpec((rpe, LANES), lambda i: (i // per_seq, 0)),
                  _tile_spec(rpe, lambda i: (k, 0))],
        out_specs=pl.BlockSpec((tb * rpe, LANES), lambda i: (i, 0)),
        out_shape=jax.ShapeDtypeStruct(x2r.shape, F32),
        compiler_params=_cparams("arbitrary"),
        name="peer_up",
    )(idx, coef, cnt, x2r, gate2r, table2)


def _tile_tables_kernel(x_ref, o_ref):
    nt = N_EXPERT_TILES
    per = N_KEYS // nt
    rpe = x_ref.shape[1] // LANES
    key1 = pl.program_id(1)
    for r in range(nt):
        tile = (key1 + r) % nt
        for c in range(rpe):
            o_ref[tile, pl.ds(c, per, stride=rpe), :] = x_ref[r * per:(r + 1) * per, c * LANES:(c + 1) * LANES]


def tile_tables(tables):
    depth, e, d = tables.shape
    nt = N_EXPERT_TILES
    per = N_KEYS // nt
    rpe = d // LANES
    out = pl.pallas_call(
        _tile_tables_kernel,
        grid=(depth, N_KEYS),
        in_specs=[pl.BlockSpec((None, N_KEYS, d), lambda l, k1: (l, k1, 0))],
        out_specs=pl.BlockSpec((None, nt, None, per * rpe, LANES), lambda l, k1: (l, 0, k1, 0, 0)),
        out_shape=jax.ShapeDtypeStruct((depth, nt, N_KEYS, per * rpe, LANES), tables.dtype),
        compiler_params=_cparams("parallel", "parallel"),
        name="tile_tables",
    )(tables)
    return out.reshape(depth, e * rpe, LANES)


def peer_layer(x2, mod, p, layer, batch, seq):
    _, _, _, sh_f, sc_f, gt_f = _split_mod(mod)
    t, d = x2.shape
    rpe = d // LANES
    q, h = norm_matmul(x2, p["norm_ffn"][layer], sc_f, sh_f, p["peer_w_query"][layer].astype(BF16), seq, emit_h=True)
    idx, gate, cnt = peer_route(q, p["peer_sub_keys"][layer], rpe)
    idx = idx.reshape(N_EXPERT_TILES * t, PEER_SLOTS)
    cnt = cnt.reshape(N_EXPERT_TILES, 1, t)
    coef = peer_down_coefs(idx, cnt, h.reshape(t * rpe, LANES), gate, p["peer_down_rows"][layer])
    up2 = p["peer_up_rows"][layer]
    xr = x2.reshape(t * rpe, LANES)
    gtr = gt_f.reshape(batch * rpe, LANES)
    for k in range(N_EXPERT_TILES):
        xr = peer_up_tile(k, idx, coef, cnt, xr, gtr, up2, seq)
    return xr.reshape(t, d)


def _final_norm_kernel(x_ref, g_ref, o_ref):
    x = x_ref[...]
    o_ref[...] = x * lax.rsqrt(jnp.mean(x * x, axis=-1, keepdims=True) + EPS) * g_ref[...]


def final_rmsnorm(x2, gain, *, tm=512):
    t, d = x2.shape
    return pl.pallas_call(
        _final_norm_kernel,
        grid=(t // tm,),
        in_specs=[pl.BlockSpec((tm, d), lambda i: (i, 0)), pl.BlockSpec((1, d), lambda i: (0, 0))],
        out_specs=pl.BlockSpec((tm, d), lambda i: (i, 0)),
        out_shape=jax.ShapeDtypeStruct((t, d), F32),
        compiler_params=_cparams("parallel"),
        name="final_norm",
    )(x2, gain.reshape(1, d))


def kernel(x, c, norm_mix, norm_ffn, ada_w, ada_b, even_w_in, even_w_out, ret_norm, diff_lambda, diff_norm, odd_w_in, odd_w_out, gdn_conv, gdn_a_log, gdn_dt_bias, gdn_norm, mla_q_norm, mla_w_uq, mla_kv_norm, mla_w_ukv, peer_w_query, peer_sub_keys, peer_down, peer_up, final_norm):
    p = dict(norm_mix=norm_mix, norm_ffn=norm_ffn, even_w_in=even_w_in, even_w_out=even_w_out, ret_norm=ret_norm,
             diff_lambda=diff_lambda, diff_norm=diff_norm, odd_w_in=odd_w_in, odd_w_out=odd_w_out, gdn_conv=gdn_conv,
             gdn_a_log=gdn_a_log, gdn_dt_bias=gdn_dt_bias, gdn_norm=gdn_norm, mla_q_norm=mla_q_norm,
             mla_w_uq=mla_w_uq, mla_kv_norm=mla_kv_norm, mla_w_ukv=mla_w_ukv, peer_w_query=peer_w_query,
             peer_sub_keys=peer_sub_keys, peer_down_rows=tile_tables(peer_down), peer_up_rows=tile_tables(peer_up))
    batch, seq, d = x.shape
    depth = ada_w.shape[0]
    x2 = x.reshape(batch * seq, d)
    mod = modulation(c, ada_w, ada_b)
    for layer in range(depth):
        if layer % 2 == 0:
            x2 = even_layer_mixer(x2, mod[layer], p, layer, layer // 2, batch, seq)
        else:
            x2 = odd_layer_mixer(x2, mod[layer], p, layer, layer // 2, batch, seq)
        x2 = peer_layer(x2, mod[layer], p, layer, batch, seq)
    return final_rmsnorm(x2, final_norm).reshape(batch, seq, d)
```

```python
import functools
import math

import jax
import jax.numpy as jnp
import numpy as np
from jax import lax
from jax.experimental import pallas as pl
from jax.experimental.pallas import tpu as pltpu

F32 = jnp.float32
BF16 = jnp.bfloat16

CHUNK = 64
EPS = 1e-6
HEAD_DIM = 128
N_HEADS = 8
ROPE_THETA = 10000.0
MLA_Q_RANK = 384
MLA_KV_RANK = 256
MLA_ROPE = 64
N_KEYS = 128
PEER_HEADS = 8
PEER_TOPK = 16
PEER_SLOTS = PEER_HEADS * PEER_TOPK

LANES = 128
SUBLANES = 8
VMEM_LIMIT_BYTES = 56 * 1024 * 1024

N_EXPERT_TILES = 4
EXPERT_TILE = (N_KEYS * N_KEYS) // N_EXPERT_TILES
KEY_BITS = int(math.log2(N_KEYS))
TILE_BITS = int(math.log2(N_EXPERT_TILES))
SLOT_GROUP = 8

NEG_INF = float("-inf")


def _cparams(*sem):
    return pltpu.CompilerParams(dimension_semantics=sem, vmem_limit_bytes=VMEM_LIMIT_BYTES)


def _mod_kernel(c_ref, w_ref, b_ref, o_ref):
    c = c_ref[...]
    cond = c * jax.nn.sigmoid(c)
    o_ref[0] = jnp.dot(cond, w_ref[0], preferred_element_type=F32) + b_ref[0]


def modulation(c, ada_w, ada_b):
    depth, d, n = ada_w.shape
    b = c.shape[0]
    tn = 1536
    return pl.pallas_call(
        _mod_kernel,
        grid=(depth, n // tn),
        in_specs=[pl.BlockSpec((b, d), lambda l, j: (0, 0)),
                  pl.BlockSpec((1, d, tn), lambda l, j: (l, 0, j)),
                  pl.BlockSpec((1, 1, tn), lambda l, j: (l, 0, j))],
        out_specs=pl.BlockSpec((1, b, tn), lambda l, j: (l, 0, j)),
        out_shape=jax.ShapeDtypeStruct((depth, b, n), F32),
        compiler_params=_cparams("parallel", "parallel"),
        name="modulation",
    )(c, ada_w, ada_b.reshape(depth, 1, n))


def _norm_matmul_kernel(x_ref, g_ref, sc_ref, sh_ref, w_ref, *rest, emit_h):
    if emit_h:
        y_ref, h_ref, h_scr = rest
    else:
        y_ref, h_scr = rest

    @pl.when(pl.program_id(1) == 0)
    def _():
        x = x_ref[...]
        xn = x * lax.rsqrt(jnp.mean(x * x, axis=-1, keepdims=True) + EPS)
        h = xn * g_ref[...] * (1.0 + sc_ref[0]) + sh_ref[0]
        h_scr[...] = h.astype(BF16)
        if emit_h:
            h_ref[...] = h

    y_ref[...] = jnp.dot(h_scr[...], w_ref[...], preferred_element_type=F32)


def norm_matmul(x2, gain, scale, shift, w_bf16, seq, *, emit_h=False, tm=1024, tn=1024):
    t, d = x2.shape
    n = w_bf16.shape[1]
    tm = min(tm, seq)
    assert seq % tm == 0 and n % tn == 0
    per_seq = seq // tm
    bvec = lambda i, j: (i // per_seq, 0, 0)
    out_shape = [jax.ShapeDtypeStruct((t, n), F32)]
    out_specs = [pl.BlockSpec((tm, tn), lambda i, j: (i, j))]
    if emit_h:
        out_shape.append(jax.ShapeDtypeStruct((t, d), F32))
        out_specs.append(pl.BlockSpec((tm, d), lambda i, j: (i, 0)))
    res = pl.pallas_call(
        functools.partial(_norm_matmul_kernel, emit_h=emit_h),
        grid=(t // tm, n // tn),
        in_specs=[pl.BlockSpec((tm, d), lambda i, j: (i, 0)),
                  pl.BlockSpec((1, d), lambda i, j: (0, 0)),
                  pl.BlockSpec((1, 1, d), bvec),
                  pl.BlockSpec((1, 1, d), bvec),
                  pl.BlockSpec((d, tn), lambda i, j: (0, j))],
        out_specs=out_specs,
        out_shape=out_shape,
        scratch_shapes=[pltpu.VMEM((tm, d), BF16)],
        compiler_params=_cparams("parallel", "arbitrary"),
        name="norm_matmul",
    )(x2, gain.reshape(1, d), scale[:, None, :], shift[:, None, :], w_bf16)
    return res if emit_h else res[0]


def _out_proj_kernel(a1_ref, a2_ref, w1_ref, w2_ref, x_ref, gt_ref, o_ref):
    y = jnp.dot(a1_ref[...], w1_ref[...], preferred_element_type=F32)
    y = y + jnp.dot(a2_ref[...], w2_ref[...], preferred_element_type=F32)
    o_ref[...] = x_ref[...] + gt_ref[0] * y


def out_proj_residual(a1, a2, w_bf16, x2, gate, seq, *, tm=1024, tn=1024):
    t, k1 = a1.shape
    k2 = a2.shape[1]
    d = x2.shape[1]
    tm = min(tm, seq)
    per_seq = seq // tm
    return pl.pallas_call(
        _out_proj_kernel,
        grid=(t // tm, d // tn),
        in_specs=[pl.BlockSpec((tm, k1), lambda i, j: (i, 0)),
                  pl.BlockSpec((tm, k2), lambda i, j: (i, 0)),
                  pl.BlockSpec((k1, tn), lambda i, j: (0, j)),
                  pl.BlockSpec((k2, tn), lambda i, j: (0, j)),
                  pl.BlockSpec((tm, tn), lambda i, j: (i, j)),
                  pl.BlockSpec((1, 1, tn), lambda i, j: (i // per_seq, 0, j))],
        out_specs=pl.BlockSpec((tm, tn), lambda i, j: (i, j)),
        out_shape=jax.ShapeDtypeStruct((t, d), F32),
        compiler_params=_cparams("parallel", "parallel"),
        name="out_proj_residual",
    )(a1, a2, w_bf16[:k1], w_bf16[k1:], x2, gate[:, None, :])


def _dot_nt(a, b):
    return lax.dot_general(a, b, (((1,), (1,)), ((), ())), preferred_element_type=F32)


def _dot_tn(a, b):
    return lax.dot_general(a, b, (((0,), (0,)), ((), ())), preferred_element_type=F32)


def _retention_kernel(q_ref, k_ref, v_ref, g_ref, intra_ref, qdec_ref, kdec_ref, cdec_ref, norm_ref,
                      o_ref, state):
    @pl.when(pl.program_id(2) == 0)
    def _():
        state[...] = jnp.zeros_like(state)

    n_chunks = q_ref.shape[0] // CHUNK
    intra = intra_ref[0]
    qdec = qdec_ref[0]
    kdec = kdec_ref[0]
    cdec = cdec_ref[0]
    chunk_rows = [pl.ds(c * CHUNK, CHUNK) for c in range(n_chunks)]
    qs = [q_ref[rows, :] for rows in chunk_rows]
    ks = [k_ref[rows, :] * (HEAD_DIM ** -0.5) for rows in chunk_rows]
    vs = [v_ref[rows, :] for rows in chunk_rows]
    scores = [_dot_nt(qc, kc) * intra for qc, kc in zip(qs, ks)]
    updates = [_dot_tn(kc * kdec, vc) for kc, vc in zip(ks, vs)]
    states = [state[...]]
    for upd in updates:
        states.append(states[-1] * cdec + upd)
    state[...] = states[-1]
    for rows, qc, vc, s, st in zip(chunk_rows, qs, vs, scores, states):
        o = jnp.dot(s, vc, preferred_element_type=F32) + jnp.dot(qc * qdec, st, preferred_element_type=F32)
        on = o * lax.rsqrt(jnp.mean(o * o, axis=-1, keepdims=True) + EPS) * norm_ref[...]
        g = g_ref[rows, :]
        o_ref[rows, :] = (on * (g * jax.nn.sigmoid(g))).astype(o_ref.dtype)


def retention_mixer(proj, ret_norm, batch, seq, *, ts=512):
    t = proj.shape[0]
    ts = min(ts, seq)
    per_seq = seq // ts
    h8 = N_HEADS
    pos = jnp.arange(CHUNK, dtype=F32)
    lg = jnp.log1p(-jnp.exp2(-5.0 - jnp.arange(h8, dtype=F32)))[:, None]
    intra = jnp.exp(lg[..., None] * jnp.abs(pos[:, None] - pos[None, :]))
    qdec = jnp.broadcast_to(jnp.exp(lg * (pos + 1.0))[..., None], (h8, CHUNK, HEAD_DIM))
    kdec = jnp.broadcast_to(jnp.exp(lg * (CHUNK - 1.0 - pos))[..., None], (h8, CHUNK, HEAD_DIM))
    cdec = jnp.broadcast_to(jnp.exp(lg * CHUNK)[..., None], (h8, HEAD_DIM, HEAD_DIM))

    def col(off):
        return pl.BlockSpec((ts, HEAD_DIM), lambda b, h, s, off=off: (b * per_seq + s, off + h))

    hspec = lambda shape: pl.BlockSpec((1,) + shape, lambda b, h, s: (h, 0, 0))
    return pl.pallas_call(
        _retention_kernel,
        grid=(batch, h8, per_seq),
        in_specs=[col(0), col(h8), col(2 * h8), col(3 * h8),
                  hspec((CHUNK, CHUNK)), hspec((CHUNK, HEAD_DIM)), hspec((CHUNK, HEAD_DIM)),
                  hspec((HEAD_DIM, HEAD_DIM)),
                  pl.BlockSpec((1, HEAD_DIM), lambda b, h, s: (0, 0))],
        out_specs=pl.BlockSpec((ts, HEAD_DIM), lambda b, h, s: (b * per_seq + s, h)),
        out_shape=jax.ShapeDtypeStruct((t, h8 * HEAD_DIM), BF16),
        scratch_shapes=[pltpu.VMEM((HEAD_DIM, HEAD_DIM), F32)],
        compiler_params=_cparams("parallel", "parallel", "arbitrary"),
        name="retention",
    )(proj, proj, proj, proj, intra, qdec, kdec, cdec, ret_norm.reshape(1, HEAD_DIM))


ATTN_TQ = 256


def _diag_mask(tq):
    kc = lax.broadcasted_iota(jnp.int32, (tq, tq), 0) // CHUNK
    qc = lax.broadcasted_iota(jnp.int32, (tq, tq), 1) // CHUNK
    return kc <= qc


def _softmax_t(s_t, scale, mask):
    tq = mask.shape[0]
    ke = s_t.shape[0]
    y = s_t * (scale * math.log2(math.e))
    diag = jnp.where(mask, y[ke - tq:], NEG_INF)
    y = diag if ke == tq else jnp.concatenate([y[:ke - tq], diag], axis=0)
    e = jnp.exp2(y - jnp.max(y, axis=0, keepdims=True))
    return e, 1.0 / jnp.sum(e, axis=0, keepdims=True)


def _diff_attn_kernel(lam_ref, q_ref, k_ref, v_ref, norm_ref, o_ref, *, tq, lambda_init):
    seq = q_ref.shape[0]
    k = k_ref[...]
    half = lax.broadcasted_iota(jnp.int32, k.shape, 1) < (HEAD_DIM // 2)
    k_maps = (jnp.where(half, k, 0.0).astype(BF16), jnp.where(half, 0.0, k).astype(BF16))
    v_t = v_ref[...].T.astype(BF16)
    scale = (HEAD_DIM // 2) ** -0.5
    dl = lam_ref[...]
    lam = (jnp.exp(jnp.sum(dl[0:1] * dl[1:2], axis=-1, keepdims=True))
           - jnp.exp(jnp.sum(dl[2:3] * dl[3:4], axis=-1, keepdims=True)) + lambda_init)
    mask = _diag_mask(tq)
    blocks = [slice(qb * tq, (qb + 1) * tq) for qb in range(seq // tq)]
    qs = [q_ref[rows, :].astype(BF16) for rows in blocks]
    scores = [[_dot_nt(km[:rows.stop], q) for km in k_maps] for rows, q in zip(blocks, qs)]
    probs = [[_softmax_t(s_t, scale, mask) for s_t in pair] for pair in scores]
    for rows, ((e0, r0), (e1, r1)) in zip(blocks, probs):
        w_t = (e0 * r0 - e1 * (lam * r1)).astype(BF16)
        o = jnp.dot(v_t[:, :rows.stop], w_t, preferred_element_type=F32).T
        on = o * lax.rsqrt(jnp.mean(o * o, axis=-1, keepdims=True) + EPS) * norm_ref[...]
        o_ref[rows, :] = (on * (1.0 - lambda_init)).astype(o_ref.dtype)


def diff_attention_mixer(proj, diff_lambda, diff_norm, lambda_init, batch, seq):
    t = proj.shape[0]
    col = lambda off: pl.BlockSpec((seq, HEAD_DIM), lambda b, h, off=off: (b, off + h))
    return pl.pallas_call(
        functools.partial(_diff_attn_kernel, tq=min(ATTN_TQ, seq), lambda_init=lambda_init),
        grid=(batch, N_HEADS),
        in_specs=[pl.BlockSpec(diff_lambda.shape, lambda b, h: (0, 0)),
                  col(4 * N_HEADS), col(5 * N_HEADS), col(6 * N_HEADS),
                  pl.BlockSpec((1, HEAD_DIM), lambda b, h: (0, 0))],
        out_specs=pl.BlockSpec((seq, HEAD_DIM), lambda b, h: (b, h)),
        out_shape=jax.ShapeDtypeStruct((t, N_HEADS * HEAD_DIM), BF16),
        compiler_params=_cparams("parallel", "parallel"),
        name="diff_attention",
    )(diff_lambda, proj, proj, proj, diff_norm.reshape(1, HEAD_DIM))


def _split_mod(mod):
    return jnp.split(mod, 6, axis=-1)


def even_layer_mixer(x2, mod, p, layer, j, batch, seq):
    sh_m, sc_m, gt_m, _, _, _ = _split_mod(mod)
    proj = norm_matmul(x2, p["norm_mix"][layer], sc_m, sh_m, p["even_w_in"][j].astype(BF16), seq)
    ret = retention_mixer(proj, p["ret_norm"][j], batch, seq)
    lambda_init = 0.8 - 0.6 * math.exp(-0.3 * layer)
    dif = diff_attention_mixer(proj, p["diff_lambda"][j], p["diff_norm"][j], lambda_init, batch, seq)
    return out_proj_residual(ret, dif, p["even_w_out"][j].astype(BF16), x2, gt_m, seq)


ODD_COLS = 40 * LANES
GATE_BLOCK = 4 * N_HEADS


def _odd_w_in_layout(w):
    d = w.shape[0]
    hd = N_HEADS * HEAD_DIM
    zeros = lambda n: jnp.zeros((d, n), w.dtype)
    o = 4 * hd
    gates = w[:, o:o + 2 * N_HEADS]
    o += 2 * N_HEADS
    cq = w[:, o:o + MLA_Q_RANK]
    o += MLA_Q_RANK
    ckv = w[:, o:o + MLA_KV_RANK]
    o += MLA_KV_RANK
    kr = w[:, o:o + MLA_ROPE]
    return jnp.concatenate([w[:, :4 * hd], gates, zeros(LANES - 2 * N_HEADS), cq, ckv, kr,
                            zeros(LANES - MLA_ROPE), zeros(LANES)], axis=1)


def _silu(x):
    return x * jax.nn.sigmoid(x)


def _gdn_conv_kernel(x_ref, w_ref, o_ref):
    c = pl.program_id(1)
    x = x_ref[...]
    w = w_ref[...]
    width = w.shape[0]
    row = lax.broadcasted_iota(jnp.int32, x.shape, 0)
    y = x * w[width - 1:width]
    for sft in range(1, width):
        xs = jnp.where(row >= sft, pltpu.roll(x, sft, axis=0), 0.0)
        y = y + xs * w[width - 1 - sft:width - sft]
    y = _silu(y)
    yn = y * lax.rsqrt(jnp.sum(y * y, axis=-1, keepdims=True) + EPS)
    yn = yn * jnp.where(c < N_HEADS, HEAD_DIM ** -0.5, 1.0)
    o_ref[...] = jnp.where(c < 2 * N_HEADS, yn, y)


def gdn_conv(proj, conv_w, batch, seq):
    t = proj.shape[0]
    n_blk = 3 * N_HEADS
    return pl.pallas_call(
        _gdn_conv_kernel,
        grid=(batch, n_blk),
        in_specs=[pl.BlockSpec((seq, LANES), lambda b, c: (b, c)),
                  pl.BlockSpec((conv_w.shape[0], LANES), lambda b, c: (0, c))],
        out_specs=pl.BlockSpec((seq, LANES), lambda b, c: (b, c)),
        out_shape=jax.ShapeDtypeStruct((t, n_blk * LANES), F32),
        compiler_params=_cparams("parallel", "parallel"),
        name="gdn_conv",
    )(proj, conv_w)


def _dot_hi(a, b):
    def split(x):
        hi = x.astype(BF16)
        return hi, (x - hi.astype(F32)).astype(BF16)

    a_hi, a_lo = split(a)
    b_hi, b_lo = split(b)
    dot = functools.partial(jnp.dot, preferred_element_type=F32)
    return dot(a_hi, b_hi) + (dot(a_hi, b_lo) + dot(a_lo, b_hi))


def _unit_lower_inverses(l_mats):
    n = l_mats[0].shape[0]
    eye = (lax.broadcasted_iota(jnp.int32, (n, n), 0) == lax.broadcasted_iota(jnp.int32, (n, n), 1)).astype(F32)
    invs = [eye - l for l in l_mats]
    powers = [_dot_hi(l, l) for l in l_mats]
    span = 2
    while span < n:
        invs = [inv + _dot_hi(inv, p) for inv, p in zip(invs, powers)]
        span *= 2
        if span < n:
            powers = [_dot_hi(p, p) for p in powers]
    return invs


def _pick_lane(x, lane):
    sel = lax.broadcasted_iota(jnp.int32, x.shape, 1) == lane
    return jnp.sum(jnp.where(sel, x, 0.0), axis=-1, keepdims=True)


HEADS_PER_STEP = 2


def _gdn_kernel(q_ref, k_ref, v_ref, gg_ref, gate_ref, alog_ref, dtb_ref, norm_ref, o_ref, state):
    @pl.when(pl.program_id(2) == 0)
    def _():
        state[...] = jnp.zeros_like(state)

    n_chunks = q_ref.shape[0] // CHUNK
    ri = lax.broadcasted_iota(jnp.int32, (CHUNK, CHUNK), 0)
    ci = lax.broadcasted_iota(jnp.int32, (CHUNK, CHUNK), 1)
    tri = ci <= ri
    strict = ci < ri
    tri_f = tri.astype(F32)
    heads = [pl.program_id(1) * HEADS_PER_STEP + r for r in range(HEADS_PER_STEP)]
    a_coefs = [-jnp.exp(_pick_lane(alog_ref[...], h)) for h in heads]
    dt_biases = [_pick_lane(dtb_ref[...], h) for h in heads]
    units = [(pl.ds(c * CHUNK, CHUNK), slice(r * HEAD_DIM, (r + 1) * HEAD_DIM), r)
             for c in range(n_chunks) for r in range(HEADS_PER_STEP)]

    def local(rows, cols, r):
        k = k_ref[rows, cols]
        raw = gate_ref[rows, :]
        beta = jax.nn.sigmoid(_pick_lane(raw, heads[r]))
        ga = _pick_lane(raw, N_HEADS + heads[r]) + dt_biases[r]
        softplus = jnp.maximum(ga, 0.0) + jnp.log1p(jnp.exp(-jnp.abs(ga)))
        g = jnp.broadcast_to(a_coefs[r] * softplus, (CHUNK, HEAD_DIM))
        gc = _dot_hi(tri_f, g)
        g_rows = gc.T[:CHUNK, :]
        decay = jnp.where(tri, jnp.exp(jnp.where(tri, gc[:, :CHUNK] - g_rows, 0.0)), 0.0)
        k_beta = k * beta
        l_mat = jnp.where(strict, _dot_nt(k_beta, k) * decay, 0.0)
        return k, beta, gc, decay, k_beta, l_mat

    locs = [local(*unit) for unit in units]
    t_invs = _unit_lower_inverses([loc[-1] for loc in locs])
    preps = []
    for (rows, cols, _), (k, beta, gc, decay, k_beta, _), t_inv in zip(units, locs, t_invs):
        q = q_ref[rows, cols]
        eg = jnp.exp(gc)
        u = jnp.dot(t_inv, v_ref[rows, cols] * beta, preferred_element_type=F32)
        w = jnp.dot(t_inv, k_beta * eg, preferred_element_type=F32)
        g_last = gc[CHUNK - 1:CHUNK, :]
        preps.append((u, w, _dot_nt(q, k) * decay, q * eg, k * jnp.exp(g_last - gc), jnp.exp(g_last)))

    for (rows, cols, r), (u, w, intra, q_dec, k_dec, chunk_dec) in zip(units, preps):
        st = state[r]
        v_new = u - jnp.dot(w, st, preferred_element_type=F32)
        o = jnp.dot(q_dec, st, preferred_element_type=F32) + jnp.dot(intra, v_new, preferred_element_type=F32)
        state[r] = st * chunk_dec + _dot_tn(k_dec, v_new)
        on = o * lax.rsqrt(jnp.mean(o * o, axis=-1, keepdims=True) + EPS) * norm_ref[...]
        o_ref[rows, cols] = (on * _silu(gg_ref[rows, cols])).astype(o_ref.dtype)


def gdn_mixer(qkv, proj, a_log, dt_bias, gdn_norm, batch, seq, *, ts=512):
    t = qkv.shape[0]
    ts = min(ts, seq)
    per_seq = seq // ts
    hps = HEADS_PER_STEP
    pad = lambda v: jnp.pad(v.reshape(1, -1), ((0, 0), (0, LANES - v.shape[0])))

    def col(off):
        return pl.BlockSpec((ts, hps * HEAD_DIM), lambda b, h, s, off=off: (b * per_seq + s, off // hps + h))

    row1 = pl.BlockSpec((1, LANES), lambda b, h, s: (0, 0))
    return pl.pallas_call(
        _gdn_kernel,
        grid=(batch, N_HEADS // hps, per_seq),
        in_specs=[col(0), col(N_HEADS), col(2 * N_HEADS), col(3 * N_HEADS),
                  pl.BlockSpec((ts, LANES), lambda b, h, s: (b * per_seq + s, GATE_BLOCK)),
                  row1, row1, row1],
        out_specs=pl.BlockSpec((ts, hps * HEAD_DIM), lambda b, h, s: (b * per_seq + s, h)),
        out_shape=jax.ShapeDtypeStruct((t, N_HEADS * HEAD_DIM), BF16),
        scratch_shapes=[pltpu.VMEM((hps, HEAD_DIM, HEAD_DIM), F32)],
        compiler_params=_cparams("parallel", "parallel", "arbitrary"),
        name="gdn",
    )(qkv, qkv, qkv, proj, proj, pad(a_log), pad(dt_bias), gdn_norm.reshape(1, HEAD_DIM))


def _rope_tables(seq):
    half = MLA_ROPE // 2
    inv_freq = ROPE_THETA ** (-jnp.arange(half, dtype=F32) / half)
    ang = jnp.arange(seq, dtype=jnp.int32).astype(F32)[:, None] * inv_freq[None, :]
    cos, sin = jnp.cos(ang), jnp.sin(ang)
    z = jnp.zeros((seq, LANES - MLA_ROPE), F32)
    zh = jnp.zeros((seq, half), F32)
    return (jnp.concatenate([cos, cos, z], axis=1),
            jnp.concatenate([zh, sin, z], axis=1),
            jnp.concatenate([-sin, zh, z], axis=1))


def _rope(y, cos, sin_up, sin_dn):
    half = MLA_ROPE // 2
    return y * cos + pltpu.roll(y, half, axis=1) * sin_up + pltpu.roll(y, LANES - half, axis=1) * sin_dn


def _mla_proj_kernel(cq_ref, ckv_ref, kr_ref, qg_ref, kvg_ref, wqn_ref, wqr_ref, wkv_ref, cos_ref, sup_ref, sdn_ref,
                     qn_ref, qr_ref, kv_ref, kro_ref):
    def rms(x, g):
        return x * lax.rsqrt(jnp.mean(x * x, axis=-1, keepdims=True) + EPS) * g

    cos, sup, sdn = cos_ref[...], sup_ref[...], sdn_ref[...]
    cq = rms(cq_ref[...], qg_ref[...]).astype(BF16)
    qn_ref[...] = jnp.dot(cq, wqn_ref[...], preferred_element_type=F32).astype(qn_ref.dtype)
    qr = jnp.dot(cq, wqr_ref[...], preferred_element_type=F32)
    for h in range(N_HEADS):
        cols = slice(h * LANES, (h + 1) * LANES)
        qr_ref[:, cols] = _rope(qr[:, cols], cos, sup, sdn).astype(qr_ref.dtype)
    ckv = rms(ckv_ref[...], kvg_ref[...]).astype(BF16)
    kv_ref[...] = jnp.dot(ckv, wkv_ref[...], preferred_element_type=F32).astype(kv_ref.dtype)
    kro_ref[...] = _rope(kr_ref[...], cos, sup, sdn).astype(kro_ref.dtype)


def mla_projection(proj, q_norm, w_uq, kv_norm, w_ukv, seq, *, tm=512):
    t = proj.shape[0]
    tm = min(tm, seq)
    per_seq = seq // tm
    hd = N_HEADS * HEAD_DIM
    w3 = w_uq.reshape(MLA_Q_RANK, N_HEADS, HEAD_DIM + MLA_ROPE)
    wqn = w3[:, :, :HEAD_DIM].reshape(MLA_Q_RANK, hd).astype(BF16)
    wqr = jnp.pad(w3[:, :, HEAD_DIM:], ((0, 0), (0, 0), (0, LANES - MLA_ROPE))).reshape(MLA_Q_RANK, hd).astype(BF16)
    cos, sup, sdn = _rope_tables(seq)
    full = lambda a: pl.BlockSpec(a.shape, lambda i: (0,) * a.ndim)
    tab = pl.BlockSpec((tm, LANES), lambda i: (i % per_seq, 0))
    wkv = w_ukv.astype(BF16)
    qg, kvg = q_norm.reshape(1, -1), kv_norm.reshape(1, -1)
    return pl.pallas_call(
        _mla_proj_kernel,
        grid=(t // tm,),
        in_specs=[pl.BlockSpec((tm, MLA_Q_RANK), lambda i: (i, (GATE_BLOCK + 1) * LANES // MLA_Q_RANK)),
                  pl.BlockSpec((tm, MLA_KV_RANK), lambda i: (i, (GATE_BLOCK + 4) * LANES // MLA_KV_RANK)),
                  pl.BlockSpec((tm, LANES), lambda i: (i, GATE_BLOCK + 6)),
                  full(qg), full(kvg), full(wqn), full(wqr), full(wkv), tab, tab, tab],
        out_specs=[pl.BlockSpec((tm, hd), lambda i: (i, 0)), pl.BlockSpec((tm, hd), lambda i: (i, 0)),
                   pl.BlockSpec((tm, 2 * hd), lambda i: (i, 0)), pl.BlockSpec((tm, LANES), lambda i: (i, 0))],
        out_shape=[jax.ShapeDtypeStruct((t, hd), BF16), jax.ShapeDtypeStruct((t, hd), BF16),
                   jax.ShapeDtypeStruct((t, 2 * hd), BF16), jax.ShapeDtypeStruct((t, LANES), BF16)],
        compiler_params=_cparams("parallel"),
        name="mla_projection",
    )(proj, proj, proj, qg, kvg, wqn, wqr, wkv, cos, sup, sdn)


def _mla_attn_kernel(qn_ref, qr_ref, kn_ref, kr_ref, v_ref, o_ref, *, tq):
    seq = qn_ref.shape[0]
    kn = kn_ref[...]
    kr = kr_ref[...]
    v_t = v_ref[...].T
    scale = (HEAD_DIM + MLA_ROPE) ** -0.5
    mask = _diag_mask(tq)
    blocks = [slice(qb * tq, (qb + 1) * tq) for qb in range(seq // tq)]
    scores = [_dot_nt(kn[:rows.stop], qn_ref[rows, :]) + _dot_nt(kr[:rows.stop], qr_ref[rows, :]) for rows in blocks]
    probs = [_softmax_t(s_t, scale, mask) for s_t in scores]
    for rows, (e, r) in zip(blocks, probs):
        o_t = jnp.dot(v_t[:, :rows.stop], e.astype(BF16), preferred_element_type=F32) * r
        o_ref[rows, :] = o_t.T.astype(o_ref.dtype)


def mla_attention_mixer(qn, qr, kv, kr, batch, seq):
    t = qn.shape[0]
    qspec = pl.BlockSpec((seq, HEAD_DIM), lambda b, h: (b, h))
    return pl.pallas_call(
        functools.partial(_mla_attn_kernel, tq=min(ATTN_TQ, seq)),
        grid=(batch, N_HEADS),
        in_specs=[qspec, qspec,
                  pl.BlockSpec((seq, HEAD_DIM), lambda b, h: (b, 2 * h)),
                  pl.BlockSpec((seq, LANES), lambda b, h: (b, 0)),
                  pl.BlockSpec((seq, HEAD_DIM), lambda b, h: (b, 2 * h + 1))],
        out_specs=qspec,
        out_shape=jax.ShapeDtypeStruct((t, N_HEADS * HEAD_DIM), BF16),
        compiler_params=_cparams("parallel", "parallel"),
        name="mla_attention",
    )(qn, qr, kv, kr, kv)


def odd_layer_mixer(x2, mod, p, layer, j, batch, seq):
    sh_m, sc_m, gt_m, _, _, _ = _split_mod(mod)
    w_in = _odd_w_in_layout(p["odd_w_in"][j]).astype(BF16)
    proj = norm_matmul(x2, p["norm_mix"][layer], sc_m, sh_m, w_in, seq)
    qkv = gdn_conv(proj, p["gdn_conv"][j], batch, seq)
    gdn = gdn_mixer(qkv, proj, p["gdn_a_log"][j], p["gdn_dt_bias"][j], p["gdn_norm"][j], batch, seq)
    qn, qr, kv, kr = mla_projection(proj, p["mla_q_norm"][j], p["mla_w_uq"][j], p["mla_kv_norm"][j],
                                    p["mla_w_ukv"][j], seq)
    mla = mla_attention_mixer(qn, qr, kv, kr, batch, seq)
    return out_proj_residual(gdn, mla, p["odd_w_out"][j].astype(BF16), x2, gt_m, seq)


ROUTE_TOKENS = 128
ROUTE_HEADS_PER_STEP = 4


def _top16(*problems):
    n, tt = problems[0].shape
    row = lax.broadcasted_iota(jnp.int32, (n, tt), 0).astype(F32)
    r16 = lax.broadcasted_iota(jnp.int32, (PEER_TOPK, tt), 0)

    def body(k, carry):
        hit = r16 == k
        out = []
        for s, vals, ids in carry:
            m = jnp.max(s, axis=0, keepdims=True)
            i = jnp.min(jnp.where(s == m, row, float(n)), axis=0, keepdims=True)
            out.append((jnp.where(row == i, NEG_INF, s), jnp.where(hit, m, vals), jnp.where(hit, i, ids)))
        return tuple(out)

    zeros = jnp.zeros((PEER_TOPK, tt), F32)
    res = lax.fori_loop(0, PEER_TOPK, body, tuple((s, zeros, zeros) for s in problems))
    res = [(vals, ids) for _, vals, ids in res]
    return res[0] if len(problems) == 1 else res


def _pruned_candidates(v1, v2):
    tt = v1.shape[1]
    sub = lax.broadcasted_iota(jnp.int32, (SUBLANES, tt), 0)
    row = lambda v, r: v[r:r + 1, :]
    blocks = [row(v1, 0) + v2[:SUBLANES], row(v1, 0) + v2[SUBLANES:]]
    blocks += [row(v1, a) + v2[:SUBLANES] for a in range(1, 5)]
    a567 = jnp.where(sub < 2, row(v1, 5), jnp.where(sub < 4, row(v1, 6), row(v1, 7)))
    b01 = jnp.where((sub & 1) == 0, row(v2, 0), row(v2, 1))
    blocks.append(jnp.where(sub < 6, a567 + b01, NEG_INF))
    blocks.append(v1[SUBLANES:] + row(v2, 0))
    return jnp.concatenate(blocks, axis=0)


def _candidate_ranks(pos):
    p16 = pos - 16.0
    p48 = pos - 48.0
    a_mid = 1.0 + jnp.floor(p16 * 0.125)
    a_hi = 5.0 + jnp.floor(p48 * 0.5)
    a = jnp.where(pos < 16.0, 0.0, jnp.where(pos < 48.0, a_mid, jnp.where(pos < 56.0, a_hi, p48)))
    b = jnp.where(pos < 16.0, pos, jnp.where(pos < 48.0, p16 - 8.0 * (a_mid - 1.0),
                                             jnp.where(pos < 56.0, p48 - 2.0 * (a_hi - 5.0), 0.0)))
    return a, b


def _select_rows(table, sel):
    out = jnp.zeros_like(sel)
    for a in range(table.shape[0]):
        out = out + jnp.where(sel == float(a), table[a:a + 1, :], 0.0)
    return out


def _route_kernel(q_ref, keys_ref, idx_ref, gate_ref, cnt_ref, idx_all, gate_all, *, rows_per_expert):
    keys = [keys_ref[half].astype(BF16) for half in range(2)]

    def head_body(it, _):
        heads = [it * ROUTE_HEADS_PER_STEP + r for r in range(ROUTE_HEADS_PER_STEP)]
        tops = []
        for hd in heads:
            scores = []
            for half in range(2):
                col = pl.multiple_of((2 * hd + half) * LANES, LANES)
                scores.append(_dot_nt(keys[half], q_ref[:, pl.ds(col, LANES)].astype(BF16)))
            tops.append(_top16(*scores))
        finals = _top16(*[_pruned_candidates(v1, v2) for (v1, _), (v2, _) in tops])
        if ROUTE_HEADS_PER_STEP == 1:
            finals = [finals]
        for hd, ((_, i1), (_, i2)), (best, pos) in zip(heads, tops, finals):
            a_sel, b_sel = _candidate_ranks(pos)
            expert = _select_rows(i1, a_sel) * N_KEYS + _select_rows(i2, b_sel)
            e = jnp.exp(best - jnp.max(best, axis=0, keepdims=True))
            rows = pl.ds(pl.multiple_of(hd * PEER_TOPK, PEER_TOPK), PEER_TOPK)
            idx_all[rows, :] = expert.astype(jnp.int32)
            gate_all[rows, :] = e / jnp.sum(e, axis=0, keepdims=True)
        return 0

    lax.fori_loop(0, PEER_HEADS // ROUTE_HEADS_PER_STEP, head_body, 0)

    idx = idx_all[...]
    gate = gate_all[...]
    key1 = idx >> KEY_BITS
    key2 = idx & (N_KEYS - 1)
    per_bits = KEY_BITS - TILE_BITS
    tid = (key1 + (key2 >> per_bits)) & (N_EXPERT_TILES - 1)
    local = ((key1 << per_bits) + (key2 & ((1 << per_bits) - 1))) * rows_per_expert
    n = PEER_SLOTS
    before = (lax.broadcasted_iota(jnp.int32, (n, n), 1) < lax.broadcasted_iota(jnp.int32, (n, n), 0)).astype(F32)
    for k in range(N_EXPERT_TILES):
        mem = tid == k
        shift = jnp.dot(before, jnp.where(mem, 0.0, 1.0), preferred_element_type=F32).astype(jnp.int32)
        xi = jnp.where(mem, local, 0)
        xg = jnp.where(mem, gate, 0.0)
        xd = jnp.where(mem, shift, 0)
        for b in range(int(math.log2(n))):
            step = 1 << b
            mv = ((xd >> b) & 1) == 1
            pull = lambda a: pltpu.roll(a, n - step, axis=0)
            inc = pull(mv.astype(jnp.int32)) == 1
            xi = jnp.where(inc, pull(xi), jnp.where(mv, 0, xi))
            xg = jnp.where(inc, pull(xg), jnp.where(mv, 0.0, xg))
            xd = jnp.where(inc, pull(xd), jnp.where(mv, 0, xd))
        idx_ref[k] = xi.T
        gate_ref[k * n:(k + 1) * n, :] = xg
        cnt_ref[k:k + 1, :] = jnp.sum(mem.astype(jnp.int32), axis=0, keepdims=True)


def peer_route(q, sub_keys, rows_per_expert):
    t, d = q.shape
    tt = ROUTE_TOKENS
    n = N_EXPERT_TILES * PEER_SLOTS
    return pl.pallas_call(
        functools.partial(_route_kernel, rows_per_expert=rows_per_expert),
        grid=(t // tt,),
        in_specs=[pl.BlockSpec((tt, d), lambda i: (i, 0)),
                  pl.BlockSpec(sub_keys.shape, lambda i: (0, 0, 0))],
        out_specs=[pl.BlockSpec((N_EXPERT_TILES, tt, PEER_SLOTS), lambda i: (0, i, 0)),
                   pl.BlockSpec((n, tt), lambda i: (0, i)),
                   pl.BlockSpec((N_EXPERT_TILES, tt), lambda i: (0, i))],
        out_shape=[jax.ShapeDtypeStruct((N_EXPERT_TILES, t, PEER_SLOTS), jnp.int32),
                   jax.ShapeDtypeStruct((n, t), F32),
                   jax.ShapeDtypeStruct((N_EXPERT_TILES, t), jnp.int32)],
        scratch_shapes=[pltpu.VMEM((PEER_SLOTS, tt), jnp.int32), pltpu.VMEM((PEER_SLOTS, tt), F32)],
        compiler_params=_cparams("parallel"),
        name="peer_route",
    )(q, sub_keys)


EXPERT_TOKENS = 128


def _n_groups(cnt):
    return lax.shift_right_logical(cnt + (SLOT_GROUP - 1), int(math.log2(SLOT_GROUP)))


STATIC_SLOTS = 40
STATIC_GROUPS = STATIC_SLOTS // SLOT_GROUP
FOLD_ORDER = (0, 4, 2, 6, 1, 5, 3, 7)
TOKEN_UNROLL = 4


def _rows_of(i, n):
    return pl.ds(pl.multiple_of(i * n, n), n)


def _fold_group(parts):
    sub = lax.broadcasted_iota(jnp.int32, parts[0].shape, 0)
    xs = [parts[i] for i in FOLD_ORDER]
    half = SUBLANES // 2
    while half >= 1:
        keep = (sub & half) == 0
        xs = [jnp.where(keep, x, pltpu.roll(y, half, axis=0)) + jnp.where(keep, pltpu.roll(x, SUBLANES - half, axis=0), y)
              for x, y in zip(xs[0::2], xs[1::2])]
        half //= 2
    return xs[0]


def _down_kernel(idx_ref, cnt_ref, h_ref, gate_ref, tbl_ref, coef_ref, act_ref):
    tb = act_ref.shape[1]
    rpe = h_ref.shape[0] // tb
    lane = lax.broadcasted_iota(jnp.int32, (SUBLANES, tb), 1)

    def partial(off, hv):
        p = tbl_ref[pl.ds(pl.multiple_of(off, rpe), rpe), :] * hv
        out = p[:SUBLANES]
        for c in range(1, rpe // SUBLANES):
            out = out + p[c * SUBLANES:(c + 1) * SUBLANES]
        return out

    def group(t, g, hv):
        return _fold_group([partial(idx_ref[t, g * SLOT_GROUP + u], hv) for u in range(SLOT_GROUP)])

    act_ref[STATIC_SLOTS:, :] = jnp.zeros((PEER_SLOTS - STATIC_SLOTS, tb), F32)

    def overflow(t, hv):
        n = cnt_ref[0, 0, t]

        @pl.when(n > STATIC_SLOTS)
        def _():
            def group_body(g, _):
                rows = _rows_of(g, SLOT_GROUP)
                col = jnp.sum(group(t, g, hv), axis=-1, keepdims=True)
                act_ref[rows, :] = jnp.where(lane == t, col, act_ref[rows, :])
                return 0

            lax.fori_loop(STATIC_GROUPS, _n_groups(n), group_body, 0)

    def absorb(accs, folds, first_token):
        accs = list(accs)
        for r in range(TOKEN_UNROLL):
            for g in range(STATIC_GROUPS):
                col = jnp.sum(folds[r * STATIC_GROUPS + g], axis=-1, keepdims=True)
                accs[g] = jnp.where(lane == first_token + r, col, accs[g])
        return tuple(accs)

    def tok_body(i, carry):
        accs, folds = carry
        accs = absorb(accs, folds, (i - 1) * TOKEN_UNROLL)
        tokens = [i * TOKEN_UNROLL + r for r in range(TOKEN_UNROLL)]
        hvs = [h_ref[_rows_of(t, rpe), :] for t in tokens]
        folds = tuple(group(t, g, hv) for t, hv in zip(tokens, hvs) for g in range(STATIC_GROUPS))
        for t, hv in zip(tokens, hvs):
            overflow(t, hv)
        return accs, folds

    assert tb == LANES
    zeros = act_ref[STATIC_SLOTS:STATIC_SLOTS + SUBLANES, :]
    n_iter = tb // TOKEN_UNROLL
    init = ((zeros,) * STATIC_GROUPS, (zeros,) * (TOKEN_UNROLL * STATIC_GROUPS))
    accs, folds = lax.fori_loop(0, n_iter, tok_body, init)
    accs = absorb(accs, folds, (n_iter - 1) * TOKEN_UNROLL)
    a = jnp.concatenate(list(accs) + [act_ref[STATIC_SLOTS:, :]], axis=0)
    coef = gate_ref[...] * (0.5 * a * (1.0 + lax.erf(a * (2.0 ** -0.5))))
    coef_ref[...] = coef.T


def _smem_spec(block, index_map):
    return pl.BlockSpec(block, index_map, memory_space=pltpu.SMEM)


def _tile_spec(rpe, index_map):
    return pl.BlockSpec((EXPERT_TILE * rpe, LANES), index_map, pipeline_mode=pl.Buffered(1))


def peer_down_coefs(idx, cnt, h2, gate, table2):
    rpe = table2.shape[0] // (N_KEYS * N_KEYS)
    t = h2.shape[0] // rpe
    tb = EXPERT_TOKENS
    n_blk = t // tb
    return pl.pallas_call(
        _down_kernel,
        grid=(N_EXPERT_TILES, n_blk),
        in_specs=[_smem_spec((tb, PEER_SLOTS), lambda k, i: (k * n_blk + i, 0)),
                  _smem_spec((1, 1, tb), lambda k, i: (k, 0, i)),
                  pl.BlockSpec((tb * rpe, LANES), lambda k, i: (i, 0)),
                  pl.BlockSpec((PEER_SLOTS, tb), lambda k, i: (k, i)),
                  _tile_spec(rpe, lambda k, i: (k, 0))],
        out_specs=pl.BlockSpec((tb, PEER_SLOTS), lambda k, i: (k * n_blk + i, 0)),
        out_shape=jax.ShapeDtypeStruct((N_EXPERT_TILES * t, PEER_SLOTS), F32),
        scratch_shapes=[pltpu.VMEM((PEER_SLOTS, tb), F32)],
        compiler_params=_cparams("arbitrary", "arbitrary"),
        name="peer_down",
    )(idx, cnt, h2, gate, table2)


def _up_kernel(idx_ref, coef_ref, cnt_ref, x_ref, gt_ref, tbl_ref, o_ref):
    tb = cnt_ref.shape[-1]
    rpe = x_ref.shape[0] // tb
    gt = gt_ref[...]

    def term(t, j):
        return coef_ref[t, j] * tbl_ref[pl.ds(pl.multiple_of(idx_ref[t, j], rpe), rpe), :]

    def token_body(t, _):
        accs = [jnp.zeros((rpe, LANES), F32), jnp.zeros((rpe, LANES), F32)]
        for j in range(STATIC_SLOTS):
            accs[j % 2] = accs[j % 2] + term(t, j)

        def group_body(g, acc):
            for u in range(SLOT_GROUP):
                acc = acc + term(t, g * SLOT_GROUP + u)
            return acc

        n_groups = jnp.maximum(_n_groups(cnt_ref[0, 0, t]), STATIC_GROUPS)
        acc = lax.fori_loop(STATIC_GROUPS, n_groups, group_body, accs[0] + accs[1])
        rows = _rows_of(t, rpe)
        o_ref[rows, :] = x_ref[rows, :] + gt * acc
        return 0

    lax.fori_loop(0, tb, token_body, 0)


def peer_up_tile(k, idx, coef, cnt, x2r, gate2r, table2, seq):
    rpe = table2.shape[0] // (N_KEYS * N_KEYS)
    t = x2r.shape[0] // rpe
    tb = min(EXPERT_TOKENS, seq)
    per_seq = seq // tb
    n_blk = t // tb
    return pl.pallas_call(
        _up_kernel,
        grid=(n_blk,),
        in_specs=[_smem_spec((tb, PEER_SLOTS), lambda i: (k * n_blk + i, 0)),
                  _smem_spec((tb, PEER_SLOTS), lambda i: (k * n_blk + i, 0)),
                  _smem_spec((1, 1, tb), lambda i: (k, 0, i)),
                  pl.BlockSpec((tb * rpe, LANES), lambda i: (i, 0)),
                  pl.BlockSpec((rpe, LANES), lambda i: (i // per_seq, 0)),
                  _tile_spec(rpe, lambda i: (k, 0))],
        out_specs=pl.BlockSpec((tb * rpe, LANES), lambda i: (i, 0)),
        out_shape=jax.ShapeDtypeStruct(x2r.shape, F32),
        compiler_params=_cparams("arbitrary"),
        name="peer_up",
    )(idx, coef, cnt, x2r, gate2r, table2)


def _tile_tables_kernel(x_ref, o_ref):
    nt = N_EXPERT_TILES
    per = N_KEYS // nt
    rpe = x_ref.shape[1] // LANES
    key1 = pl.program_id(1)
    for r in range(nt):
        tile = (key1 + r) % nt
        for c in range(rpe):
            o_ref[tile, pl.ds(c, per, stride=rpe), :] = x_ref[r * per:(r + 1) * per, c * LANES:(c + 1) * LANES]


def tile_tables(tables):
    depth, e, d = tables.shape
    nt = N_EXPERT_TILES
    per = N_KEYS // nt
    rpe = d // LANES
    out = pl.pallas_call(
        _tile_tables_kernel,
        grid=(depth, N_KEYS),
        in_specs=[pl.BlockSpec((None, N_KEYS, d), lambda l, k1: (l, k1, 0))],
        out_specs=pl.BlockSpec((None, nt, None, per * rpe, LANES), lambda l, k1: (l, 0, k1, 0, 0)),
        out_shape=jax.ShapeDtypeStruct((depth, nt, N_KEYS, per * rpe, LANES), tables.dtype),
        compiler_params=_cparams("parallel", "parallel"),
        name="tile_tables",
    )(tables)
    return out.reshape(depth, e * rpe, LANES)


def peer_layer(x2, mod, p, layer, batch, seq):
    _, _, _, sh_f, sc_f, gt_f = _split_mod(mod)
    t, d = x2.shape
    rpe = d // LANES
    q, h = norm_matmul(x2, p["norm_ffn"][layer], sc_f, sh_f, p["peer_w_query"][layer].astype(BF16), seq,
                       emit_h=True, tn=512)
    idx, gate, cnt = peer_route(q, p["peer_sub_keys"][layer], rpe)
    idx = idx.reshape(N_EXPERT_TILES * t, PEER_SLOTS)
    cnt = cnt.reshape(N_EXPERT_TILES, 1, t)
    coef = peer_down_coefs(idx, cnt, h.reshape(t * rpe, LANES), gate, p["peer_down_rows"][layer])
    up2 = p["peer_up_rows"][layer]
    xr = x2.reshape(t * rpe, LANES)
    gtr = gt_f.reshape(batch * rpe, LANES)
    for k in range(N_EXPERT_TILES):
        xr = peer_up_tile(k, idx, coef, cnt, xr, gtr, up2, seq)
    return xr.reshape(t, d)


def _final_norm_kernel(x_ref, g_ref, o_ref):
    x = x_ref[...]
    o_ref[...] = x * lax.rsqrt(jnp.mean(x * x, axis=-1, keepdims=True) + EPS) * g_ref[...]


def final_rmsnorm(x2, gain, *, tm=512):
    t, d = x2.shape
    return pl.pallas_call(
        _final_norm_kernel,
        grid=(t // tm,),
        in_specs=[pl.BlockSpec((tm, d), lambda i: (i, 0)), pl.BlockSpec((1, d), lambda i: (0, 0))],
        out_specs=pl.BlockSpec((tm, d), lambda i: (i, 0)),
        out_shape=jax.ShapeDtypeStruct((t, d), F32),
        compiler_params=_cparams("parallel"),
        name="final_norm",
    )(x2, gain.reshape(1, d))


def kernel(x, c, norm_mix, norm_ffn, ada_w, ada_b, even_w_in, even_w_out, ret_norm, diff_lambda, diff_norm, odd_w_in, odd_w_out, gdn_conv, gdn_a_log, gdn_dt_bias, gdn_norm, mla_q_norm, mla_w_uq, mla_kv_norm, mla_w_ukv, peer_w_query, peer_sub_keys, peer_down, peer_up, final_norm):
    p = dict(norm_mix=norm_mix, norm_ffn=norm_ffn, even_w_in=even_w_in, even_w_out=even_w_out, ret_norm=ret_norm,
             diff_lambda=diff_lambda, diff_norm=diff_norm, odd_w_in=odd_w_in, odd_w_out=odd_w_out, gdn_conv=gdn_conv,
             gdn_a_log=gdn_a_log, gdn_dt_bias=gdn_dt_bias, gdn_norm=gdn_norm, mla_q_norm=mla_q_norm,
             mla_w_uq=mla_w_uq, mla_kv_norm=mla_kv_norm, mla_w_ukv=mla_w_ukv, peer_w_query=peer_w_query,
             peer_sub_keys=peer_sub_keys, peer_down_rows=tile_tables(peer_down), peer_up_rows=tile_tables(peer_up))
    batch, seq, d = x.shape
    depth = ada_w.shape[0]
    x2 = x.reshape(batch * seq, d)
    mod = modulation(c, ada_w, ada_b)
    for layer in range(depth):
        if layer % 2 == 0:
            x2 = even_layer_mixer(x2, mod[layer], p, layer, layer // 2, batch, seq)
        else:
            x2 = odd_layer_mixer(x2, mod[layer], p, layer, layer // 2, batch, seq)
        x2 = peer_layer(x2, mod[layer], p, layer, batch, seq)
    return final_rmsnorm(x2, final_norm).reshape(batch, seq, d)
```

```python
import functools
import math

import jax
import jax.numpy as jnp
import numpy as np
from jax import lax
from jax.experimental import pallas as pl
from jax.experimental.pallas import tpu as pltpu

F32 = jnp.float32
BF16 = jnp.bfloat16

CHUNK = 64
EPS = 1e-6
HEAD_DIM = 128
N_HEADS = 8
ROPE_THETA = 10000.0
MLA_Q_RANK = 384
MLA_KV_RANK = 256
MLA_ROPE = 64
N_KEYS = 128
PEER_HEADS = 8
PEER_TOPK = 16
PEER_SLOTS = PEER_HEADS * PEER_TOPK

LANES = 128
SUBLANES = 8
VMEM_LIMIT_BYTES = 56 * 1024 * 1024

N_EXPERT_TILES = 4
EXPERT_TILE = (N_KEYS * N_KEYS) // N_EXPERT_TILES
KEY_BITS = int(math.log2(N_KEYS))
TILE_BITS = int(math.log2(N_EXPERT_TILES))
SLOT_GROUP = 8

NEG_INF = float("-inf")


def _cparams(*sem):
    return pltpu.CompilerParams(dimension_semantics=sem, vmem_limit_bytes=VMEM_LIMIT_BYTES)


def _mod_kernel(c_ref, w_ref, b_ref, o_ref):
    c = c_ref[...]
    cond = c * jax.nn.sigmoid(c)
    o_ref[0] = jnp.dot(cond, w_ref[0], preferred_element_type=F32) + b_ref[0]


def modulation(c, ada_w, ada_b):
    depth, d, n = ada_w.shape
    b = c.shape[0]
    tn = 1536
    return pl.pallas_call(
        _mod_kernel,
        grid=(depth, n // tn),
        in_specs=[pl.BlockSpec((b, d), lambda l, j: (0, 0)),
                  pl.BlockSpec((1, d, tn), lambda l, j: (l, 0, j)),
                  pl.BlockSpec((1, 1, tn), lambda l, j: (l, 0, j))],
        out_specs=pl.BlockSpec((1, b, tn), lambda l, j: (l, 0, j)),
        out_shape=jax.ShapeDtypeStruct((depth, b, n), F32),
        compiler_params=_cparams("parallel", "parallel"),
        name="modulation",
    )(c, ada_w, ada_b.reshape(depth, 1, n))


def _norm_matmul_kernel(x_ref, g_ref, sc_ref, sh_ref, w_ref, *rest, emit_h):
    if emit_h:
        y_ref, h_ref, h_scr = rest
    else:
        y_ref, h_scr = rest

    @pl.when(pl.program_id(1) == 0)
    def _():
        x = x_ref[...]
        xn = x * lax.rsqrt(jnp.mean(x * x, axis=-1, keepdims=True) + EPS)
        h = xn * g_ref[...] * (1.0 + sc_ref[0]) + sh_ref[0]
        h_scr[...] = h.astype(BF16)
        if emit_h:
            h_ref[...] = h

    y_ref[...] = jnp.dot(h_scr[...], w_ref[...], preferred_element_type=F32)


def norm_matmul(x2, gain, scale, shift, w_bf16, seq, *, emit_h=False, tm=1024, tn=1024):
    t, d = x2.shape
    n = w_bf16.shape[1]
    tm = min(tm, seq)
    assert seq % tm == 0 and n % tn == 0
    per_seq = seq // tm
    bvec = lambda i, j: (i // per_seq, 0, 0)
    out_shape = [jax.ShapeDtypeStruct((t, n), F32)]
    out_specs = [pl.BlockSpec((tm, tn), lambda i, j: (i, j))]
    if emit_h:
        out_shape.append(jax.ShapeDtypeStruct((t, d), F32))
        out_specs.append(pl.BlockSpec((tm, d), lambda i, j: (i, 0)))
    res = pl.pallas_call(
        functools.partial(_norm_matmul_kernel, emit_h=emit_h),
        grid=(t // tm, n // tn),
        in_specs=[pl.BlockSpec((tm, d), lambda i, j: (i, 0)),
                  pl.BlockSpec((1, d), lambda i, j: (0, 0)),
                  pl.BlockSpec((1, 1, d), bvec),
                  pl.BlockSpec((1, 1, d), bvec),
                  pl.BlockSpec((d, tn), lambda i, j: (0, j))],
        out_specs=out_specs,
        out_shape=out_shape,
        scratch_shapes=[pltpu.VMEM((tm, d), BF16)],
        compiler_params=_cparams("parallel", "arbitrary"),
        name="norm_matmul",
    )(x2, gain.reshape(1, d), scale[:, None, :], shift[:, None, :], w_bf16)
    return res if emit_h else res[0]


def _out_proj_kernel(a1_ref, a2_ref, w1_ref, w2_ref, x_ref, gt_ref, o_ref):
    y = jnp.dot(a1_ref[...], w1_ref[...], preferred_element_type=F32)
    y = y + jnp.dot(a2_ref[...], w2_ref[...], preferred_element_type=F32)
    o_ref[...] = x_ref[...] + gt_ref[0] * y


def out_proj_residual(a1, a2, w_bf16, x2, gate, seq, *, tm=1024, tn=1024):
    t, k1 = a1.shape
    k2 = a2.shape[1]
    d = x2.shape[1]
    tm = min(tm, seq)
    per_seq = seq // tm
    return pl.pallas_call(
        _out_proj_kernel,
        grid=(t // tm, d // tn),
        in_specs=[pl.BlockSpec((tm, k1), lambda i, j: (i, 0)),
                  pl.BlockSpec((tm, k2), lambda i, j: (i, 0)),
                  pl.BlockSpec((k1, tn), lambda i, j: (0, j)),
                  pl.BlockSpec((k2, tn), lambda i, j: (0, j)),
                  pl.BlockSpec((tm, tn), lambda i, j: (i, j)),
                  pl.BlockSpec((1, 1, tn), lambda i, j: (i // per_seq, 0, j))],
        out_specs=pl.BlockSpec((tm, tn), lambda i, j: (i, j)),
        out_shape=jax.ShapeDtypeStruct((t, d), F32),
        compiler_params=_cparams("parallel", "parallel"),
        name="out_proj_residual",
    )(a1, a2, w_bf16[:k1], w_bf16[k1:], x2, gate[:, None, :])


def _dot_nt(a, b):
    return lax.dot_general(a, b, (((1,), (1,)), ((), ())), preferred_element_type=F32)


def _dot_tn(a, b):
    return lax.dot_general(a, b, (((0,), (0,)), ((), ())), preferred_element_type=F32)


def _retention_kernel(q_ref, k_ref, v_ref, g_ref, intra_ref, qdec_ref, kdec_ref, cdec_ref, norm_ref,
                      o_ref, state):
    @pl.when(pl.program_id(2) == 0)
    def _():
        state[...] = jnp.zeros_like(state)

    n_chunks = q_ref.shape[0] // CHUNK
    intra = intra_ref[0]
    qdec = qdec_ref[0]
    kdec = kdec_ref[0]
    cdec = cdec_ref[0]
    chunk_rows = [pl.ds(c * CHUNK, CHUNK) for c in range(n_chunks)]
    qs = [q_ref[rows, :] for rows in chunk_rows]
    ks = [k_ref[rows, :] * (HEAD_DIM ** -0.5) for rows in chunk_rows]
    vs = [v_ref[rows, :] for rows in chunk_rows]
    scores = [_dot_nt(qc, kc) * intra for qc, kc in zip(qs, ks)]
    updates = [_dot_tn(kc * kdec, vc) for kc, vc in zip(ks, vs)]
    states = [state[...]]
    for upd in updates:
        states.append(states[-1] * cdec + upd)
    state[...] = states[-1]
    for rows, qc, vc, s, st in zip(chunk_rows, qs, vs, scores, states):
        o = jnp.dot(s, vc, preferred_element_type=F32) + jnp.dot(qc * qdec, st, preferred_element_type=F32)
        on = o * lax.rsqrt(jnp.mean(o * o, axis=-1, keepdims=True) + EPS) * norm_ref[...]
        g = g_ref[rows, :]
        o_ref[rows, :] = (on * (g * jax.nn.sigmoid(g))).astype(o_ref.dtype)


def retention_mixer(proj, ret_norm, batch, seq, *, ts=2048):
    t = proj.shape[0]
    ts = min(ts, seq)
    per_seq = seq // ts
    h8 = N_HEADS
    pos = jnp.arange(CHUNK, dtype=F32)
    lg = jnp.log1p(-jnp.exp2(-5.0 - jnp.arange(h8, dtype=F32)))[:, None]
    intra = jnp.exp(lg[..., None] * jnp.abs(pos[:, None] - pos[None, :]))
    qdec = jnp.broadcast_to(jnp.exp(lg * (pos + 1.0))[..., None], (h8, CHUNK, HEAD_DIM))
    kdec = jnp.broadcast_to(jnp.exp(lg * (CHUNK - 1.0 - pos))[..., None], (h8, CHUNK, HEAD_DIM))
    cdec = jnp.broadcast_to(jnp.exp(lg * CHUNK)[..., None], (h8, HEAD_DIM, HEAD_DIM))

    def col(off):
        return pl.BlockSpec((ts, HEAD_DIM), lambda b, h, s, off=off: (b * per_seq + s, off + h))

    hspec = lambda shape: pl.BlockSpec((1,) + shape, lambda b, h, s: (h, 0, 0))
    return pl.pallas_call(
        _retention_kernel,
        grid=(batch, h8, per_seq),
        in_specs=[col(0), col(h8), col(2 * h8), col(3 * h8),
                  hspec((CHUNK, CHUNK)), hspec((CHUNK, HEAD_DIM)), hspec((CHUNK, HEAD_DIM)),
                  hspec((HEAD_DIM, HEAD_DIM)),
                  pl.BlockSpec((1, HEAD_DIM), lambda b, h, s: (0, 0))],
        out_specs=pl.BlockSpec((ts, HEAD_DIM), lambda b, h, s: (b * per_seq + s, h)),
        out_shape=jax.ShapeDtypeStruct((t, h8 * HEAD_DIM), BF16),
        scratch_shapes=[pltpu.VMEM((HEAD_DIM, HEAD_DIM), F32)],
        compiler_params=_cparams("parallel", "parallel", "arbitrary"),
        name="retention",
    )(proj, proj, proj, proj, intra, qdec, kdec, cdec, ret_norm.reshape(1, HEAD_DIM))


ATTN_TQ = 256


def _diag_mask(tq):
    kc = lax.broadcasted_iota(jnp.int32, (tq, tq), 0) // CHUNK
    qc = lax.broadcasted_iota(jnp.int32, (tq, tq), 1) // CHUNK
    return kc <= qc


def _softmax_t(s_t, scale, mask):
    tq = mask.shape[0]
    ke = s_t.shape[0]
    y = s_t * (scale * math.log2(math.e))
    diag = jnp.where(mask, y[ke - tq:], NEG_INF)
    y = diag if ke == tq else jnp.concatenate([y[:ke - tq], diag], axis=0)
    e = jnp.exp2(y - jnp.max(y, axis=0, keepdims=True))
    return e, 1.0 / jnp.sum(e, axis=0, keepdims=True)


def _diff_attn_kernel(lam_ref, q_ref, k_ref, v_ref, norm_ref, o_ref, *, tq, lambda_init):
    seq = q_ref.shape[0]
    k = k_ref[...]
    half = lax.broadcasted_iota(jnp.int32, k.shape, 1) < (HEAD_DIM // 2)
    k_maps = (jnp.where(half, k, 0.0).astype(BF16), jnp.where(half, 0.0, k).astype(BF16))
    v_t = v_ref[...].T.astype(BF16)
    scale = (HEAD_DIM // 2) ** -0.5
    dl = lam_ref[...]
    lam = (jnp.exp(jnp.sum(dl[0:1] * dl[1:2], axis=-1, keepdims=True))
           - jnp.exp(jnp.sum(dl[2:3] * dl[3:4], axis=-1, keepdims=True)) + lambda_init)
    mask = _diag_mask(tq)
    blocks = [slice(qb * tq, (qb + 1) * tq) for qb in range(seq // tq)]
    qs = [q_ref[rows, :].astype(BF16) for rows in blocks]
    scores = [[_dot_nt(km[:rows.stop], q) for km in k_maps] for rows, q in zip(blocks, qs)]
    probs = [[_softmax_t(s_t, scale, mask) for s_t in pair] for pair in scores]
    for rows, ((e0, r0), (e1, r1)) in zip(blocks, probs):
        w_t = (e0 * r0 - e1 * (lam * r1)).astype(BF16)
        o = jnp.dot(v_t[:, :rows.stop], w_t, preferred_element_type=F32).T
        on = o * lax.rsqrt(jnp.mean(o * o, axis=-1, keepdims=True) + EPS) * norm_ref[...]
        o_ref[rows, :] = (on * (1.0 - lambda_init)).astype(o_ref.dtype)


def diff_attention_mixer(proj, diff_lambda, diff_norm, lambda_init, batch, seq):
    t = proj.shape[0]
    col = lambda off: pl.BlockSpec((seq, HEAD_DIM), lambda b, h, off=off: (b, off + h))
    return pl.pallas_call(
        functools.partial(_diff_attn_kernel, tq=min(ATTN_TQ, seq), lambda_init=lambda_init),
        grid=(batch, N_HEADS),
        in_specs=[pl.BlockSpec(diff_lambda.shape, lambda b, h: (0, 0)),
                  col(4 * N_HEADS), col(5 * N_HEADS), col(6 * N_HEADS),
                  pl.BlockSpec((1, HEAD_DIM), lambda b, h: (0, 0))],
        out_specs=pl.BlockSpec((seq, HEAD_DIM), lambda b, h: (b, h)),
        out_shape=jax.ShapeDtypeStruct((t, N_HEADS * HEAD_DIM), BF16),
        compiler_params=_cparams("parallel", "parallel"),
        name="diff_attention",
    )(diff_lambda, proj, proj, proj, diff_norm.reshape(1, HEAD_DIM))


def _split_mod(mod):
    return jnp.split(mod, 6, axis=-1)


def even_layer_mixer(x2, mod, p, layer, j, batch, seq):
    sh_m, sc_m, gt_m, _, _, _ = _split_mod(mod)
    proj = norm_matmul(x2, p["norm_mix"][layer], sc_m, sh_m, p["even_w_in"][j].astype(BF16), seq)
    ret = retention_mixer(proj, p["ret_norm"][j], batch, seq)
    lambda_init = 0.8 - 0.6 * math.exp(-0.3 * layer)
    dif = diff_attention_mixer(proj, p["diff_lambda"][j], p["diff_norm"][j], lambda_init, batch, seq)
    return out_proj_residual(ret, dif, p["even_w_out"][j].astype(BF16), x2, gt_m, seq)


ODD_COLS = 40 * LANES
GATE_BLOCK = 4 * N_HEADS


def _odd_w_in_layout(w):
    d = w.shape[0]
    hd = N_HEADS * HEAD_DIM
    zeros = lambda n: jnp.zeros((d, n), w.dtype)
    o = 4 * hd
    gates = w[:, o:o + 2 * N_HEADS]
    o += 2 * N_HEADS
    cq = w[:, o:o + MLA_Q_RANK]
    o += MLA_Q_RANK
    ckv = w[:, o:o + MLA_KV_RANK]
    o += MLA_KV_RANK
    kr = w[:, o:o + MLA_ROPE]
    return jnp.concatenate([w[:, :4 * hd], gates, zeros(LANES - 2 * N_HEADS), cq, ckv, kr,
                            zeros(LANES - MLA_ROPE), zeros(LANES)], axis=1)


def _silu(x):
    return x * jax.nn.sigmoid(x)


def _gdn_conv_kernel(x_ref, w_ref, o_ref):
    c = pl.program_id(1)
    x = x_ref[...]
    w = w_ref[...]
    width = w.shape[0]
    row = lax.broadcasted_iota(jnp.int32, x.shape, 0)
    y = x * w[width - 1:width]
    for sft in range(1, width):
        xs = jnp.where(row >= sft, pltpu.roll(x, sft, axis=0), 0.0)
        y = y + xs * w[width - 1 - sft:width - sft]
    y = _silu(y)
    yn = y * lax.rsqrt(jnp.sum(y * y, axis=-1, keepdims=True) + EPS)
    yn = yn * jnp.where(c < N_HEADS, HEAD_DIM ** -0.5, 1.0)
    o_ref[...] = jnp.where(c < 2 * N_HEADS, yn, y)


def gdn_conv(proj, conv_w, batch, seq):
    t = proj.shape[0]
    n_blk = 3 * N_HEADS
    return pl.pallas_call(
        _gdn_conv_kernel,
        grid=(batch, n_blk),
        in_specs=[pl.BlockSpec((seq, LANES), lambda b, c: (b, c)),
                  pl.BlockSpec((conv_w.shape[0], LANES), lambda b, c: (0, c))],
        out_specs=pl.BlockSpec((seq, LANES), lambda b, c: (b, c)),
        out_shape=jax.ShapeDtypeStruct((t, n_blk * LANES), F32),
        compiler_params=_cparams("parallel", "parallel"),
        name="gdn_conv",
    )(proj, conv_w)


def _dot_hi(a, b):
    def split(x):
        hi = x.astype(BF16)
        return hi, (x - hi.astype(F32)).astype(BF16)

    a_hi, a_lo = split(a)
    b_hi, b_lo = split(b)
    dot = functools.partial(jnp.dot, preferred_element_type=F32)
    return dot(a_hi, b_hi) + (dot(a_hi, b_lo) + dot(a_lo, b_hi))


def _unit_lower_inverses(l_mats):
    n = l_mats[0].shape[0]
    eye = (lax.broadcasted_iota(jnp.int32, (n, n), 0) == lax.broadcasted_iota(jnp.int32, (n, n), 1)).astype(F32)
    invs = [eye - l for l in l_mats]
    powers = [_dot_hi(l, l) for l in l_mats]
    span = 2
    while span < n:
        invs = [inv + _dot_hi(inv, p) for inv, p in zip(invs, powers)]
        span *= 2
        if span < n:
            powers = [_dot_hi(p, p) for p in powers]
    return invs


def _pick_lane(x, lane):
    sel = lax.broadcasted_iota(jnp.int32, x.shape, 1) == lane
    return jnp.sum(jnp.where(sel, x, 0.0), axis=-1, keepdims=True)


HEADS_PER_STEP = 4


def _gdn_kernel(q_ref, k_ref, v_ref, gg_ref, gate_ref, alog_ref, dtb_ref, norm_ref, o_ref, state):
    @pl.when(pl.program_id(2) == 0)
    def _():
        state[...] = jnp.zeros_like(state)

    n_chunks = q_ref.shape[0] // CHUNK
    ri = lax.broadcasted_iota(jnp.int32, (CHUNK, CHUNK), 0)
    ci = lax.broadcasted_iota(jnp.int32, (CHUNK, CHUNK), 1)
    tri = ci <= ri
    strict = ci < ri
    tri_f = tri.astype(F32)
    heads = [pl.program_id(1) * HEADS_PER_STEP + r for r in range(HEADS_PER_STEP)]
    a_coefs = [-jnp.exp(_pick_lane(alog_ref[...], h)) for h in heads]
    dt_biases = [_pick_lane(dtb_ref[...], h) for h in heads]
    units = [(pl.ds(c * CHUNK, CHUNK), slice(r * HEAD_DIM, (r + 1) * HEAD_DIM), r)
             for c in range(n_chunks) for r in range(HEADS_PER_STEP)]

    def local(rows, cols, r):
        k = k_ref[rows, cols]
        raw = gate_ref[rows, :]
        beta = jax.nn.sigmoid(_pick_lane(raw, heads[r]))
        ga = _pick_lane(raw, N_HEADS + heads[r]) + dt_biases[r]
        softplus = jnp.maximum(ga, 0.0) + jnp.log1p(jnp.exp(-jnp.abs(ga)))
        g = jnp.broadcast_to(a_coefs[r] * softplus, (CHUNK, HEAD_DIM))
        gc = _dot_hi(tri_f, g)
        g_rows = gc.T[:CHUNK, :]
        decay = jnp.where(tri, jnp.exp(jnp.where(tri, gc[:, :CHUNK] - g_rows, 0.0)), 0.0)
        k_beta = k * beta
        l_mat = jnp.where(strict, _dot_nt(k_beta, k) * decay, 0.0)
        return k, beta, gc, decay, k_beta, l_mat

    locs = [local(*unit) for unit in units]
    t_invs = _unit_lower_inverses([loc[-1] for loc in locs])
    preps = []
    for (rows, cols, _), (k, beta, gc, decay, k_beta, _), t_inv in zip(units, locs, t_invs):
        q = q_ref[rows, cols]
        eg = jnp.exp(gc)
        u = jnp.dot(t_inv, v_ref[rows, cols] * beta, preferred_element_type=F32)
        w = jnp.dot(t_inv, k_beta * eg, preferred_element_type=F32)
        g_last = gc[CHUNK - 1:CHUNK, :]
        preps.append((u, w, _dot_nt(q, k) * decay, q * eg, k * jnp.exp(g_last - gc), jnp.exp(g_last)))

    for (rows, cols, r), (u, w, intra, q_dec, k_dec, chunk_dec) in zip(units, preps):
        st = state[r]
        v_new = u - jnp.dot(w, st, preferred_element_type=F32)
        o = jnp.dot(q_dec, st, preferred_element_type=F32) + jnp.dot(intra, v_new, preferred_element_type=F32)
        state[r] = st * chunk_dec + _dot_tn(k_dec, v_new)
        on = o * lax.rsqrt(jnp.mean(o * o, axis=-1, keepdims=True) + EPS) * norm_ref[...]
        o_ref[rows, cols] = (on * _silu(gg_ref[rows, cols])).astype(o_ref.dtype)


def gdn_mixer(qkv, proj, a_log, dt_bias, gdn_norm, batch, seq, *, ts=512):
    t = qkv.shape[0]
    ts = min(ts, seq)
    per_seq = seq // ts
    hps = HEADS_PER_STEP
    pad = lambda v: jnp.pad(v.reshape(1, -1), ((0, 0), (0, LANES - v.shape[0])))

    def col(off):
        return pl.BlockSpec((ts, hps * HEAD_DIM), lambda b, h, s, off=off: (b * per_seq + s, off // hps + h))

    row1 = pl.BlockSpec((1, LANES), lambda b, h, s: (0, 0))
    return pl.pallas_call(
        _gdn_kernel,
        grid=(batch, N_HEADS // hps, per_seq),
        in_specs=[col(0), col(N_HEADS), col(2 * N_HEADS), col(3 * N_HEADS),
                  pl.BlockSpec((ts, LANES), lambda b, h, s: (b * per_seq + s, GATE_BLOCK)),
                  row1, row1, row1],
        out_specs=pl.BlockSpec((ts, hps * HEAD_DIM), lambda b, h, s: (b * per_seq + s, h)),
        out_shape=jax.ShapeDtypeStruct((t, N_HEADS * HEAD_DIM), BF16),
        scratch_shapes=[pltpu.VMEM((hps, HEAD_DIM, HEAD_DIM), F32)],
        compiler_params=_cparams("parallel", "parallel", "arbitrary"),
        name="gdn",
    )(qkv, qkv, qkv, proj, proj, pad(a_log), pad(dt_bias), gdn_norm.reshape(1, HEAD_DIM))


def _rope_tables(seq):
    half = MLA_ROPE // 2
    inv_freq = ROPE_THETA ** (-jnp.arange(half, dtype=F32) / half)
    ang = jnp.arange(seq, dtype=jnp.int32).astype(F32)[:, None] * inv_freq[None, :]
    cos, sin = jnp.cos(ang), jnp.sin(ang)
    z = jnp.zeros((seq, LANES - MLA_ROPE), F32)
    zh = jnp.zeros((seq, half), F32)
    return (jnp.concatenate([cos, cos, z], axis=1),
            jnp.concatenate([zh, sin, z], axis=1),
            jnp.concatenate([-sin, zh, z], axis=1))


def _rope(y, cos, sin_up, sin_dn):
    half = MLA_ROPE // 2
    return y * cos + pltpu.roll(y, half, axis=1) * sin_up + pltpu.roll(y, LANES - half, axis=1) * sin_dn


def _mla_proj_kernel(cq_ref, ckv_ref, kr_ref, qg_ref, kvg_ref, wqn_ref, wqr_ref, wkv_ref, cos_ref, sup_ref, sdn_ref,
                     qn_ref, qr_ref, kv_ref, kro_ref):
    def rms(x, g):
        return x * lax.rsqrt(jnp.mean(x * x, axis=-1, keepdims=True) + EPS) * g

    cos, sup, sdn = cos_ref[...], sup_ref[...], sdn_ref[...]
    cq = rms(cq_ref[...], qg_ref[...]).astype(BF16)
    qn_ref[...] = jnp.dot(cq, wqn_ref[...], preferred_element_type=F32).astype(qn_ref.dtype)
    qr = jnp.dot(cq, wqr_ref[...], preferred_element_type=F32)
    for h in range(N_HEADS):
        cols = slice(h * LANES, (h + 1) * LANES)
        qr_ref[:, cols] = _rope(qr[:, cols], cos, sup, sdn).astype(qr_ref.dtype)
    ckv = rms(ckv_ref[...], kvg_ref[...]).astype(BF16)
    kv_ref[...] = jnp.dot(ckv, wkv_ref[...], preferred_element_type=F32).astype(kv_ref.dtype)
    kro_ref[...] = _rope(kr_ref[...], cos, sup, sdn).astype(kro_ref.dtype)


def mla_projection(proj, q_norm, w_uq, kv_norm, w_ukv, seq, *, tm=512):
    t = proj.shape[0]
    tm = min(tm, seq)
    per_seq = seq // tm
    hd = N_HEADS * HEAD_DIM
    w3 = w_uq.reshape(MLA_Q_RANK, N_HEADS, HEAD_DIM + MLA_ROPE)
    wqn = w3[:, :, :HEAD_DIM].reshape(MLA_Q_RANK, hd).astype(BF16)
    wqr = jnp.pad(w3[:, :, HEAD_DIM:], ((0, 0), (0, 0), (0, LANES - MLA_ROPE))).reshape(MLA_Q_RANK, hd).astype(BF16)
    cos, sup, sdn = _rope_tables(seq)
    full = lambda a: pl.BlockSpec(a.shape, lambda i: (0,) * a.ndim)
    tab = pl.BlockSpec((tm, LANES), lambda i: (i % per_seq, 0))
    wkv = w_ukv.astype(BF16)
    qg, kvg = q_norm.reshape(1, -1), kv_norm.reshape(1, -1)
    return pl.pallas_call(
        _mla_proj_kernel,
        grid=(t // tm,),
        in_specs=[pl.BlockSpec((tm, MLA_Q_RANK), lambda i: (i, (GATE_BLOCK + 1) * LANES // MLA_Q_RANK)),
                  pl.BlockSpec((tm, MLA_KV_RANK), lambda i: (i, (GATE_BLOCK + 4) * LANES // MLA_KV_RANK)),
                  pl.BlockSpec((tm, LANES), lambda i: (i, GATE_BLOCK + 6)),
                  full(qg), full(kvg), full(wqn), full(wqr), full(wkv), tab, tab, tab],
        out_specs=[pl.BlockSpec((tm, hd), lambda i: (i, 0)), pl.BlockSpec((tm, hd), lambda i: (i, 0)),
                   pl.BlockSpec((tm, 2 * hd), lambda i: (i, 0)), pl.BlockSpec((tm, LANES), lambda i: (i, 0))],
        out_shape=[jax.ShapeDtypeStruct((t, hd), BF16), jax.ShapeDtypeStruct((t, hd), BF16),
                   jax.ShapeDtypeStruct((t, 2 * hd), BF16), jax.ShapeDtypeStruct((t, LANES), BF16)],
        compiler_params=_cparams("parallel"),
        name="mla_projection",
    )(proj, proj, proj, qg, kvg, wqn, wqr, wkv, cos, sup, sdn)


def _mla_attn_kernel(qn_ref, qr_ref, kn_ref, kr_ref, v_ref, o_ref, *, tq):
    seq = qn_ref.shape[0]
    kn = kn_ref[...]
    kr = kr_ref[...]
    v_t = v_ref[...].T
    scale = (HEAD_DIM + MLA_ROPE) ** -0.5
    mask = _diag_mask(tq)
    blocks = [slice(qb * tq, (qb + 1) * tq) for qb in range(seq // tq)]
    scores = [_dot_nt(kn[:rows.stop], qn_ref[rows, :]) + _dot_nt(kr[:rows.stop], qr_ref[rows, :]) for rows in blocks]
    probs = [_softmax_t(s_t, scale, mask) for s_t in scores]
    for rows, (e, r) in zip(blocks, probs):
        o_t = jnp.dot(v_t[:, :rows.stop], e.astype(BF16), preferred_element_type=F32) * r
        o_ref[rows, :] = o_t.T.astype(o_ref.dtype)


def mla_attention_mixer(qn, qr, kv, kr, batch, seq):
    t = qn.shape[0]
    qspec = pl.BlockSpec((seq, HEAD_DIM), lambda b, h: (b, h))
    return pl.pallas_call(
        functools.partial(_mla_attn_kernel, tq=min(ATTN_TQ, seq)),
        grid=(batch, N_HEADS),
        in_specs=[qspec, qspec,
                  pl.BlockSpec((seq, HEAD_DIM), lambda b, h: (b, 2 * h)),
                  pl.BlockSpec((seq, LANES), lambda b, h: (b, 0)),
                  pl.BlockSpec((seq, HEAD_DIM), lambda b, h: (b, 2 * h + 1))],
        out_specs=qspec,
        out_shape=jax.ShapeDtypeStruct((t, N_HEADS * HEAD_DIM), BF16),
        compiler_params=_cparams("parallel", "parallel"),
        name="mla_attention",
    )(qn, qr, kv, kr, kv)


def odd_layer_mixer(x2, mod, p, layer, j, batch, seq):
    sh_m, sc_m, gt_m, _, _, _ = _split_mod(mod)
    w_in = _odd_w_in_layout(p["odd_w_in"][j]).astype(BF16)
    proj = norm_matmul(x2, p["norm_mix"][layer], sc_m, sh_m, w_in, seq)
    qkv = gdn_conv(proj, p["gdn_conv"][j], batch, seq)
    gdn = gdn_mixer(qkv, proj, p["gdn_a_log"][j], p["gdn_dt_bias"][j], p["gdn_norm"][j], batch, seq)
    qn, qr, kv, kr = mla_projection(proj, p["mla_q_norm"][j], p["mla_w_uq"][j], p["mla_kv_norm"][j],
                                    p["mla_w_ukv"][j], seq)
    mla = mla_attention_mixer(qn, qr, kv, kr, batch, seq)
    return out_proj_residual(gdn, mla, p["odd_w_out"][j].astype(BF16), x2, gt_m, seq)


ROUTE_TOKENS = 128
ROUTE_HEADS_PER_STEP = 4


def _top16(*problems):
    n, tt = problems[0].shape
    row = lax.broadcasted_iota(jnp.int32, (n, tt), 0).astype(F32)
    r16 = lax.broadcasted_iota(jnp.int32, (PEER_TOPK, tt), 0)

    def body(k, carry):
        hit = r16 == k
        out = []
        for s, vals, ids in carry:
            m = jnp.max(s, axis=0, keepdims=True)
            i = jnp.min(jnp.where(s == m, row, float(n)), axis=0, keepdims=True)
            out.append((jnp.where(row == i, NEG_INF, s), jnp.where(hit, m, vals), jnp.where(hit, i, ids)))
        return tuple(out)

    zeros = jnp.zeros((PEER_TOPK, tt), F32)
    res = lax.fori_loop(0, PEER_TOPK, body, tuple((s, zeros, zeros) for s in problems))
    res = [(vals, ids) for _, vals, ids in res]
    return res[0] if len(problems) == 1 else res


def _pruned_candidates(v1, v2):
    tt = v1.shape[1]
    sub = lax.broadcasted_iota(jnp.int32, (SUBLANES, tt), 0)
    row = lambda v, r: v[r:r + 1, :]
    blocks = [row(v1, 0) + v2[:SUBLANES], row(v1, 0) + v2[SUBLANES:]]
    blocks += [row(v1, a) + v2[:SUBLANES] for a in range(1, 5)]
    a567 = jnp.where(sub < 2, row(v1, 5), jnp.where(sub < 4, row(v1, 6), row(v1, 7)))
    b01 = jnp.where((sub & 1) == 0, row(v2, 0), row(v2, 1))
    blocks.append(jnp.where(sub < 6, a567 + b01, NEG_INF))
    blocks.append(v1[SUBLANES:] + row(v2, 0))
    return jnp.concatenate(blocks, axis=0)


def _candidate_ranks(pos):
    p16 = pos - 16.0
    p48 = pos - 48.0
    a_mid = 1.0 + jnp.floor(p16 * 0.125)
    a_hi = 5.0 + jnp.floor(p48 * 0.5)
    a = jnp.where(pos < 16.0, 0.0, jnp.where(pos < 48.0, a_mid, jnp.where(pos < 56.0, a_hi, p48)))
    b = jnp.where(pos < 16.0, pos, jnp.where(pos < 48.0, p16 - 8.0 * (a_mid - 1.0),
                                             jnp.where(pos < 56.0, p48 - 2.0 * (a_hi - 5.0), 0.0)))
    return a, b


def _select_rows(table, sel):
    out = jnp.zeros_like(sel)
    for a in range(table.shape[0]):
        out = out + jnp.where(sel == float(a), table[a:a + 1, :], 0.0)
    return out


def _route_kernel(q_ref, keys_ref, idx_ref, gate_ref, cnt_ref, idx_all, gate_all, *, rows_per_expert):
    keys = [keys_ref[half].astype(BF16) for half in range(2)]

    def head_body(it, _):
        heads = [it * ROUTE_HEADS_PER_STEP + r for r in range(ROUTE_HEADS_PER_STEP)]
        tops = []
        for hd in heads:
            scores = []
            for half in range(2):
                col = pl.multiple_of((2 * hd + half) * LANES, LANES)
                scores.append(_dot_nt(keys[half], q_ref[:, pl.ds(col, LANES)].astype(BF16)))
            tops.append(_top16(*scores))
        finals = _top16(*[_pruned_candidates(v1, v2) for (v1, _), (v2, _) in tops])
        if ROUTE_HEADS_PER_STEP == 1:
            finals = [finals]
        for hd, ((_, i1), (_, i2)), (best, pos) in zip(heads, tops, finals):
            a_sel, b_sel = _candidate_ranks(pos)
            expert = _select_rows(i1, a_sel) * N_KEYS + _select_rows(i2, b_sel)
            e = jnp.exp(best - jnp.max(best, axis=0, keepdims=True))
            rows = pl.ds(pl.multiple_of(hd * PEER_TOPK, PEER_TOPK), PEER_TOPK)
            idx_all[rows, :] = expert.astype(jnp.int32)
            gate_all[rows, :] = e / jnp.sum(e, axis=0, keepdims=True)
        return 0

    lax.fori_loop(0, PEER_HEADS // ROUTE_HEADS_PER_STEP, head_body, 0)

    idx = idx_all[...]
    gate = gate_all[...]
    key1 = idx >> KEY_BITS
    key2 = idx & (N_KEYS - 1)
    per_bits = KEY_BITS - TILE_BITS
    tid = (key1 + (key2 >> per_bits)) & (N_EXPERT_TILES - 1)
    local = ((key1 << per_bits) + (key2 & ((1 << per_bits) - 1))) * rows_per_expert
    n = PEER_SLOTS
    before = (lax.broadcasted_iota(jnp.int32, (n, n), 1) < lax.broadcasted_iota(jnp.int32, (n, n), 0)).astype(F32)
    for k in range(N_EXPERT_TILES):
        mem = tid == k
        shift = jnp.dot(before, jnp.where(mem, 0.0, 1.0), preferred_element_type=F32).astype(jnp.int32)
        xi = jnp.where(mem, local, 0)
        xg = jnp.where(mem, gate, 0.0)
        xd = jnp.where(mem, shift, 0)
        for b in range(int(math.log2(n))):
            step = 1 << b
            mv = ((xd >> b) & 1) == 1
            pull = lambda a: pltpu.roll(a, n - step, axis=0)
            inc = pull(mv.astype(jnp.int32)) == 1
            xi = jnp.where(inc, pull(xi), jnp.where(mv, 0, xi))
            xg = jnp.where(inc, pull(xg), jnp.where(mv, 0.0, xg))
            xd = jnp.where(inc, pull(xd), jnp.where(mv, 0, xd))
        idx_ref[k] = xi.T
        gate_ref[k * n:(k + 1) * n, :] = xg
        cnt_ref[k:k + 1, :] = jnp.sum(mem.astype(jnp.int32), axis=0, keepdims=True)


def peer_route(q, sub_keys, rows_per_expert):
    t, d = q.shape
    tt = ROUTE_TOKENS
    n = N_EXPERT_TILES * PEER_SLOTS
    return pl.pallas_call(
        functools.partial(_route_kernel, rows_per_expert=rows_per_expert),
        grid=(t // tt,),
        in_specs=[pl.BlockSpec((tt, d), lambda i: (i, 0)),
                  pl.BlockSpec(sub_keys.shape, lambda i: (0, 0, 0))],
        out_specs=[pl.BlockSpec((N_EXPERT_TILES, tt, PEER_SLOTS), lambda i: (0, i, 0)),
                   pl.BlockSpec((n, tt), lambda i: (0, i)),
                   pl.BlockSpec((N_EXPERT_TILES, tt), lambda i: (0, i))],
        out_shape=[jax.ShapeDtypeStruct((N_EXPERT_TILES, t, PEER_SLOTS), jnp.int32),
                   jax.ShapeDtypeStruct((n, t), F32),
                   jax.ShapeDtypeStruct((N_EXPERT_TILES, t), jnp.int32)],
        scratch_shapes=[pltpu.VMEM((PEER_SLOTS, tt), jnp.int32), pltpu.VMEM((PEER_SLOTS, tt), F32)],
        compiler_params=_cparams("parallel"),
        name="peer_route",
    )(q, sub_keys)


EXPERT_TOKENS = 128


def _n_groups(cnt):
    return lax.shift_right_logical(cnt + (SLOT_GROUP - 1), int(math.log2(SLOT_GROUP)))


STATIC_SLOTS = 40
STATIC_GROUPS = STATIC_SLOTS // SLOT_GROUP
FOLD_ORDER = (0, 4, 2, 6, 1, 5, 3, 7)
TOKEN_UNROLL = 4


def _rows_of(i, n):
    return pl.ds(pl.multiple_of(i * n, n), n)


def _fold_group(parts):
    sub = lax.broadcasted_iota(jnp.int32, parts[0].shape, 0)
    xs = [parts[i] for i in FOLD_ORDER]
    half = SUBLANES // 2
    while half >= 1:
        keep = (sub & half) == 0
        xs = [jnp.where(keep, x, pltpu.roll(y, half, axis=0)) + jnp.where(keep, pltpu.roll(x, SUBLANES - half, axis=0), y)
              for x, y in zip(xs[0::2], xs[1::2])]
        half //= 2
    return xs[0]


def _down_kernel(idx_ref, cnt_ref, h_ref, gate_ref, tbl_ref, coef_ref, act_ref):
    tb = act_ref.shape[1]
    rpe = h_ref.shape[0] // tb
    lane = lax.broadcasted_iota(jnp.int32, (SUBLANES, tb), 1)

    def partial(off, hv):
        p = tbl_ref[pl.ds(pl.multiple_of(off, rpe), rpe), :] * hv
        out = p[:SUBLANES]
        for c in range(1, rpe // SUBLANES):
            out = out + p[c * SUBLANES:(c + 1) * SUBLANES]
        return out

    def group(t, g, hv):
        return _fold_group([partial(idx_ref[t, g * SLOT_GROUP + u], hv) for u in range(SLOT_GROUP)])

    act_ref[STATIC_SLOTS:, :] = jnp.zeros((PEER_SLOTS - STATIC_SLOTS, tb), F32)

    def overflow(t, hv):
        n = cnt_ref[0, 0, t]

        @pl.when(n > STATIC_SLOTS)
        def _():
            def group_body(g, _):
                rows = _rows_of(g, SLOT_GROUP)
                col = jnp.sum(group(t, g, hv), axis=-1, keepdims=True)
                act_ref[rows, :] = jnp.where(lane == t, col, act_ref[rows, :])
                return 0

            lax.fori_loop(STATIC_GROUPS, _n_groups(n), group_body, 0)

    def absorb(accs, folds, first_token):
        accs = list(accs)
        for r in range(TOKEN_UNROLL):
            for g in range(STATIC_GROUPS):
                col = jnp.sum(folds[r * STATIC_GROUPS + g], axis=-1, keepdims=True)
                accs[g] = jnp.where(lane == first_token + r, col, accs[g])
        return tuple(accs)

    def tok_body(i, carry):
        accs, folds = carry
        accs = absorb(accs, folds, (i - 1) * TOKEN_UNROLL)
        tokens = [i * TOKEN_UNROLL + r for r in range(TOKEN_UNROLL)]
        hvs = [h_ref[_rows_of(t, rpe), :] for t in tokens]
        folds = tuple(group(t, g, hv) for t, hv in zip(tokens, hvs) for g in range(STATIC_GROUPS))
        for t, hv in zip(tokens, hvs):
            overflow(t, hv)
        return accs, folds

    assert tb == LANES
    zeros = act_ref[STATIC_SLOTS:STATIC_SLOTS + SUBLANES, :]
    n_iter = tb // TOKEN_UNROLL
    init = ((zeros,) * STATIC_GROUPS, (zeros,) * (TOKEN_UNROLL * STATIC_GROUPS))
    accs, folds = lax.fori_loop(0, n_iter, tok_body, init)
    accs = absorb(accs, folds, (n_iter - 1) * TOKEN_UNROLL)
    a = jnp.concatenate(list(accs) + [act_ref[STATIC_SLOTS:, :]], axis=0)
    coef = gate_ref[...] * (0.5 * a * (1.0 + lax.erf(a * (2.0 ** -0.5))))
    coef_ref[...] = coef.T


def _smem_spec(block, index_map):
    return pl.BlockSpec(block, index_map, memory_space=pltpu.SMEM)


def _tile_spec(rpe, index_map):
    return pl.BlockSpec((EXPERT_TILE * rpe, LANES), index_map, pipeline_mode=pl.Buffered(1))


def peer_down_coefs(idx, cnt, h2, gate, table2):
    rpe = table2.shape[0] // (N_KEYS * N_KEYS)
    t = h2.shape[0] // rpe
    tb = EXPERT_TOKENS
    n_blk = t // tb
    return pl.pallas_call(
        _down_kernel,
        grid=(N_EXPERT_TILES, n_blk),
        in_specs=[_smem_spec((tb, PEER_SLOTS), lambda k, i: (k * n_blk + i, 0)),
                  _smem_spec((1, 1, tb), lambda k, i: (k, 0, i)),
                  pl.BlockSpec((tb * rpe, LANES), lambda k, i: (i, 0)),
                  pl.BlockSpec((PEER_SLOTS, tb), lambda k, i: (k, i)),
                  _tile_spec(rpe, lambda k, i: (k, 0))],
        out_specs=pl.BlockSpec((tb, PEER_SLOTS), lambda k, i: (k * n_blk + i, 0)),
        out_shape=jax.ShapeDtypeStruct((N_EXPERT_TILES * t, PEER_SLOTS), F32),
        scratch_shapes=[pltpu.VMEM((PEER_SLOTS, tb), F32)],
        compiler_params=_cparams("arbitrary", "arbitrary"),
        name="peer_down",
    )(idx, cnt, h2, gate, table2)


def _up_kernel(idx_ref, coef_ref, cnt_ref, x_ref, gt_ref, tbl_ref, o_ref):
    tb = cnt_ref.shape[-1]
    rpe = x_ref.shape[0] // tb
    gt = gt_ref[...]

    def term(t, j):
        return coef_ref[t, j] * tbl_ref[pl.ds(pl.multiple_of(idx_ref[t, j], rpe), rpe), :]

    def token_body(t, _):
        accs = [jnp.zeros((rpe, LANES), F32), jnp.zeros((rpe, LANES), F32)]
        for j in range(STATIC_SLOTS):
            accs[j % 2] = accs[j % 2] + term(t, j)

        def group_body(g, acc):
            for u in range(SLOT_GROUP):
                acc = acc + term(t, g * SLOT_GROUP + u)
            return acc

        n_groups = jnp.maximum(_n_groups(cnt_ref[0, 0, t]), STATIC_GROUPS)
        acc = lax.fori_loop(STATIC_GROUPS, n_groups, group_body, accs[0] + accs[1])
        rows = _rows_of(t, rpe)
        o_ref[rows, :] = x_ref[rows, :] + gt * acc
        return 0

    lax.fori_loop(0, tb, token_body, 0)


def peer_up_tile(k, idx, coef, cnt, x2r, gate2r, table2, seq):
    rpe = table2.shape[0] // (N_KEYS * N_KEYS)
    t = x2r.shape[0] // rpe
    tb = min(EXPERT_TOKENS, seq)
    per_seq = seq // tb
    n_blk = t // tb
    return pl.pallas_call(
        _up_kernel,
        grid=(n_blk,),
        in_specs=[_smem_spec((tb, PEER_SLOTS), lambda i: (k * n_blk + i, 0)),
                  _smem_spec((tb, PEER_SLOTS), lambda i: (k * n_blk + i, 0)),
                  _smem_spec((1, 1, tb), lambda i: (k, 0, i)),
                  pl.BlockSpec((tb * rpe, LANES), lambda i: (i, 0)),
                  pl.BlockSpec((rpe, LANES), lambda i: (i // per_seq, 0)),
                  _tile_spec(rpe, lambda i: (k, 0))],
        out_specs=pl.BlockSpec((tb * rpe, LANES), lambda i: (i, 0)),
        out_shape=jax.ShapeDtypeStruct(x2r.shape, F32),
        compiler_params=_cparams("arbitrary"),
        name="peer_up",
    )(idx, coef, cnt, x2r, gate2r, table2)


def _tile_tables_kernel(x_ref, o_ref):
    nt = N_EXPERT_TILES
    per = N_KEYS // nt
    rpe = x_ref.shape[1] // LANES
    key1 = pl.program_id(1)
    for r in range(nt):
        tile = (key1 + r) % nt
        for c in range(rpe):
            o_ref[tile, pl.ds(c, per, stride=rpe), :] = x_ref[r * per:(r + 1) * per, c * LANES:(c + 1) * LANES]


def tile_tables(tables):
    depth, e, d = tables.shape
    nt = N_EXPERT_TILES
    per = N_KEYS // nt
    rpe = d // LANES
    out = pl.pallas_call(
        _tile_tables_kernel,
        grid=(depth, N_KEYS),
        in_specs=[pl.BlockSpec((None, N_KEYS, d), lambda l, k1: (l, k1, 0))],
        out_specs=pl.BlockSpec((None, nt, None, per * rpe, LANES), lambda l, k1: (l, 0, k1, 0, 0)),
        out_shape=jax.ShapeDtypeStruct((depth, nt, N_KEYS, per * rpe, LANES), tables.dtype),
        compiler_params=_cparams("parallel", "parallel"),
        name="tile_tables",
    )(tables)
    return out.reshape(depth, e * rpe, LANES)


def peer_layer(x2, mod, p, layer, batch, seq):
    _, _, _, sh_f, sc_f, gt_f = _split_mod(mod)
    t, d = x2.shape
    rpe = d // LANES
    q, h = norm_matmul(x2, p["norm_ffn"][layer], sc_f, sh_f, p["peer_w_query"][layer].astype(BF16), seq,
                       emit_h=True, tn=512)
    idx, gate, cnt = peer_route(q, p["peer_sub_keys"][layer], rpe)
    idx = idx.reshape(N_EXPERT_TILES * t, PEER_SLOTS)
    cnt = cnt.reshape(N_EXPERT_TILES, 1, t)
    coef = peer_down_coefs(idx, cnt, h.reshape(t * rpe, LANES), gate, p["peer_down_rows"][layer])
    up2 = p["peer_up_rows"][layer]
    xr = x2.reshape(t * rpe, LANES)
    gtr = gt_f.reshape(batch * rpe, LANES)
    for k in range(N_EXPERT_TILES):
        xr = peer_up_tile(k, idx, coef, cnt, xr, gtr, up2, seq)
    return xr.reshape(t, d)


def _final_norm_kernel(x_ref, g_ref, o_ref):
    x = x_ref[...]
    o_ref[...] = x * lax.rsqrt(jnp.mean(x * x, axis=-1, keepdims=True) + EPS) * g_ref[...]


def final_rmsnorm(x2, gain, *, tm=512):
    t, d = x2.shape
    return pl.pallas_call(
        _final_norm_kernel,
        grid=(t // tm,),
        in_specs=[pl.BlockSpec((tm, d), lambda i: (i, 0)), pl.BlockSpec((1, d), lambda i: (0, 0))],
        out_specs=pl.BlockSpec((tm, d), lambda i: (i, 0)),
        out_shape=jax.ShapeDtypeStruct((t, d), F32),
        compiler_params=_cparams("parallel"),
        name="final_norm",
    )(x2, gain.reshape(1, d))


def kernel(x, c, norm_mix, norm_ffn, ada_w, ada_b, even_w_in, even_w_out, ret_norm, diff_lambda, diff_norm, odd_w_in, odd_w_out, gdn_conv, gdn_a_log, gdn_dt_bias, gdn_norm, mla_q_norm, mla_w_uq, mla_kv_norm, mla_w_ukv, peer_w_query, peer_sub_keys, peer_down, peer_up, final_norm):
    p = dict(norm_mix=norm_mix, norm_ffn=norm_ffn, even_w_in=even_w_in, even_w_out=even_w_out, ret_norm=ret_norm,
             diff_lambda=diff_lambda, diff_norm=diff_norm, odd_w_in=odd_w_in, odd_w_out=odd_w_out, gdn_conv=gdn_conv,
             gdn_a_log=gdn_a_log, gdn_dt_bias=gdn_dt_bias, gdn_norm=gdn_norm, mla_q_norm=mla_q_norm,
             mla_w_uq=mla_w_uq, mla_kv_norm=mla_kv_norm, mla_w_ukv=mla_w_ukv, peer_w_query=peer_w_query,
             peer_sub_keys=peer_sub_keys, peer_down_rows=tile_tables(peer_down), peer_up_rows=tile_tables(peer_up))
    batch, seq, d = x.shape
    depth = ada_w.shape[0]
    x2 = x.reshape(batch * seq, d)
    mod = modulation(c, ada_w, ada_b)
    for layer in range(depth):
        if layer % 2 == 0:
            x2 = even_layer_mixer(x2, mod[layer], p, layer, layer // 2, batch, seq)
        else:
            x2 = odd_layer_mixer(x2, mod[layer], p, layer, layer // 2, batch, seq)
        x2 = peer_layer(x2, mod[layer], p, layer, batch, seq)
    return final_rmsnorm(x2, final_norm).reshape(batch, seq, d)
```
